```python
import math
import jax, jax.numpy as jnp
from jax import lax
import numpy as np

D_MODEL = 1024
BATCH = 32
SEQ = 2048
DEPTH = 1

EPS = 1e-6
NEG = -1e30
D_FF = 2816

MLA_HEADS = 8
Q_LORA = 256
KV_LORA = 128
QK_NOPE = 64
QK_ROPE = 32
V_HEAD = 64
ROPE_THETA = 10000.0
Q_BLOCK = 128

SWA_HEADS = 8
SWA_KV_HEADS = 2
SWA_HEAD_DIM = 64
WINDOW = 128

REL_BUCKETS = 32
REL_MAX_DIST = 128

MLA_OUT = MLA_HEADS * V_HEAD
SWA_OUT = SWA_HEADS * SWA_HEAD_DIM
D_MIX = MLA_OUT + SWA_OUT
IN_WIDTHS = (Q_LORA, KV_LORA + QK_ROPE, SWA_HEADS * SWA_HEAD_DIM,
             SWA_KV_HEADS * SWA_HEAD_DIM, SWA_KV_HEADS * SWA_HEAD_DIM)
D_IN = sum(IN_WIDTHS)
IN_SPLITS = tuple(int(s) for s in np.cumsum(IN_WIDTHS)[:-1])

kernel_name = "hymba_mla_swa_macaron_t5"


def rmsnorm(x, g):
    xf = x.astype(jnp.float32)
    y = xf * lax.rsqrt(jnp.mean(xf * xf, axis=-1, keepdims=True) + EPS)
    return (y * g.astype(jnp.float32)).astype(x.dtype)


def swiglu(x, w_gate, w_up, w_down):
    return (jax.nn.silu(x @ w_gate) * (x @ w_up)) @ w_down


def rope(x, cos, sin):
    half = x.shape[-1] // 2
    xf = x.astype(jnp.float32)
    x1, x2 = xf[..., :half], xf[..., half:]
    return jnp.concatenate([x1 * cos - x2 * sin, x2 * cos + x1 * sin], axis=-1).astype(x.dtype)


def t5_bucket(dist):
    n = jnp.maximum(dist, 0)
    max_exact = REL_BUCKETS // 2
    nf = jnp.maximum(n, 1).astype(jnp.float32)
    large = max_exact + (jnp.log(nf / max_exact) / math.log(REL_MAX_DIST / max_exact)
                         * (REL_BUCKETS - max_exact)).astype(jnp.int32)
    large = jnp.minimum(large, REL_BUCKETS - 1)
    return jnp.where(n < max_exact, n, large)


def mla_group(c_q, ckv_pe, g_q_a, w_q_b, g_kv_a, w_kv_b):
    B, S, _ = c_q.shape
    pos = jnp.arange(S, dtype=jnp.float32)
    inv_freq = ROPE_THETA ** (-jnp.arange(0, QK_ROPE, 2, dtype=jnp.float32) / QK_ROPE)
    ang = pos[:, None] * inv_freq[None, :]
    cos, sin = jnp.cos(ang), jnp.sin(ang)

    q = (rmsnorm(c_q, g_q_a) @ w_q_b).reshape(B, S, MLA_HEADS, QK_NOPE + QK_ROPE)
    q_nope = q[..., :QK_NOPE]
    q_pe = rope(q[..., QK_NOPE:], cos[:, None, :], sin[:, None, :])
    c_kv = ckv_pe[..., :KV_LORA]
    k_pe = rope(ckv_pe[..., KV_LORA:], cos, sin)
    kv = (rmsnorm(c_kv, g_kv_a) @ w_kv_b).reshape(B, S, MLA_HEADS, QK_NOPE + V_HEAD)
    k_nope, v = kv[..., :QK_NOPE], kv[..., QK_NOPE:]
    scale = (QK_NOPE + QK_ROPE) ** -0.5

    nb = S // Q_BLOCK
    qn_b = q_nope.reshape(B, nb, Q_BLOCK, MLA_HEADS, QK_NOPE).transpose(1, 0, 2, 3, 4)
    qp_b = q_pe.reshape(B, nb, Q_BLOCK, MLA_HEADS, QK_ROPE).transpose(1, 0, 2, 3, 4)
    kpos = jnp.arange(S)

    def block(args):
        qn, qp, i = args
        s = (jnp.einsum('bqhd,bkhd->bhqk', qn, k_nope)
             + jnp.einsum('bqhr,bkr->bhqk', qp, k_pe)).astype(jnp.float32) * scale
        qpos = i * Q_BLOCK + jnp.arange(Q_BLOCK)
        causal = kpos[None, :] <= qpos[:, None]
        s = jnp.where(causal[None, None], s, NEG)
        p = jax.nn.softmax(s, axis=-1).astype(v.dtype)
        return jnp.einsum('bhqk,bkhd->bqhd', p, v)

    o = lax.map(block, (qn_b, qp_b, jnp.arange(nb)))
    return o.transpose(1, 0, 2, 3, 4).reshape(B, S, MLA_OUT)


def swa_group(q, k, v, sinks, rel_bias):
    B, S, _ = q.shape
    nb = S // WINDOW
    G = SWA_HEADS // SWA_KV_HEADS
    dh = SWA_HEAD_DIM
    q = q.reshape(B, nb, WINDOW, SWA_KV_HEADS, G, dh)
    k = k.reshape(B, nb, WINDOW, SWA_KV_HEADS, dh)
    v = v.reshape(B, nb, WINDOW, SWA_KV_HEADS, dh)

    def band(t):
        prev = jnp.pad(t, ((0, 0), (1, 0), (0, 0), (0, 0), (0, 0)))[:, :-1]
        return jnp.concatenate([prev, t], axis=2)

    kb, vb = band(k), band(v)
    s = jnp.einsum('bnqhgd,bnkhd->bnhgqk', q, kb).astype(jnp.float32) * (dh ** -0.5)

    qi = jnp.arange(WINDOW)[:, None]
    kj = jnp.arange(2 * WINDOW)[None, :]
    dist = qi + WINDOW - kj
    bias = rel_bias[t5_bucket(dist)]
    bias = bias.transpose(2, 0, 1).reshape(SWA_KV_HEADS, G, WINDOW, 2 * WINDOW)
    kpos = jnp.arange(nb)[:, None, None] * WINDOW - WINDOW + kj[None]
    valid = (dist >= 0)[None] & (dist < WINDOW)[None] & (kpos >= 0)

    s = s + bias.astype(jnp.float32)[None, None]
    s = jnp.where(valid[None, :, None, None], s, NEG)
    sink = sinks.astype(jnp.float32).reshape(SWA_KV_HEADS, G)[None, None, :, :, None, None]
    m = jnp.maximum(jnp.max(s, axis=-1, keepdims=True), sink)
    p = jnp.exp(s - m)
    denom = jnp.sum(p, axis=-1, keepdims=True) + jnp.exp(sink - m)
    o = jnp.einsum('bnhgqk,bnkhd->bnqhgd', (p / denom).astype(vb.dtype), vb)
    return o.reshape(B, S, SWA_OUT)


def setup_inputs(seed: int = 0) -> dict:
    key = jax.random.key(seed)
    ks = iter(jax.random.split(key, 32))
    L = DEPTH

    def w(shape, fan_in):
        return jax.random.normal(next(ks), shape, jnp.float32) * fan_in ** -0.5

    def gain(shape):
        return 1.0 + 0.02 * jax.random.normal(next(ks), shape, jnp.float32)

    return {
        "x": jax.random.normal(next(ks), (BATCH, SEQ, D_MODEL), jnp.float32),
        "g_ffn1": gain((L, D_MODEL)),
        "w_ffn1_gate": w((L, D_MODEL, D_FF), D_MODEL),
        "w_ffn1_up": w((L, D_MODEL, D_FF), D_MODEL),
        "w_ffn1_down": w((L, D_FF, D_MODEL), D_FF),
        "g_mix": gain((L, D_MODEL)),
        "w_in": w((L, D_MODEL, D_IN), D_MODEL),
        "g_q_a": gain((L, Q_LORA)),
        "w_q_b": w((L, Q_LORA, MLA_HEADS * (QK_NOPE + QK_ROPE)), Q_LORA),
        "g_kv_a": gain((L, KV_LORA)),
        "w_kv_b": w((L, KV_LORA, MLA_HEADS * (QK_NOPE + V_HEAD)), KV_LORA),
        "attn_sinks": 0.5 * jax.random.normal(next(ks), (L, SWA_HEADS), jnp.float32),
        "rel_bias": 0.5 * jax.random.normal(next(ks), (REL_BUCKETS, SWA_HEADS), jnp.float32),
        "g_out_mla": gain((L, MLA_OUT)),
        "g_out_swa": gain((L, SWA_OUT)),
        "w_o": w((L, D_MIX, D_MODEL), D_MIX),
        "g_ffn2": gain((L, D_MODEL)),
        "w_ffn2_gate": w((L, D_MODEL, D_FF), D_MODEL),
        "w_ffn2_up": w((L, D_MODEL, D_FF), D_MODEL),
        "w_ffn2_down": w((L, D_FF, D_MODEL), D_FF),
        "g_final": gain((D_MODEL,)),
    }


def reference(x, g_ffn1, w_ffn1_gate, w_ffn1_up, w_ffn1_down, g_mix, w_in, g_q_a, w_q_b,
              g_kv_a, w_kv_b, attn_sinks, rel_bias, g_out_mla, g_out_swa, w_o, g_ffn2,
              w_ffn2_gate, w_ffn2_up, w_ffn2_down, g_final):
    h = x
    for l in range(DEPTH):
        h = h + 0.5 * swiglu(rmsnorm(h, g_ffn1[l]), w_ffn1_gate[l], w_ffn1_up[l], w_ffn1_down[l])
        u = rmsnorm(h, g_mix[l])
        proj = u @ w_in[l]
        c_q, ckv_pe, q_s, k_s, v_s = jnp.split(proj, IN_SPLITS, axis=-1)
        o_mla = mla_group(c_q, ckv_pe, g_q_a[l], w_q_b[l], g_kv_a[l], w_kv_b[l])
        o_swa = swa_group(q_s, k_s, v_s, attn_sinks[l], rel_bias)
        o = jnp.concatenate([rmsnorm(o_mla, g_out_mla[l]), rmsnorm(o_swa, g_out_swa[l])], axis=-1)
        h = h + o @ w_o[l]
        h = h + 0.5 * swiglu(rmsnorm(h, g_ffn2[l]), w_ffn2_gate[l], w_ffn2_up[l], w_ffn2_down[l])
    return rmsnorm(h, g_final)
```

```python
import functools
import math

import jax
import jax.numpy as jnp
import numpy as np
from jax import lax
from jax.experimental import pallas as pl
from jax.experimental.pallas import tpu as pltpu

F32 = jnp.float32
BF16 = jnp.bfloat16

EPS = 1e-6
NEG = -1e30
LOG2E = math.log2(math.e)

D_MODEL = 1024
D_FF = 2816
MLA_HEADS = 8
Q_LORA = 256
KV_LORA = 128
QK_NOPE = 64
QK_ROPE = 32
V_HEAD = 64
ROPE_THETA = 10000.0
SWA_HEADS = 8
SWA_KV_HEADS = 2
SWA_HEAD_DIM = 64
WINDOW = 128
REL_BUCKETS = 32
REL_MAX_DIST = 128
MLA_OUT = MLA_HEADS * V_HEAD
SWA_OUT = SWA_HEADS * SWA_HEAD_DIM

LANES = 128
ROPE_LO = QK_NOPE
ROPE_HALF = QK_ROPE // 2

P_CQ = 0
P_CKV = P_CQ + Q_LORA
P_KPE = P_CKV + KV_LORA
P_QS = P_KPE + LANES
P_KS = P_QS + SWA_OUT
P_VS = P_KS + SWA_KV_HEADS * SWA_HEAD_DIM
P_END = P_VS + SWA_KV_HEADS * SWA_HEAD_DIM

TOKEN_TILE = 512
MLA_TQ = 256
VMEM_LIMIT = 56 * 1024 * 1024


def _rms(x, g):
    return x * lax.rsqrt(jnp.mean(x * x, axis=-1, keepdims=True) + EPS) * g


def _swiglu(xn, wg_ref, wu_ref, wd_ref):
    gate = jnp.dot(xn, wg_ref[...], preferred_element_type=F32)
    up = jnp.dot(xn, wu_ref[...], preferred_element_type=F32)
    act = (gate * jax.nn.sigmoid(gate) * up).astype(BF16)
    return jnp.dot(act, wd_ref[...], preferred_element_type=F32)


def _rope(x, tab_ref):
    c = tab_ref[:, 0:LANES]
    sa = tab_ref[:, LANES:2 * LANES]
    sb = tab_ref[:, 2 * LANES:3 * LANES]
    return (x * c + pltpu.roll(x, LANES - ROPE_HALF, 1) * sa + pltpu.roll(x, ROPE_HALF, 1) * sb)


def _ffn1_proj_kernel(x_ref, g1_ref, wg_ref, wu_ref, wd_ref, gmix_ref, win_ref, gqa_ref, wqb_ref,
                      gkva_ref, wkb_ref, wvb_ref, ropeq_ref, ropek_ref,
                      h1_ref, qm_ref, km_ref, vm_ref, qs_ref, ks_ref, vs_ref):
    x = x_ref[...]
    xn = _rms(x, g1_ref[...]).astype(BF16)
    h1 = x + 0.5 * _swiglu(xn, wg_ref, wu_ref, wd_ref)
    h1_ref[...] = h1

    u = _rms(h1, gmix_ref[...]).astype(BF16)
    proj = jnp.dot(u, win_ref[...], preferred_element_type=F32)

    cq = _rms(proj[:, P_CQ:P_CQ + Q_LORA], gqa_ref[...]).astype(BF16)
    q = jnp.dot(cq, wqb_ref[...], preferred_element_type=F32)
    for h in range(MLA_HEADS):
        qm_ref[:, h * LANES:(h + 1) * LANES] = _rope(q[:, h * LANES:(h + 1) * LANES], ropeq_ref).astype(BF16)

    ckv = _rms(proj[:, P_CKV:P_CKV + KV_LORA], gkva_ref[...]).astype(BF16)
    kn = jnp.dot(ckv, wkb_ref[...], preferred_element_type=F32)
    kpe = _rope(proj[:, P_KPE:P_KPE + LANES], ropek_ref)
    for h in range(MLA_HEADS):
        km_ref[:, h * LANES:(h + 1) * LANES] = (kn[:, h * LANES:(h + 1) * LANES] + kpe).astype(BF16)
    vm_ref[...] = jnp.dot(ckv, wvb_ref[...], preferred_element_type=F32).astype(BF16)

    qs_ref[...] = (proj[:, P_QS:P_QS + SWA_OUT] * (SWA_HEAD_DIM ** -0.5 * LOG2E)).astype(BF16)
    ks_ref[...] = proj[:, P_KS:P_KS + SWA_KV_HEADS * SWA_HEAD_DIM].astype(BF16)
    vs_ref[...] = proj[:, P_VS:P_VS + SWA_KV_HEADS * SWA_HEAD_DIM].astype(BF16)


def _mla_kernel(q_ref, k_ref, v_ref, o_ref, *, seq):
    tq = MLA_TQ
    nq = seq // tq
    row = lax.broadcasted_iota(jnp.int32, (tq, tq), 0)
    col = lax.broadcasted_iota(jnp.int32, (tq, tq), 1)
    causal = col <= row

    def kv_step(qs, ks, carry, masked):
        new = []
        for hh in range(2):
            m, l, acc = carry[hh]
            q = q_ref[pl.ds(qs, tq), hh * LANES:(hh + 1) * LANES]
            k = k_ref[pl.ds(ks, tq), hh * LANES:(hh + 1) * LANES]
            v = v_ref[pl.ds(ks, tq), hh * V_HEAD:(hh + 1) * V_HEAD]
            s = lax.dot_general(q, k, (((1,), (1,)), ((), ())), preferred_element_type=F32)
            if masked:
                s = jnp.where(causal, s, NEG)
            m_new = jnp.maximum(m, jnp.max(s, axis=-1, keepdims=True))
            alpha = jnp.exp2(m - m_new)
            p = jnp.exp2(s - m_new)
            l = alpha * l + jnp.sum(p, axis=-1, keepdims=True)
            acc = alpha * acc + jnp.dot(p.astype(BF16), v, preferred_element_type=F32)
            new.append((m_new, l, acc))
        return tuple(new)

    def q_tile(i, _):
        qs = pl.multiple_of(i * tq, tq)
        init = tuple((jnp.full((tq, 1), NEG, F32), jnp.zeros((tq, 1), F32), jnp.zeros((tq, V_HEAD), F32))
                     for _ in range(2))
        carry = lax.fori_loop(
            0, i, lambda j, c: kv_step(qs, pl.multiple_of(j * tq, tq), c, False), init)
        carry = kv_step(qs, qs, carry, True)
        outs = [acc / l for (_, l, acc) in carry]
        o_ref[pl.ds(qs, tq), :] = jnp.concatenate(outs, axis=-1)
        return 0

    lax.fori_loop(0, nq, q_tile, 0)


def _swa_kernel(bucket_ref, relb_ref, sink_ref, q_ref, k_ref, v_ref, o_ref, bias_ref, *, seq):
    w = WINDOW
    dh = SWA_HEAD_DIM
    g = SWA_HEADS // SWA_KV_HEADS
    nb = seq // w

    @pl.when(pl.program_id(0) == 0)
    def _():
        bucket = bucket_ref[...]
        for h in range(SWA_HEADS):
            b = jnp.zeros((w, 2 * w), F32)
            for r in range(REL_BUCKETS):
                b = jnp.where(bucket == r, relb_ref[r, h] * LOG2E, b)
            bias_ref[h] = b

    qi = lax.broadcasted_iota(jnp.int32, (w, 2 * w), 0)
    kj = lax.broadcasted_iota(jnp.int32, (w, 2 * w), 1)
    dist = qi + w - kj
    band = (dist >= 0) & (dist < w)

    def block(qs, ks):
        outs = []
        for kvh in range(SWA_KV_HEADS):
            kb = k_ref[pl.ds(ks, 2 * w), kvh * dh:(kvh + 1) * dh]
            vb = v_ref[pl.ds(ks, 2 * w), kvh * dh:(kvh + 1) * dh]
            for gg in range(g):
                h = kvh * g + gg
                q = q_ref[pl.ds(qs, w), h * dh:(h + 1) * dh]
                s = lax.dot_general(q, kb, (((1,), (1,)), ((), ())), preferred_element_type=F32)
                s = jnp.where(band, s + bias_ref[h], NEG)
                sink = sink_ref[h] * LOG2E
                m = jnp.maximum(jnp.max(s, axis=-1, keepdims=True), sink)
                p = jnp.exp2(s - m)
                denom = jnp.sum(p, axis=-1, keepdims=True) + jnp.exp2(sink - m)
                o = jnp.dot((p / denom).astype(BF16), vb, preferred_element_type=F32)
                outs.append(o)
        o_ref[pl.ds(qs, w), :] = jnp.concatenate(outs, axis=-1)

    def body(n, _):
        qs = pl.multiple_of(n * w, w)
        block(qs, pl.multiple_of(qs - w, w))
        return 0

    _first_block(q_ref, k_ref, v_ref, o_ref, bias_ref, sink_ref)
    lax.fori_loop(1, nb, body, 0)


def _first_block(q_ref, k_ref, v_ref, o_ref, bias_ref, sink_ref):
    w = WINDOW
    dh = SWA_HEAD_DIM
    g = SWA_HEADS // SWA_KV_HEADS
    qi = lax.broadcasted_iota(jnp.int32, (w, w), 0)
    kj = lax.broadcasted_iota(jnp.int32, (w, w), 1)
    valid = kj <= qi
    outs = []
    for kvh in range(SWA_KV_HEADS):
        kb = k_ref[0:w, kvh * dh:(kvh + 1) * dh]
        vb = v_ref[0:w, kvh * dh:(kvh + 1) * dh]
        for gg in range(g):
            h = kvh * g + gg
            q = q_ref[0:w, h * dh:(h + 1) * dh]
            s = lax.dot_general(q, kb, (((1,), (1,)), ((), ())), preferred_element_type=F32)
            s = jnp.where(valid, s + bias_ref[h, :, w:2 * w], NEG)
            sink = sink_ref[h] * LOG2E
            m = jnp.maximum(jnp.max(s, axis=-1, keepdims=True), sink)
            p = jnp.exp2(s - m)
            denom = jnp.sum(p, axis=-1, keepdims=True) + jnp.exp2(sink - m)
            outs.append(jnp.dot((p / denom).astype(BF16), vb, preferred_element_type=F32))
    o_ref[0:w, :] = jnp.concatenate(outs, axis=-1)


def _out_ffn2_kernel(h1_ref, om_ref, os_ref, gom_ref, gos_ref, wom_ref, wos_ref, g2_ref, wg_ref, wu_ref,
                     wd_ref, gfin_ref, out_ref):
    om = _rms(om_ref[...], gom_ref[...]).astype(BF16)
    osw = _rms(os_ref[...], gos_ref[...]).astype(BF16)
    h2 = (h1_ref[...] + jnp.dot(om, wom_ref[...], preferred_element_type=F32)
          + jnp.dot(osw, wos_ref[...], preferred_element_type=F32))
    hn = _rms(h2, g2_ref[...]).astype(BF16)
    h3 = h2 + 0.5 * _swiglu(hn, wg_ref, wu_ref, wd_ref)
    out_ref[...] = _rms(h3, gfin_ref[...])


def _pack_w_in(w_in):
    d = w_in.shape[0]
    o_cq, o_ckv = 0, Q_LORA
    o_kpe = o_ckv + KV_LORA
    o_qs = o_kpe + QK_ROPE
    o_ks = o_qs + SWA_OUT
    o_vs = o_ks + SWA_KV_HEADS * SWA_HEAD_DIM
    kpe = jnp.zeros((d, LANES), w_in.dtype).at[:, ROPE_LO:ROPE_LO + QK_ROPE].set(w_in[:, o_kpe:o_qs])
    return jnp.concatenate([w_in[:, o_cq:o_kpe], kpe, w_in[:, o_qs:]], axis=1)


def _pack_w_q_b(w_q_b):
    r = w_q_b.shape[0]
    w = w_q_b.reshape(r, MLA_HEADS, QK_NOPE + QK_ROPE)
    w = jnp.pad(w, ((0, 0), (0, 0), (0, LANES - QK_NOPE - QK_ROPE)))
    return w.reshape(r, MLA_HEADS * LANES)


def _pack_w_kv_b(w_kv_b):
    r = w_kv_b.shape[0]
    w = w_kv_b.reshape(r, MLA_HEADS, QK_NOPE + V_HEAD)
    wk = jnp.pad(w[:, :, :QK_NOPE], ((0, 0), (0, 0), (0, LANES - QK_NOPE))).reshape(r, MLA_HEADS * LANES)
    wv = w[:, :, QK_NOPE:].reshape(r, MLA_HEADS * V_HEAD)
    return wk, wv


def _rope_tables(seq, scale):
    pos = jnp.arange(seq, dtype=F32)
    inv_freq = ROPE_THETA ** (-jnp.arange(0, QK_ROPE, 2, dtype=F32) / QK_ROPE)
    ang = pos[:, None] * inv_freq[None, :]
    cos, sin = jnp.cos(ang), jnp.sin(ang)
    zeros = jnp.zeros((seq, LANES), F32)
    c = jnp.ones((seq, LANES), F32).at[:, ROPE_LO:ROPE_LO + ROPE_HALF].set(cos)
    c = c.at[:, ROPE_LO + ROPE_HALF:ROPE_LO + QK_ROPE].set(cos)
    c = c.at[:, ROPE_LO + QK_ROPE:].set(0.0)
    sa = zeros.at[:, ROPE_LO:ROPE_LO + ROPE_HALF].set(-sin)
    sb = zeros.at[:, ROPE_LO + ROPE_HALF:ROPE_LO + QK_ROPE].set(sin)
    return jnp.concatenate([c, sa, sb], axis=1) * scale


def _t5_bucket(dist):
    n = jnp.maximum(dist, 0)
    max_exact = REL_BUCKETS // 2
    nf = jnp.maximum(n, 1).astype(F32)
    large = max_exact + (jnp.log(nf / max_exact) / math.log(REL_MAX_DIST / max_exact)
                         * (REL_BUCKETS - max_exact)).astype(jnp.int32)
    large = jnp.minimum(large, REL_BUCKETS - 1)
    return jnp.where(n < max_exact, n, large)


def _const_spec(shape):
    nd = len(shape)
    return pl.BlockSpec(shape, lambda *_: (0,) * nd, pipeline_mode=pl.Buffered(1))


def kernel(x, g_ffn1, w_ffn1_gate, w_ffn1_up, w_ffn1_down, g_mix, w_in, g_q_a, w_q_b, g_kv_a, w_kv_b,
           attn_sinks, rel_bias, g_out_mla, g_out_swa, w_o, g_ffn2, w_ffn2_gate, w_ffn2_up, w_ffn2_down,
           g_final):
    bsz, seq, d = x.shape
    n_tok = bsz * seq
    tm = TOKEN_TILE
    assert d == D_MODEL and seq % tm == 0 and seq % MLA_TQ == 0 and seq % WINDOW == 0
    assert g_ffn1.shape[0] == 1, "single layer"

    row = lambda a: a.reshape(1, -1).astype(F32)
    xf = x.reshape(n_tok, d)

    win = _pack_w_in(w_in[0]).astype(BF16)
    wqb = _pack_w_q_b(w_q_b[0]).astype(BF16)
    wkb, wvb = _pack_w_kv_b(w_kv_b[0])
    wkb, wvb = wkb.astype(BF16), wvb.astype(BF16)
    rope_q = _rope_tables(seq, (QK_NOPE + QK_ROPE) ** -0.5 * LOG2E)
    rope_k = _rope_tables(seq, 1.0)

    tiles = n_tok // tm
    tile_spec = lambda width: pl.BlockSpec((tm, width), lambda i: (i, 0))
    pos_spec = pl.BlockSpec((tm, 3 * LANES), lambda i: (i % (seq // tm), 0))
    kv_swa = SWA_KV_HEADS * SWA_HEAD_DIM

    weights1 = [row(g_ffn1[0]), w_ffn1_gate[0].astype(BF16), w_ffn1_up[0].astype(BF16),
                w_ffn1_down[0].astype(BF16), row(g_mix[0]), win, row(g_q_a[0]), wqb, row(g_kv_a[0]), wkb, wvb]
    h1, q_mla, k_mla, v_mla, q_swa, k_swa, v_swa = pl.pallas_call(
        _ffn1_proj_kernel,
        grid=(tiles,),
        in_specs=[tile_spec(d)] + [_const_spec(a.shape) for a in weights1] + [pos_spec, pos_spec],
        out_specs=[tile_spec(d), tile_spec(MLA_HEADS * LANES), tile_spec(MLA_HEADS * LANES),
                   tile_spec(MLA_OUT), tile_spec(SWA_OUT), tile_spec(kv_swa), tile_spec(kv_swa)],
        out_shape=[jax.ShapeDtypeStruct((n_tok, d), F32),
                   jax.ShapeDtypeStruct((n_tok, MLA_HEADS * LANES), BF16),
                   jax.ShapeDtypeStruct((n_tok, MLA_HEADS * LANES), BF16),
                   jax.ShapeDtypeStruct((n_tok, MLA_OUT), BF16),
                   jax.ShapeDtypeStruct((n_tok, SWA_OUT), BF16),
                   jax.ShapeDtypeStruct((n_tok, kv_swa), BF16),
                   jax.ShapeDtypeStruct((n_tok, kv_swa), BF16)],
        compiler_params=pltpu.CompilerParams(dimension_semantics=("arbitrary",),
                                             vmem_limit_bytes=VMEM_LIMIT),
        name="ffn1_proj",
    )(xf, *weights1, rope_q, rope_k)

    pairs = MLA_HEADS // 2
    o_mla = pl.pallas_call(
        functools.partial(_mla_kernel, seq=seq),
        grid=(bsz, pairs),
        in_specs=[pl.BlockSpec((seq, 2 * LANES), lambda b, p: (b, p)),
                  pl.BlockSpec((seq, 2 * LANES), lambda b, p: (b, p)),
                  pl.BlockSpec((seq, 2 * V_HEAD), lambda b, p: (b, p))],
        out_specs=pl.BlockSpec((seq, 2 * V_HEAD), lambda b, p: (b, p)),
        out_shape=jax.ShapeDtypeStruct((n_tok, MLA_OUT), F32),
        compiler_params=pltpu.CompilerParams(dimension_semantics=("arbitrary", "arbitrary"),
                                             vmem_limit_bytes=VMEM_LIMIT),
        name="mla",
    )(q_mla, k_mla, v_mla)

    qi = jnp.arange(WINDOW)[:, None]
    kj = jnp.arange(2 * WINDOW)[None, :]
    bucket = _t5_bucket(qi + WINDOW - kj).astype(jnp.int32)
    smem = pl.BlockSpec(memory_space=pltpu.SMEM)
    o_swa = pl.pallas_call(
        functools.partial(_swa_kernel, seq=seq),
        grid=(bsz,),
        in_specs=[_const_spec(bucket.shape), smem, smem,
                  pl.BlockSpec((seq, SWA_OUT), lambda b: (b, 0)),
                  pl.BlockSpec((seq, kv_swa), lambda b: (b, 0)),
                  pl.BlockSpec((seq, kv_swa), lambda b: (b, 0))],
        out_specs=pl.BlockSpec((seq, SWA_OUT), lambda b: (b, 0)),
        out_shape=jax.ShapeDtypeStruct((n_tok, SWA_OUT), F32),
        scratch_shapes=[pltpu.VMEM((SWA_HEADS, WINDOW, 2 * WINDOW), F32)],
        compiler_params=pltpu.CompilerParams(dimension_semantics=("arbitrary",),
                                             vmem_limit_bytes=VMEM_LIMIT),
        name="swa",
    )(bucket, rel_bias.astype(F32), attn_sinks[0].astype(F32), q_swa, k_swa, v_swa)

    wo = w_o[0].astype(BF16)
    weights4 = [row(g_out_mla[0]), row(g_out_swa[0]), wo[:MLA_OUT], wo[MLA_OUT:], row(g_ffn2[0]),
                w_ffn2_gate[0].astype(BF16), w_ffn2_up[0].astype(BF16), w_ffn2_down[0].astype(BF16),
                row(g_final)]
    out = pl.pallas_call(
        _out_ffn2_kernel,
        grid=(tiles,),
        in_specs=[tile_spec(d), tile_spec(MLA_OUT), tile_spec(SWA_OUT)]
                 + [_const_spec(a.shape) for a in weights4],
        out_specs=tile_spec(d),
        out_shape=jax.ShapeDtypeStruct((n_tok, d), F32),
        compiler_params=pltpu.CompilerParams(dimension_semantics=("arbitrary",),
                                             vmem_limit_bytes=VMEM_LIMIT),
        name="out_ffn2",
    )(h1, o_mla, o_swa, *weights4)
    return out.reshape(bsz, seq, d)
```

```python
import functools
import math

import jax
import jax.numpy as jnp
from jax import lax
from jax.experimental import pallas as pl
from jax.experimental.pallas import tpu as pltpu

F32 = jnp.float32
BF16 = jnp.bfloat16

EPS = 1e-6
NEG = -1e30
LOG2E = math.log2(math.e)

D_MODEL = 1024
D_FF = 2816
MLA_HEADS = 8
Q_LORA = 256
KV_LORA = 128
QK_NOPE = 64
QK_ROPE = 32
V_HEAD = 64
ROPE_THETA = 10000.0
SWA_HEADS = 8
SWA_KV_HEADS = 2
SWA_HEAD_DIM = 64
SWA_GROUP = SWA_HEADS // SWA_KV_HEADS
WINDOW = 128
REL_BUCKETS = 32
REL_MAX_DIST = 128
MLA_OUT = MLA_HEADS * V_HEAD
SWA_OUT = SWA_HEADS * SWA_HEAD_DIM
SWA_KV = SWA_KV_HEADS * SWA_HEAD_DIM

LANES = 128
ROPE_LO = QK_NOPE
ROPE_HALF = QK_ROPE // 2

P_CQ = 0
P_CKV = P_CQ + Q_LORA
P_KPE = P_CKV + KV_LORA
P_QS = P_KPE + LANES
P_KS = P_QS + SWA_OUT
P_VS = P_KS + SWA_KV
P_END = P_VS + SWA_KV

TOKEN_TILE = 512
MLA_TILE = 256
MLA_HEADS_PER_STEP = 4
VMEM_LIMIT = 56 * 1024 * 1024

_NT = (((1,), (1,)), ((), ()))


def _rms(x, g):
    return x * lax.rsqrt(jnp.mean(x * x, axis=-1, keepdims=True) + EPS) * g


def _swiglu(xn, wg_ref, wu_ref, wd_ref):
    gate = jnp.dot(xn, wg_ref[...], preferred_element_type=F32)
    up = jnp.dot(xn, wu_ref[...], preferred_element_type=F32)
    act = (gate * jax.nn.sigmoid(gate) * up).astype(BF16)
    return jnp.dot(act, wd_ref[...], preferred_element_type=F32)


def _rope_lanes(x, tab_ref):
    c = tab_ref[:, 0:LANES]
    sa = tab_ref[:, LANES:2 * LANES]
    sb = tab_ref[:, 2 * LANES:3 * LANES]
    return (x * c + pltpu.roll(x, LANES - ROPE_HALF, 1) * sa + pltpu.roll(x, ROPE_HALF, 1) * sb)


def _ffn1_proj_kernel(x_ref, g1_ref, wg_ref, wu_ref, wd_ref, gmix_ref, win_ref, gqa_ref, wqbt_ref,
                      gkva_ref, wkb_ref, wvbt_ref, ropeq_ref, ropek_ref,
                      h1_ref, qm_ref, km_ref, vm_ref, qs_ref, ks_ref, vs_ref, *, q_scale):
    tm = x_ref.shape[0]
    x = x_ref[...]
    xn = _rms(x, g1_ref[...]).astype(BF16)
    h1 = x + 0.5 * _swiglu(xn, wg_ref, wu_ref, wd_ref)
    h1_ref[...] = h1

    u = _rms(h1, gmix_ref[...]).astype(BF16)
    proj = jnp.dot(u, win_ref[...], preferred_element_type=F32)

    cq = _rms(proj[:, P_CQ:P_CQ + Q_LORA], gqa_ref[...]).astype(BF16)
    qt = lax.dot_general(wqbt_ref[...], cq, _NT, preferred_element_type=F32)
    cs = ropeq_ref[0:ROPE_HALF, :]
    sn = ropeq_ref[ROPE_HALF:QK_ROPE, :]
    for h in range(MLA_HEADS):
        r = h * LANES
        x1 = qt[r + ROPE_LO:r + ROPE_LO + ROPE_HALF]
        x2 = qt[r + ROPE_LO + ROPE_HALF:r + ROPE_LO + QK_ROPE]
        qh = jnp.concatenate([qt[r:r + QK_NOPE] * q_scale, x1 * cs - x2 * sn, x2 * cs + x1 * sn,
                              qt[r + ROPE_LO + QK_ROPE:r + LANES]], axis=0).astype(BF16)
        for t in range(tm // MLA_TILE):
            qm_ref[t, r:r + LANES, :] = qh[:, t * MLA_TILE:(t + 1) * MLA_TILE]

    ckv = _rms(proj[:, P_CKV:P_CKV + KV_LORA], gkva_ref[...]).astype(BF16)
    kn = jnp.dot(ckv, wkb_ref[...], preferred_element_type=F32)
    kpe = _rope_lanes(proj[:, P_KPE:P_KPE + LANES], ropek_ref)
    for h in range(MLA_HEADS):
        km_ref[:, h * LANES:(h + 1) * LANES] = (kn[:, h * LANES:(h + 1) * LANES] + kpe).astype(BF16)
    vt = lax.dot_general(wvbt_ref[...], ckv, _NT, preferred_element_type=F32).astype(BF16)
    for t in range(tm // MLA_TILE):
        vm_ref[t] = vt[:, t * MLA_TILE:(t + 1) * MLA_TILE]

    qst = (proj[:, P_QS:P_QS + SWA_OUT] * (SWA_HEAD_DIM ** -0.5 * LOG2E)).T.astype(BF16)
    vst = proj[:, P_VS:P_VS + SWA_KV].T.astype(BF16)
    for t in range(tm // WINDOW):
        qs_ref[t] = qst[:, t * WINDOW:(t + 1) * WINDOW]
        vs_ref[t] = vst[:, t * WINDOW:(t + 1) * WINDOW]
    ks_ref[...] = proj[:, P_KS:P_KS + SWA_KV].astype(BF16)


def _mla_kernel(q_ref, k_ref, v_ref, o_ref, s_ref):
    nt, rows, t = q_ref.shape
    hp = rows // LANES
    key = lax.broadcasted_iota(jnp.int32, (t, t), 0)
    qry = lax.broadcasted_iota(jnp.int32, (t, t), 1)
    causal = key <= qry

    def issue(i, j):
        ks = pl.multiple_of(j * t, t)
        return tuple(jnp.dot(k_ref[pl.ds(ks, t), hh * LANES:(hh + 1) * LANES],
                             q_ref[i, hh * LANES:(hh + 1) * LANES, :], preferred_element_type=F32)
                     for hh in range(hp))

    def stash(sc, masked):
        tile_max = []
        for hh in range(hp):
            s = jnp.where(causal, sc[hh], NEG) if masked else sc[hh]
            s_ref[hh] = s
            tile_max.append(jnp.max(s, axis=0, keepdims=True))
        return tuple(tile_max)

    def update(j, tile_max, carry):
        new = []
        for hh in range(hp):
            m, l, acc = carry[hh]
            m_new = jnp.maximum(m, tile_max[hh])
            alpha = jnp.exp2(m - m_new)
            p = jnp.exp2(s_ref[hh] - m_new)
            l = alpha * l + jnp.sum(p, axis=0, keepdims=True)
            v = v_ref[j, hh * V_HEAD:(hh + 1) * V_HEAD, :]
            acc = alpha * acc + jnp.dot(v, p.astype(BF16), preferred_element_type=F32)
            new.append((m_new, l, acc))
        return tuple(new)

    def step(j, nxt, nxt_masked, carry, tile_max):
        sc = issue(*nxt)
        carry = update(j, tile_max, carry)
        return carry, stash(sc, nxt_masked)

    def finish(i, carry):
        o_ref[i] = jnp.concatenate([acc / l for (_, l, acc) in carry], axis=0)

    init = tuple((jnp.full((1, t), NEG, F32), jnp.zeros((1, t), F32), jnp.zeros((V_HEAD, t), F32))
                 for _ in range(hp))

    tile_max = stash(issue(0, 0), True)
    carry, tile_max = step(0, (1, 0), False, init, tile_max)
    finish(0, carry)

    def q_tile(i, tile_max):
        carry, tile_max = lax.fori_loop(
            0, i - 1, lambda j, st: step(j, (i, j + 1), False, *st), (init, tile_max))
        carry, tile_max = step(i - 1, (i, i), True, carry, tile_max)
        carry, tile_max = step(i, (jnp.minimum(i + 1, nt - 1), 0), False, carry, tile_max)
        finish(i, carry)
        return tile_max

    lax.fori_loop(1, nt, q_tile, tile_max)


def _swa_kernel(bucket_ref, relb_ref, sink_ref, q_ref, k_ref, v_ref, o_ref, bias_ref, sinkrow_ref):
    w, dh, g = WINDOW, SWA_HEAD_DIM, SWA_GROUP
    nb = q_ref.shape[0]

    @pl.when(pl.program_id(0) == 0)
    def _():
        bucket = bucket_ref[...]
        for kvh in range(SWA_KV_HEADS):
            for gg in range(g):
                h = kvh * g + gg
                b = jnp.zeros((2 * w, w), F32)
                for r in range(REL_BUCKETS):
                    b = jnp.where(bucket == r, relb_ref[r, h] * LOG2E, b)
                bias_ref[kvh, :, gg * w:(gg + 1) * w] = b
                sinkrow_ref[kvh, :, gg * w:(gg + 1) * w] = jnp.full((1, w), sink_ref[h] * LOG2E, F32)

    def tile_cols(a):
        return jnp.concatenate([a] * g, axis=1)

    kj = lax.broadcasted_iota(jnp.int32, (2 * w, w), 0)
    qi = lax.broadcasted_iota(jnp.int32, (2 * w, w), 1)
    dist = qi + w - kj
    band = tile_cols((dist >= 0) & (dist < w))
    kj0 = lax.broadcasted_iota(jnp.int32, (w, w), 0)
    qi0 = lax.broadcasted_iota(jnp.int32, (w, w), 1)
    causal0 = tile_cols(kj0 <= qi0)
    zeros_q = jnp.zeros((dh, g * w), BF16)

    def attend(n, kwin, vwin_t, valid, bias_rows):
        for kvh in range(SWA_KV_HEADS):
            qt = jnp.concatenate([q_ref[n, (kvh * g + gg) * dh:(kvh * g + gg + 1) * dh, :] for gg in range(g)],
                                 axis=1)
            qt = jnp.concatenate([qt, zeros_q] if kvh == 0 else [zeros_q, qt], axis=0)
            s = jnp.dot(kwin, qt, preferred_element_type=F32)
            s = jnp.where(valid, s + bias_ref[kvh, bias_rows, :], NEG)
            sink = sinkrow_ref[kvh]
            m = jnp.maximum(jnp.max(s, axis=0, keepdims=True), sink)
            p = jnp.exp2(s - m)
            denom = jnp.sum(p, axis=0, keepdims=True) + jnp.exp2(sink - m)
            o = jnp.dot(vwin_t[kvh * dh:(kvh + 1) * dh], p.astype(BF16), preferred_element_type=F32)
            o = o / denom
            for gg in range(g):
                h = kvh * g + gg
                o_ref[n, h * dh:(h + 1) * dh, :] = o[:, gg * w:(gg + 1) * w]

    attend(0, k_ref[0:w, :], v_ref[0], causal0, slice(w, 2 * w))

    def body(n, _):
        ks = pl.multiple_of((n - 1) * w, w)
        vwin_t = jnp.concatenate([v_ref[n - 1], v_ref[n]], axis=1)
        attend(n, k_ref[pl.ds(ks, 2 * w), :], vwin_t, band, slice(0, 2 * w))
        return 0

    lax.fori_loop(1, nb, body, 0)


def _out_ffn2_kernel(h1_ref, om_ref, os_ref, gom_ref, gos_ref, wom_ref, wos_ref, g2_ref, wg_ref, wu_ref,
                     wd_ref, gfin_ref, out_ref):
    om = jnp.concatenate([om_ref[t].T for t in range(om_ref.shape[0])], axis=0)
    osw = jnp.concatenate([os_ref[t].T for t in range(os_ref.shape[0])], axis=0)
    om = _rms(om, gom_ref[...]).astype(BF16)
    osw = _rms(osw, gos_ref[...]).astype(BF16)
    h2 = (h1_ref[...] + jnp.dot(om, wom_ref[...], preferred_element_type=F32)
          + jnp.dot(osw, wos_ref[...], preferred_element_type=F32))
    hn = _rms(h2, g2_ref[...]).astype(BF16)
    h3 = h2 + 0.5 * _swiglu(hn, wg_ref, wu_ref, wd_ref)
    out_ref[...] = _rms(h3, gfin_ref[...])


def _pack_w_in(w_in):
    d = w_in.shape[0]
    o_kpe = Q_LORA + KV_LORA
    o_qs = o_kpe + QK_ROPE
    kpe = jnp.zeros((d, LANES), w_in.dtype).at[:, ROPE_LO:ROPE_LO + QK_ROPE].set(w_in[:, o_kpe:o_qs])
    return jnp.concatenate([w_in[:, :o_kpe], kpe, w_in[:, o_qs:]], axis=1)


def _pack_w_q_b_t(w_q_b):
    r = w_q_b.shape[0]
    w = w_q_b.reshape(r, MLA_HEADS, QK_NOPE + QK_ROPE)
    w = jnp.pad(w, ((0, 0), (0, 0), (0, LANES - QK_NOPE - QK_ROPE)))
    return w.reshape(r, MLA_HEADS * LANES).T


def _pack_w_kv_b(w_kv_b):
    r = w_kv_b.shape[0]
    w = w_kv_b.reshape(r, MLA_HEADS, QK_NOPE + V_HEAD)
    wk = jnp.pad(w[:, :, :QK_NOPE], ((0, 0), (0, 0), (0, LANES - QK_NOPE))).reshape(r, MLA_HEADS * LANES)
    wv_t = w[:, :, QK_NOPE:].reshape(r, MLA_HEADS * V_HEAD).T
    return wk, wv_t


def _rope_angles(seq):
    pos = jnp.arange(seq, dtype=F32)
    inv_freq = ROPE_THETA ** (-jnp.arange(0, QK_ROPE, 2, dtype=F32) / QK_ROPE)
    ang = pos[:, None] * inv_freq[None, :]
    return jnp.cos(ang), jnp.sin(ang)


def _rope_table_lanes(seq):
    cos, sin = _rope_angles(seq)
    zeros = jnp.zeros((seq, LANES), F32)
    c = zeros.at[:, ROPE_LO:ROPE_LO + ROPE_HALF].set(cos).at[:, ROPE_LO + ROPE_HALF:ROPE_LO + QK_ROPE].set(cos)
    sa = zeros.at[:, ROPE_LO:ROPE_LO + ROPE_HALF].set(-sin)
    sb = zeros.at[:, ROPE_LO + ROPE_HALF:ROPE_LO + QK_ROPE].set(sin)
    return jnp.concatenate([c, sa, sb], axis=1)


def _t5_bucket(dist):
    n = jnp.maximum(dist, 0)
    max_exact = REL_BUCKETS // 2
    nf = jnp.maximum(n, 1).astype(F32)
    large = max_exact + (jnp.log(nf / max_exact) / math.log(REL_MAX_DIST / max_exact)
                         * (REL_BUCKETS - max_exact)).astype(jnp.int32)
    large = jnp.minimum(large, REL_BUCKETS - 1)
    return jnp.where(n < max_exact, n, large)


def _const_spec(shape):
    nd = len(shape)
    return pl.BlockSpec(shape, lambda *_: (0,) * nd, pipeline_mode=pl.Buffered(1))


def kernel(x, g_ffn1, w_ffn1_gate, w_ffn1_up, w_ffn1_down, g_mix, w_in, g_q_a, w_q_b, g_kv_a, w_kv_b,
           attn_sinks, rel_bias, g_out_mla, g_out_swa, w_o, g_ffn2, w_ffn2_gate, w_ffn2_up, w_ffn2_down,
           g_final):
    bsz, seq, d = x.shape
    n_tok = bsz * seq
    tm = TOKEN_TILE
    assert d == D_MODEL and seq % tm == 0 and tm % MLA_TILE == 0 and tm % WINDOW == 0
    assert g_ffn1.shape[0] == 1, "single layer"

    row = lambda a: a.reshape(1, -1).astype(F32)
    xf = x.reshape(n_tok, d)

    win = _pack_w_in(w_in[0]).astype(BF16)
    wqbt = _pack_w_q_b_t(w_q_b[0]).astype(BF16)
    wkb, wvbt = _pack_w_kv_b(w_kv_b[0])
    wkb, wvbt = wkb.astype(BF16), wvbt.astype(BF16)
    q_scale = (QK_NOPE + QK_ROPE) ** -0.5 * LOG2E
    cos, sin = _rope_angles(seq)
    rope_q = jnp.concatenate([cos.T, sin.T], axis=0) * q_scale
    rope_k = _rope_table_lanes(seq)

    tiles = n_tok // tm
    tps = seq // tm
    nt_mla, nb_swa = seq // MLA_TILE, seq // WINDOW
    tile_spec = lambda width: pl.BlockSpec((tm, width), lambda i: (i, 0))
    fm_spec = lambda rows, tok: pl.BlockSpec((None, tm // tok, rows, tok), lambda i: (i // tps, i % tps, 0, 0))

    weights1 = [row(g_ffn1[0]), w_ffn1_gate[0].astype(BF16), w_ffn1_up[0].astype(BF16),
                w_ffn1_down[0].astype(BF16), row(g_mix[0]), win, row(g_q_a[0]), wqbt, row(g_kv_a[0]), wkb, wvbt]
    h1, q_mla, k_mla, v_mla, q_swa, k_swa, v_swa = pl.pallas_call(
        functools.partial(_ffn1_proj_kernel, q_scale=q_scale),
        grid=(tiles,),
        in_specs=[tile_spec(d)] + [_const_spec(a.shape) for a in weights1]
                 + [pl.BlockSpec((QK_ROPE, tm), lambda i: (0, i % tps)),
                    pl.BlockSpec((tm, 3 * LANES), lambda i: (i % tps, 0))],
        out_specs=[tile_spec(d), fm_spec(MLA_HEADS * LANES, MLA_TILE), tile_spec(MLA_HEADS * LANES),
                   fm_spec(MLA_OUT, MLA_TILE), fm_spec(SWA_OUT, WINDOW), tile_spec(SWA_KV),
                   fm_spec(SWA_KV, WINDOW)],
        out_shape=[jax.ShapeDtypeStruct((n_tok, d), F32),
                   jax.ShapeDtypeStruct((bsz, nt_mla, MLA_HEADS * LANES, MLA_TILE), BF16),
                   jax.ShapeDtypeStruct((n_tok, MLA_HEADS * LANES), BF16),
                   jax.ShapeDtypeStruct((bsz, nt_mla, MLA_OUT, MLA_TILE), BF16),
                   jax.ShapeDtypeStruct((bsz, nb_swa, SWA_OUT, WINDOW), BF16),
                   jax.ShapeDtypeStruct((n_tok, SWA_KV), BF16),
                   jax.ShapeDtypeStruct((bsz, nb_swa, SWA_KV, WINDOW), BF16)],
        compiler_params=pltpu.CompilerParams(dimension_semantics=("arbitrary",),
                                             vmem_limit_bytes=VMEM_LIMIT),
        name="ffn1_proj",
    )(xf, *weights1, rope_q, rope_k)

    hp = MLA_HEADS_PER_STEP
    o_mla = pl.pallas_call(
        _mla_kernel,
        grid=(bsz, MLA_HEADS // hp),
        in_specs=[pl.BlockSpec((None, nt_mla, hp * LANES, MLA_TILE), lambda b, p: (b, 0, p, 0)),
                  pl.BlockSpec((seq, hp * LANES), lambda b, p: (b, p)),
                  pl.BlockSpec((None, nt_mla, hp * V_HEAD, MLA_TILE), lambda b, p: (b, 0, p, 0))],
        out_specs=pl.BlockSpec((None, nt_mla, hp * V_HEAD, MLA_TILE), lambda b, p: (b, 0, p, 0)),
        out_shape=jax.ShapeDtypeStruct((bsz, nt_mla, MLA_OUT, MLA_TILE), F32),
        scratch_shapes=[pltpu.VMEM((hp, MLA_TILE, MLA_TILE), F32)],
        compiler_params=pltpu.CompilerParams(dimension_semantics=("arbitrary", "arbitrary"),
                                             vmem_limit_bytes=VMEM_LIMIT),
        name="mla",
    )(q_mla, k_mla, v_mla)

    kj = jnp.arange(2 * WINDOW)[:, None]
    qi = jnp.arange(WINDOW)[None, :]
    bucket_t = _t5_bucket(qi + WINDOW - kj).astype(jnp.int32)
    smem = pl.BlockSpec(memory_space=pltpu.SMEM)
    o_swa = pl.pallas_call(
        _swa_kernel,
        grid=(bsz,),
        in_specs=[_const_spec(bucket_t.shape), smem, smem,
                  pl.BlockSpec((None, nb_swa, SWA_OUT, WINDOW), lambda b: (b, 0, 0, 0)),
                  pl.BlockSpec((seq, SWA_KV), lambda b: (b, 0)),
                  pl.BlockSpec((None, nb_swa, SWA_KV, WINDOW), lambda b: (b, 0, 0, 0))],
        out_specs=pl.BlockSpec((None, nb_swa, SWA_OUT, WINDOW), lambda b: (b, 0, 0, 0)),
        out_shape=jax.ShapeDtypeStruct((bsz, nb_swa, SWA_OUT, WINDOW), F32),
        scratch_shapes=[pltpu.VMEM((SWA_KV_HEADS, 2 * WINDOW, SWA_GROUP * WINDOW), F32),
                        pltpu.VMEM((SWA_KV_HEADS, 1, SWA_GROUP * WINDOW), F32)],
        compiler_params=pltpu.CompilerParams(dimension_semantics=("arbitrary",),
                                             vmem_limit_bytes=VMEM_LIMIT),
        name="swa",
    )(bucket_t, rel_bias.astype(F32), attn_sinks[0].astype(F32), q_swa, k_swa, v_swa)

    wo = w_o[0].astype(BF16)
    weights4 = [row(g_out_mla[0]), row(g_out_swa[0]), wo[:MLA_OUT], wo[MLA_OUT:], row(g_ffn2[0]),
                w_ffn2_gate[0].astype(BF16), w_ffn2_up[0].astype(BF16), w_ffn2_down[0].astype(BF16),
                row(g_final)]
    out = pl.pallas_call(
        _out_ffn2_kernel,
        grid=(tiles,),
        in_specs=[tile_spec(d), fm_spec(MLA_OUT, MLA_TILE), fm_spec(SWA_OUT, WINDOW)]
                 + [_const_spec(a.shape) for a in weights4],
        out_specs=tile_spec(d),
        out_shape=jax.ShapeDtypeStruct((n_tok, d), F32),
        compiler_params=pltpu.CompilerParams(dimension_semantics=("arbitrary",),
                                             vmem_limit_bytes=VMEM_LIMIT),
        name="out_ffn2",
    )(h1, o_mla, o_swa, *weights4)
    return out.reshape(bsz, seq, d)
```

```python
import functools
import math

import jax
import jax.numpy as jnp
from jax import lax
from jax.experimental import pallas as pl
from jax.experimental.pallas import tpu as pltpu

F32 = jnp.float32
BF16 = jnp.bfloat16

EPS = 1e-6
NEG = -1e30
LOG2E = math.log2(math.e)

D_MODEL = 1024
D_FF = 2816
MLA_HEADS = 8
Q_LORA = 256
KV_LORA = 128
QK_NOPE = 64
QK_ROPE = 32
V_HEAD = 64
ROPE_THETA = 10000.0
SWA_HEADS = 8
SWA_KV_HEADS = 2
SWA_HEAD_DIM = 64
SWA_GROUP = SWA_HEADS // SWA_KV_HEADS
WINDOW = 128
REL_BUCKETS = 32
REL_MAX_DIST = 128
MLA_OUT = MLA_HEADS * V_HEAD
SWA_OUT = SWA_HEADS * SWA_HEAD_DIM
SWA_KV = SWA_KV_HEADS * SWA_HEAD_DIM

LANES = 128
ROPE_LO = QK_NOPE
ROPE_HALF = QK_ROPE // 2

P_CQ = 0
P_CKV = P_CQ + Q_LORA
P_KPE = P_CKV + KV_LORA
P_QS = P_KPE + LANES
P_KS = P_QS + SWA_OUT
P_VS = P_KS + SWA_KV
P_END = P_VS + SWA_KV

TOKEN_TILE = 512
MLA_TILE = 256
MLA_HEADS_PER_STEP = 4
ONES_ROWS = 16
MLA_V_ROWS = V_HEAD + ONES_ROWS
SWA_V_ROWS = SWA_HEAD_DIM + ONES_ROWS
VMEM_LIMIT = 56 * 1024 * 1024

_NT = (((1,), (1,)), ((), ()))


def _rms(x, g):
    return x * lax.rsqrt(jnp.mean(x * x, axis=-1, keepdims=True) + EPS) * g


def _swiglu(xn, wg_ref, wu_ref, wd_ref):
    gate = jnp.dot(xn, wg_ref[...], preferred_element_type=F32)
    up = jnp.dot(xn, wu_ref[...], preferred_element_type=F32)
    act = (gate * jax.nn.sigmoid(gate) * up).astype(BF16)
    return jnp.dot(act, wd_ref[...], preferred_element_type=F32)


def _rope_lanes(x, tab_ref):
    c = tab_ref[:, 0:LANES]
    sa = tab_ref[:, LANES:2 * LANES]
    sb = tab_ref[:, 2 * LANES:3 * LANES]
    return (x * c + pltpu.roll(x, LANES - ROPE_HALF, 1) * sa + pltpu.roll(x, ROPE_HALF, 1) * sb)


def _ffn1_proj_kernel(x_ref, g1_ref, wg_ref, wu_ref, wd_ref, gmix_ref, win_ref, gqa_ref, wqbt_ref,
                      gkva_ref, wkb_ref, wvbt_ref, ropeq_ref, ropek_ref,
                      h1_ref, qm_ref, km_ref, vm_ref, qs_ref, ks_ref, vs_ref, *, q_scale):
    tm = x_ref.shape[0]
    x = x_ref[...]
    xn = _rms(x, g1_ref[...]).astype(BF16)
    h1 = x + 0.5 * _swiglu(xn, wg_ref, wu_ref, wd_ref)
    h1_ref[...] = h1

    u = _rms(h1, gmix_ref[...]).astype(BF16)
    proj = jnp.dot(u, win_ref[...], preferred_element_type=F32)

    cq = _rms(proj[:, P_CQ:P_CQ + Q_LORA], gqa_ref[...]).astype(BF16)
    qt = lax.dot_general(wqbt_ref[...], cq, _NT, preferred_element_type=F32)
    cs = ropeq_ref[0:ROPE_HALF, :]
    sn = ropeq_ref[ROPE_HALF:QK_ROPE, :]
    for h in range(MLA_HEADS):
        r = h * LANES
        x1 = qt[r + ROPE_LO:r + ROPE_LO + ROPE_HALF]
        x2 = qt[r + ROPE_LO + ROPE_HALF:r + ROPE_LO + QK_ROPE]
        qh = jnp.concatenate([qt[r:r + QK_NOPE] * q_scale, x1 * cs - x2 * sn, x2 * cs + x1 * sn,
                              qt[r + ROPE_LO + QK_ROPE:r + LANES]], axis=0).astype(BF16)
        for t in range(tm // MLA_TILE):
            qm_ref[t, r:r + LANES, :] = qh[:, t * MLA_TILE:(t + 1) * MLA_TILE]

    ckv = _rms(proj[:, P_CKV:P_CKV + KV_LORA], gkva_ref[...]).astype(BF16)
    kn = jnp.dot(ckv, wkb_ref[...], preferred_element_type=F32)
    kpe = _rope_lanes(proj[:, P_KPE:P_KPE + LANES], ropek_ref)
    for h in range(MLA_HEADS):
        km_ref[:, h * LANES:(h + 1) * LANES] = (kn[:, h * LANES:(h + 1) * LANES] + kpe).astype(BF16)
    vt = lax.dot_general(wvbt_ref[...], ckv, _NT, preferred_element_type=F32).astype(BF16)
    ones = jnp.ones((ONES_ROWS, tm), BF16)
    vt = jnp.concatenate([blk for h in range(MLA_HEADS) for blk in (vt[h * V_HEAD:(h + 1) * V_HEAD], ones)],
                         axis=0)
    for t in range(tm // MLA_TILE):
        vm_ref[t] = vt[:, t * MLA_TILE:(t + 1) * MLA_TILE]

    qst = (proj[:, P_QS:P_QS + SWA_OUT] * (SWA_HEAD_DIM ** -0.5 * LOG2E)).T.astype(BF16)
    vst = proj[:, P_VS:P_VS + SWA_KV].T.astype(BF16)
    vst = jnp.concatenate([blk for h in range(SWA_KV_HEADS)
                           for blk in (vst[h * SWA_HEAD_DIM:(h + 1) * SWA_HEAD_DIM], ones)], axis=0)
    for t in range(tm // WINDOW):
        qs_ref[t] = qst[:, t * WINDOW:(t + 1) * WINDOW]
        vs_ref[t] = vst[:, t * WINDOW:(t + 1) * WINDOW]
    ks_ref[...] = proj[:, P_KS:P_KS + SWA_KV].astype(BF16)


def _mla_kernel(q_ref, k_ref, v_ref, o_ref, s_ref, p_ref):
    nt, rows, t = q_ref.shape
    hp = rows // LANES
    vr = v_ref.shape[1] // hp
    key = lax.broadcasted_iota(jnp.int32, (t, t), 0)
    qry = lax.broadcasted_iota(jnp.int32, (t, t), 1)
    causal = key <= qry

    def issue(i, j):
        return tuple(jnp.dot(k_ref[j * t:(j + 1) * t, hh * LANES:(hh + 1) * LANES],
                             q_ref[i, hh * LANES:(hh + 1) * LANES, :], preferred_element_type=F32)
                     for hh in range(hp))

    def stash(sc, masked):
        tile_max = []
        for hh in range(hp):
            s = jnp.where(causal, sc[hh], NEG) if masked else sc[hh]
            s_ref[hh] = s
            tile_max.append(jnp.max(s, axis=0, keepdims=True))
        return tuple(tile_max)

    def fold_values(jp, state):
        new = []
        for hh in range(hp):
            m, acc, alpha = state[hh]
            pv = jnp.dot(v_ref[jp, hh * vr:(hh + 1) * vr, :], p_ref[hh], preferred_element_type=F32)
            new.append((m, alpha * acc + pv, alpha))
        return tuple(new)

    def softmax(tile_max, state):
        new = []
        for hh in range(hp):
            m, acc, _ = state[hh]
            m_new = jnp.maximum(m, tile_max[hh])
            p_ref[hh] = jnp.exp2(s_ref[hh] - m_new).astype(BF16)
            new.append((m_new, acc, jnp.exp2(m - m_new)))
        return tuple(new)

    def finish(i, state):
        o_ref[i] = jnp.concatenate([acc[0:V_HEAD] / acc[V_HEAD:V_HEAD + 1] for (_, acc, _) in state], axis=0)

    fresh = tuple((jnp.full((1, t), NEG, F32), jnp.zeros((vr, t), F32), jnp.zeros((1, t), F32))
                  for _ in range(hp))

    tiles = [(i, j) for i in range(nt) for j in range(i + 1)]
    tile_max = stash(issue(0, 0), True)
    state = fresh
    for n, (i, j) in enumerate(tiles):
        prev = tiles[n - 1] if n > 0 else None
        nxt = tiles[n + 1] if n + 1 < len(tiles) else None
        if nxt is not None:
            sc = issue(*nxt)
        if prev is not None:
            state = fold_values(prev[1], state)
            if prev[0] != i:
                finish(prev[0], state)
                state = fresh
        state = softmax(tile_max, state)
        if nxt is not None:
            tile_max = stash(sc, nxt[0] == nxt[1])
    finish(nt - 1, fold_values(nt - 1, state))


def _swa_kernel(bucket_ref, relb_ref, sink_ref, q_ref, k_ref, v_ref, o_ref, bias_ref, sinkrow_ref,
                s_ref, p_ref):
    w, dh, g = WINDOW, SWA_HEAD_DIM, SWA_GROUP
    nb = q_ref.shape[0]
    vr = v_ref.shape[1] // SWA_KV_HEADS

    @pl.when(pl.program_id(0) == 0)
    def _():
        bucket = bucket_ref[...]
        for kvh in range(SWA_KV_HEADS):
            for gg in range(g):
                h = kvh * g + gg
                b = jnp.zeros((2 * w, w), F32)
                for r in range(REL_BUCKETS):
                    b = jnp.where(bucket == r, relb_ref[r, h] * LOG2E, b)
                bias_ref[kvh, :, gg * w:(gg + 1) * w] = b
                sinkrow_ref[kvh, :, gg * w:(gg + 1) * w] = jnp.full((1, w), sink_ref[h] * LOG2E, F32)

    kj = lax.broadcasted_iota(jnp.int32, (2 * w, w), 0)
    qi = lax.broadcasted_iota(jnp.int32, (2 * w, w), 1)
    dist = qi + w - kj
    band = jnp.concatenate([(dist >= 0) & (dist < w)] * g, axis=1)
    zeros_q = jnp.zeros((dh, g * w), BF16)

    def key_rows(n):
        return (0, w) if n == 0 else ((n - 1) * w, 2 * w)

    def issue(n, kvh):
        qt = jnp.concatenate([q_ref[n, (kvh * g + gg) * dh:(kvh * g + gg + 1) * dh, :] for gg in range(g)],
                             axis=1)
        qt = jnp.concatenate([qt, zeros_q] if kvh == 0 else [zeros_q, qt], axis=0)
        start, size = key_rows(n)
        return jnp.dot(k_ref[start:start + size, :], qt, preferred_element_type=F32)

    def stash(sc, n, kvh):
        lo = 2 * w - sc.shape[0]
        s = jnp.where(band[lo:], sc + bias_ref[kvh, lo:, :], NEG)
        s_ref[lo:, :] = s
        return jnp.max(s, axis=0, keepdims=True)

    def softmax(tile_max, n, kvh):
        lo = 2 * w - key_rows(n)[1]
        sink = sinkrow_ref[kvh]
        m = jnp.maximum(tile_max, sink)
        p_ref[lo:, :] = jnp.exp2(s_ref[lo:, :] - m).astype(BF16)
        return jnp.exp2(sink - m)

    def fold_values(sink_term, n, kvh):
        vrows = slice(kvh * vr, (kvh + 1) * vr)
        if n == 0:
            vwin_t = v_ref[0, vrows, :]
        else:
            vwin_t = jnp.concatenate([v_ref[n - 1, vrows, :], v_ref[n, vrows, :]], axis=1)
        lo = 2 * w - key_rows(n)[1]
        pv = jnp.dot(vwin_t, p_ref[lo:, :], preferred_element_type=F32)
        o = pv[0:dh] / (pv[dh:dh + 1] + sink_term)
        for gg in range(g):
            h = kvh * g + gg
            o_ref[n, h * dh:(h + 1) * dh, :] = o[:, gg * w:(gg + 1) * w]

    units = [(n, kvh) for n in range(nb) for kvh in range(SWA_KV_HEADS)]
    tile_max = stash(issue(*units[0]), *units[0])
    sink_term = None
    for idx, unit in enumerate(units):
        nxt = units[idx + 1] if idx + 1 < len(units) else None
        if nxt is not None:
            sc = issue(*nxt)
        if idx > 0:
            fold_values(sink_term, *units[idx - 1])
        sink_term = softmax(tile_max, *unit)
        if nxt is not None:
            tile_max = stash(sc, *nxt)
    fold_values(sink_term, *units[-1])


def _out_ffn2_kernel(h1_ref, om_ref, os_ref, gom_ref, gos_ref, wom_ref, wos_ref, g2_ref, wg_ref, wu_ref,
                     wd_ref, gfin_ref, out_ref):
    om = jnp.concatenate([om_ref[t].T for t in range(om_ref.shape[0])], axis=0)
    osw = jnp.concatenate([os_ref[t].T for t in range(os_ref.shape[0])], axis=0)
    om = _rms(om, gom_ref[...]).astype(BF16)
    osw = _rms(osw, gos_ref[...]).astype(BF16)
    h2 = (h1_ref[...] + jnp.dot(om, wom_ref[...], preferred_element_type=F32)
          + jnp.dot(osw, wos_ref[...], preferred_element_type=F32))
    hn = _rms(h2, g2_ref[...]).astype(BF16)
    h3 = h2 + 0.5 * _swiglu(hn, wg_ref, wu_ref, wd_ref)
    out_ref[...] = _rms(h3, gfin_ref[...])


def _pack_w_in(w_in):
    d = w_in.shape[0]
    o_kpe = Q_LORA + KV_LORA
    o_qs = o_kpe + QK_ROPE
    kpe = jnp.zeros((d, LANES), w_in.dtype).at[:, ROPE_LO:ROPE_LO + QK_ROPE].set(w_in[:, o_kpe:o_qs])
    return jnp.concatenate([w_in[:, :o_kpe], kpe, w_in[:, o_qs:]], axis=1)


def _pack_w_q_b_t(w_q_b):
    r = w_q_b.shape[0]
    w = w_q_b.reshape(r, MLA_HEADS, QK_NOPE + QK_ROPE)
    w = jnp.pad(w, ((0, 0), (0, 0), (0, LANES - QK_NOPE - QK_ROPE)))
    return w.reshape(r, MLA_HEADS * LANES).T


def _pack_w_kv_b(w_kv_b):
    r = w_kv_b.shape[0]
    w = w_kv_b.reshape(r, MLA_HEADS, QK_NOPE + V_HEAD)
    wk = jnp.pad(w[:, :, :QK_NOPE], ((0, 0), (0, 0), (0, LANES - QK_NOPE))).reshape(r, MLA_HEADS * LANES)
    wv_t = w[:, :, QK_NOPE:].reshape(r, MLA_HEADS * V_HEAD).T
    return wk, wv_t


def _rope_angles(seq):
    pos = jnp.arange(seq, dtype=F32)
    inv_freq = ROPE_THETA ** (-jnp.arange(0, QK_ROPE, 2, dtype=F32) / QK_ROPE)
    ang = pos[:, None] * inv_freq[None, :]
    return jnp.cos(ang), jnp.sin(ang)


def _rope_table_lanes(seq):
    cos, sin = _rope_angles(seq)
    zeros = jnp.zeros((seq, LANES), F32)
    c = zeros.at[:, ROPE_LO:ROPE_LO + ROPE_HALF].set(cos).at[:, ROPE_LO + ROPE_HALF:ROPE_LO + QK_ROPE].set(cos)
    sa = zeros.at[:, ROPE_LO:ROPE_LO + ROPE_HALF].set(-sin)
    sb = zeros.at[:, ROPE_LO + ROPE_HALF:ROPE_LO + QK_ROPE].set(sin)
    return jnp.concatenate([c, sa, sb], axis=1)


def _t5_bucket(dist):
    n = jnp.maximum(dist, 0)
    max_exact = REL_BUCKETS // 2
    nf = jnp.maximum(n, 1).astype(F32)
    large = max_exact + (jnp.log(nf / max_exact) / math.log(REL_MAX_DIST / max_exact)
                         * (REL_BUCKETS - max_exact)).astype(jnp.int32)
    large = jnp.minimum(large, REL_BUCKETS - 1)
    return jnp.where(n < max_exact, n, large)


def _const_spec(shape):
    nd = len(shape)
    return pl.BlockSpec(shape, lambda *_: (0,) * nd, pipeline_mode=pl.Buffered(1))


def kernel(x, g_ffn1, w_ffn1_gate, w_ffn1_up, w_ffn1_down, g_mix, w_in, g_q_a, w_q_b, g_kv_a, w_kv_b,
           attn_sinks, rel_bias, g_out_mla, g_out_swa, w_o, g_ffn2, w_ffn2_gate, w_ffn2_up, w_ffn2_down,
           g_final):
    bsz, seq, d = x.shape
    n_tok = bsz * seq
    tm = TOKEN_TILE
    assert d == D_MODEL and seq % tm == 0 and tm % MLA_TILE == 0 and tm % WINDOW == 0
    assert g_ffn1.shape[0] == 1, "single layer"

    row = lambda a: a.reshape(1, -1).astype(F32)
    xf = x.reshape(n_tok, d)

    win = _pack_w_in(w_in[0]).astype(BF16)
    wqbt = _pack_w_q_b_t(w_q_b[0]).astype(BF16)
    wkb, wvbt = _pack_w_kv_b(w_kv_b[0])
    wkb, wvbt = wkb.astype(BF16), wvbt.astype(BF16)
    q_scale = (QK_NOPE + QK_ROPE) ** -0.5 * LOG2E
    cos, sin = _rope_angles(seq)
    rope_q = jnp.concatenate([cos.T, sin.T], axis=0) * q_scale
    rope_k = _rope_table_lanes(seq)

    tiles = n_tok // tm
    tps = seq // tm
    nt_mla, nb_swa = seq // MLA_TILE, seq // WINDOW
    tile_spec = lambda width: pl.BlockSpec((tm, width), lambda i: (i, 0))
    fm_spec = lambda rows, tok: pl.BlockSpec((None, tm // tok, rows, tok), lambda i: (i // tps, i % tps, 0, 0))

    weights1 = [row(g_ffn1[0]), w_ffn1_gate[0].astype(BF16), w_ffn1_up[0].astype(BF16),
                w_ffn1_down[0].astype(BF16), row(g_mix[0]), win, row(g_q_a[0]), wqbt, row(g_kv_a[0]), wkb, wvbt]
    h1, q_mla, k_mla, v_mla, q_swa, k_swa, v_swa = pl.pallas_call(
        functools.partial(_ffn1_proj_kernel, q_scale=q_scale),
        grid=(tiles,),
        in_specs=[tile_spec(d)] + [_const_spec(a.shape) for a in weights1]
                 + [pl.BlockSpec((QK_ROPE, tm), lambda i: (0, i % tps)),
                    pl.BlockSpec((tm, 3 * LANES), lambda i: (i % tps, 0))],
        out_specs=[tile_spec(d), fm_spec(MLA_HEADS * LANES, MLA_TILE), tile_spec(MLA_HEADS * LANES),
                   fm_spec(MLA_HEADS * MLA_V_ROWS, MLA_TILE), fm_spec(SWA_OUT, WINDOW), tile_spec(SWA_KV),
                   fm_spec(SWA_KV_HEADS * SWA_V_ROWS, WINDOW)],
        out_shape=[jax.ShapeDtypeStruct((n_tok, d), F32),
                   jax.ShapeDtypeStruct((bsz, nt_mla, MLA_HEADS * LANES, MLA_TILE), BF16),
                   jax.ShapeDtypeStruct((n_tok, MLA_HEADS * LANES), BF16),
                   jax.ShapeDtypeStruct((bsz, nt_mla, MLA_HEADS * MLA_V_ROWS, MLA_TILE), BF16),
                   jax.ShapeDtypeStruct((bsz, nb_swa, SWA_OUT, WINDOW), BF16),
                   jax.ShapeDtypeStruct((n_tok, SWA_KV), BF16),
                   jax.ShapeDtypeStruct((bsz, nb_swa, SWA_KV_HEADS * SWA_V_ROWS, WINDOW), BF16)],
        compiler_params=pltpu.CompilerParams(dimension_semantics=("arbitrary",),
                                             vmem_limit_bytes=VMEM_LIMIT),
        name="ffn1_proj",
    )(xf, *weights1, rope_q, rope_k)

    hp = MLA_HEADS_PER_STEP
    o_mla = pl.pallas_call(
        _mla_kernel,
        grid=(bsz, MLA_HEADS // hp),
        in_specs=[pl.BlockSpec((None, nt_mla, hp * LANES, MLA_TILE), lambda b, p: (b, 0, p, 0)),
                  pl.BlockSpec((seq, hp * LANES), lambda b, p: (b, p)),
                  pl.BlockSpec((None, nt_mla, hp * MLA_V_ROWS, MLA_TILE), lambda b, p: (b, 0, p, 0))],
        out_specs=pl.BlockSpec((None, nt_mla, hp * V_HEAD, MLA_TILE), lambda b, p: (b, 0, p, 0)),
        out_shape=jax.ShapeDtypeStruct((bsz, nt_mla, MLA_OUT, MLA_TILE), F32),
        scratch_shapes=[pltpu.VMEM((hp, MLA_TILE, MLA_TILE), F32), pltpu.VMEM((hp, MLA_TILE, MLA_TILE), BF16)],
        compiler_params=pltpu.CompilerParams(dimension_semantics=("arbitrary", "arbitrary"),
                                             vmem_limit_bytes=VMEM_LIMIT),
        name="mla",
    )(q_mla, k_mla, v_mla)

    kj = jnp.arange(2 * WINDOW)[:, None]
    qi = jnp.arange(WINDOW)[None, :]
    bucket_t = _t5_bucket(qi + WINDOW - kj).astype(jnp.int32)
    smem = pl.BlockSpec(memory_space=pltpu.SMEM)
    o_swa = pl.pallas_call(
        _swa_kernel,
        grid=(bsz,),
        in_specs=[_const_spec(bucket_t.shape), smem, smem,
                  pl.BlockSpec((None, nb_swa, SWA_OUT, WINDOW), lambda b: (b, 0, 0, 0)),
                  pl.BlockSpec((seq, SWA_KV), lambda b: (b, 0)),
                  pl.BlockSpec((None, nb_swa, SWA_KV_HEADS * SWA_V_ROWS, WINDOW), lambda b: (b, 0, 0, 0))],
        out_specs=pl.BlockSpec((None, nb_swa, SWA_OUT, WINDOW), lambda b: (b, 0, 0, 0)),
        out_shape=jax.ShapeDtypeStruct((bsz, nb_swa, SWA_OUT, WINDOW), F32),
        scratch_shapes=[pltpu.VMEM((SWA_KV_HEADS, 2 * WINDOW, SWA_GROUP * WINDOW), F32),
                        pltpu.VMEM((SWA_KV_HEADS, 1, SWA_GROUP * WINDOW), F32),
                        pltpu.VMEM((2 * WINDOW, SWA_GROUP * WINDOW), F32),
                        pltpu.VMEM((2 * WINDOW, SWA_GROUP * WINDOW), BF16)],
        compiler_params=pltpu.CompilerParams(dimension_semantics=("arbitrary",),
                                             vmem_limit_bytes=VMEM_LIMIT),
        name="swa",
    )(bucket_t, rel_bias.astype(F32), attn_sinks[0].astype(F32), q_swa, k_swa, v_swa)

    wo = w_o[0].astype(BF16)
    weights4 = [row(g_out_mla[0]), row(g_out_swa[0]), wo[:MLA_OUT], wo[MLA_OUT:], row(g_ffn2[0]),
                w_ffn2_gate[0].astype(BF16), w_ffn2_up[0].astype(BF16), w_ffn2_down[0].astype(BF16),
                row(g_final)]
    out = pl.pallas_call(
        _out_ffn2_kernel,
        grid=(tiles,),
        in_specs=[tile_spec(d), fm_spec(MLA_OUT, MLA_TILE), fm_spec(SWA_OUT, WINDOW)]
                 + [_const_spec(a.shape) for a in weights4],
        out_specs=tile_spec(d),
        out_shape=jax.ShapeDtypeStruct((n_tok, d), F32),
        compiler_params=pltpu.CompilerParams(dimension_semantics=("arbitrary",),
                                             vmem_limit_bytes=VMEM_LIMIT),
        name="out_ffn2",
    )(h1, o_mla, o_swa, *weights4)
    return out.reshape(bsz, seq, d)
```

```python
import functools
import math

import jax
import jax.numpy as jnp
from jax import lax
from jax.experimental import pallas as pl
from jax.experimental.pallas import tpu as pltpu

F32 = jnp.float32
BF16 = jnp.bfloat16

EPS = 1e-6
NEG = -1e30
LOG2E = math.log2(math.e)

D_MODEL = 1024
D_FF = 2816
MLA_HEADS = 8
Q_LORA = 256
KV_LORA = 128
QK_NOPE = 64
QK_ROPE = 32
V_HEAD = 64
ROPE_THETA = 10000.0
SWA_HEADS = 8
SWA_KV_HEADS = 2
SWA_HEAD_DIM = 64
SWA_GROUP = SWA_HEADS // SWA_KV_HEADS
WINDOW = 128
REL_BUCKETS = 32
REL_MAX_DIST = 128
MLA_OUT = MLA_HEADS * V_HEAD
SWA_OUT = SWA_HEADS * SWA_HEAD_DIM
SWA_KV = SWA_KV_HEADS * SWA_HEAD_DIM

LANES = 128
ROPE_LO = QK_NOPE
ROPE_HALF = QK_ROPE // 2

P_CQ = 0
P_CKV = P_CQ + Q_LORA
P_KPE = P_CKV + KV_LORA
P_QS = P_KPE + LANES
P_KS = P_QS + SWA_OUT
P_VS = P_KS + SWA_KV
P_END = P_VS + SWA_KV

TOKEN_TILE = 512
MLA_TILE = 256
ONES_ROWS = 16
MLA_V_ROWS = V_HEAD + ONES_ROWS
SWA_V_ROWS = SWA_HEAD_DIM + ONES_ROWS
VMEM_LIMIT = 56 * 1024 * 1024

_NT = (((1,), (1,)), ((), ()))


def _rms(x, g):
    return x * lax.rsqrt(jnp.mean(x * x, axis=-1, keepdims=True) + EPS) * g


def _swiglu(xn, wg_ref, wu_ref, wd_ref):
    gate = jnp.dot(xn, wg_ref[...], preferred_element_type=F32)
    up = jnp.dot(xn, wu_ref[...], preferred_element_type=F32)
    act = (gate * jax.nn.sigmoid(gate) * up).astype(BF16)
    return jnp.dot(act, wd_ref[...], preferred_element_type=F32)


def _rope_lanes(x, tab_ref):
    c = tab_ref[:, 0:LANES]
    sa = tab_ref[:, LANES:2 * LANES]
    sb = tab_ref[:, 2 * LANES:3 * LANES]
    return (x * c + pltpu.roll(x, LANES - ROPE_HALF, 1) * sa + pltpu.roll(x, ROPE_HALF, 1) * sb)


def _ffn1_proj_kernel(x_ref, g1_ref, wg_ref, wu_ref, wd_ref, gmix_ref, win_ref, gqa_ref, wqbt_ref,
                      gkva_ref, wkb_ref, wvbt_ref, ropeq_ref, ropek_ref,
                      h1_ref, qm_ref, km_ref, vm_ref, qs_ref, ks_ref, vs_ref, *, q_scale):
    tm = x_ref.shape[0]
    x = x_ref[...]
    xn = _rms(x, g1_ref[...]).astype(BF16)
    h1 = x + 0.5 * _swiglu(xn, wg_ref, wu_ref, wd_ref)
    h1_ref[...] = h1

    u = _rms(h1, gmix_ref[...]).astype(BF16)
    proj = jnp.dot(u, win_ref[...], preferred_element_type=F32)

    cq = _rms(proj[:, P_CQ:P_CQ + Q_LORA], gqa_ref[...]).astype(BF16)
    qt = lax.dot_general(wqbt_ref[...], cq, _NT, preferred_element_type=F32)
    cs = ropeq_ref[0:ROPE_HALF, :]
    sn = ropeq_ref[ROPE_HALF:QK_ROPE, :]
    for h in range(MLA_HEADS):
        r = h * LANES
        x1 = qt[r + ROPE_LO:r + ROPE_LO + ROPE_HALF]
        x2 = qt[r + ROPE_LO + ROPE_HALF:r + ROPE_LO + QK_ROPE]
        qh = jnp.concatenate([qt[r:r + QK_NOPE] * q_scale, x1 * cs - x2 * sn, x2 * cs + x1 * sn,
                              qt[r + ROPE_LO + QK_ROPE:r + LANES]], axis=0).astype(BF16)
        for t in range(tm // MLA_TILE):
            qm_ref[t, r:r + LANES, :] = qh[:, t * MLA_TILE:(t + 1) * MLA_TILE]

    ckv = _rms(proj[:, P_CKV:P_CKV + KV_LORA], gkva_ref[...]).astype(BF16)
    kn = jnp.dot(ckv, wkb_ref[...], preferred_element_type=F32)
    kpe = _rope_lanes(proj[:, P_KPE:P_KPE + LANES], ropek_ref)
    for h in range(MLA_HEADS):
        km_ref[:, h * LANES:(h + 1) * LANES] = (kn[:, h * LANES:(h + 1) * LANES] + kpe).astype(BF16)
    vt = lax.dot_general(wvbt_ref[...], ckv, _NT, preferred_element_type=F32).astype(BF16)
    ones = jnp.ones((ONES_ROWS, tm), BF16)
    vt = jnp.concatenate([blk for h in range(MLA_HEADS) for blk in (vt[h * V_HEAD:(h + 1) * V_HEAD], ones)],
                         axis=0)
    for t in range(tm // MLA_TILE):
        vm_ref[t] = vt[:, t * MLA_TILE:(t + 1) * MLA_TILE]

    qst = (proj[:, P_QS:P_QS + SWA_OUT] * (SWA_HEAD_DIM ** -0.5 * LOG2E)).T.astype(BF16)
    vst = proj[:, P_VS:P_VS + SWA_KV].T.astype(BF16)
    vst = jnp.concatenate([blk for h in range(SWA_KV_HEADS)
                           for blk in (vst[h * SWA_HEAD_DIM:(h + 1) * SWA_HEAD_DIM], ones)], axis=0)
    for t in range(tm // WINDOW):
        qs_ref[t] = qst[:, t * WINDOW:(t + 1) * WINDOW]
        vs_ref[t] = vst[:, t * WINDOW:(t + 1) * WINDOW]
    ks_ref[...] = proj[:, P_KS:P_KS + SWA_KV].astype(BF16)


def _mla_pipeline(q_ref, k_ref, v_ref, o_ref, s_ref, p_ref, acc_ref):
    nt, rows, t = q_ref.shape
    hp = rows // LANES
    vr = v_ref.shape[1] // hp
    key = lax.broadcasted_iota(jnp.int32, (t, t), 0)
    qry = lax.broadcasted_iota(jnp.int32, (t, t), 1)
    causal = key <= qry

    def issue(i, j):
        return tuple(jnp.dot(k_ref[j * t:(j + 1) * t, hh * LANES:(hh + 1) * LANES],
                             q_ref[i, hh * LANES:(hh + 1) * LANES, :], preferred_element_type=F32)
                     for hh in range(hp))

    def stash(sc, masked):
        tile_max = []
        for hh in range(hp):
            s = jnp.where(causal, sc[hh], NEG) if masked else sc[hh]
            s_ref[hh] = s
            tile_max.append(jnp.max(s, axis=0, keepdims=True))
        return tuple(tile_max)

    def fold_values(jp, state):
        for hh in range(hp):
            pv = jnp.dot(v_ref[jp, hh * vr:(hh + 1) * vr, :], p_ref[hh], preferred_element_type=F32)
            acc_ref[hh] = pv if jp == 0 else state[hh][1] * acc_ref[hh] + pv

    def softmax(tile_max, state):
        new = []
        for hh in range(hp):
            m = state[hh][0]
            m_new = jnp.maximum(m, tile_max[hh])
            p_ref[hh] = jnp.exp2(s_ref[hh] - m_new).astype(BF16)
            new.append((m_new, jnp.exp2(m - m_new)))
        return tuple(new)

    def finish(i):
        o_ref[i] = jnp.concatenate([acc_ref[hh, 0:V_HEAD, :] / acc_ref[hh, V_HEAD:V_HEAD + 1, :]
                                    for hh in range(hp)], axis=0)

    fresh = tuple((jnp.full((1, t), NEG, F32), None) for _ in range(hp))

    tiles = [(i, j) for i in range(nt) for j in range(i + 1)]
    tile_max = stash(issue(0, 0), True)
    state = fresh
    yield
    for n, (i, j) in enumerate(tiles):
        prev = tiles[n - 1] if n > 0 else None
        nxt = tiles[n + 1] if n + 1 < len(tiles) else None
        if nxt is not None:
            sc = issue(*nxt)
        if prev is not None:
            fold_values(prev[1], state)
            if prev[0] != i:
                finish(prev[0])
                state = fresh
        state = softmax(tile_max, state)
        if nxt is not None:
            tile_max = stash(sc, nxt[0] == nxt[1])
        yield
    fold_values(nt - 1, state)
    finish(nt - 1)


def _mla_pipeline_steps(nt):
    return nt * (nt + 1) // 2 + 2


def _swa_pipeline(kvh, q_ref, k_ref, v_ref, o_ref, bias_ref, sinkrow_ref, s_ref, p_ref):
    w, dh, g = WINDOW, SWA_HEAD_DIM, SWA_GROUP
    nb = q_ref.shape[0]
    kj = lax.broadcasted_iota(jnp.int32, (2 * w, w), 0)
    qi = lax.broadcasted_iota(jnp.int32, (2 * w, w), 1)
    dist = qi + w - kj
    band = jnp.concatenate([(dist >= 0) & (dist < w)] * g, axis=1)
    zeros_q = jnp.zeros((dh, g * w), BF16)

    def key_rows(n):
        return (0, w) if n == 0 else ((n - 1) * w, 2 * w)

    def issue(n):
        qt = jnp.concatenate([q_ref[n, gg * dh:(gg + 1) * dh, :] for gg in range(g)], axis=1)
        qt = jnp.concatenate([jnp.where(kvh == hd, qt, zeros_q) for hd in range(SWA_KV_HEADS)], axis=0)
        start, size = key_rows(n)
        return jnp.dot(k_ref[start:start + size, :], qt, preferred_element_type=F32)

    def stash(sc):
        lo = 2 * w - sc.shape[0]
        s = jnp.where(band[lo:], sc + bias_ref[kvh, lo:, :], NEG)
        s_ref[lo:, :] = s
        return jnp.max(s, axis=0, keepdims=True)

    def softmax(tile_max, n):
        lo = 2 * w - key_rows(n)[1]
        sink = sinkrow_ref[kvh]
        m = jnp.maximum(tile_max, sink)
        p_ref[lo:, :] = jnp.exp2(s_ref[lo:, :] - m).astype(BF16)
        return jnp.exp2(sink - m)

    def fold_values(sink_term, n):
        vwin_t = v_ref[0] if n == 0 else jnp.concatenate([v_ref[n - 1], v_ref[n]], axis=1)
        lo = 2 * w - key_rows(n)[1]
        pv = jnp.dot(vwin_t, p_ref[lo:, :], preferred_element_type=F32)
        o = pv[0:dh] / (pv[dh:dh + 1] + sink_term)
        for gg in range(g):
            o_ref[n, gg * dh:(gg + 1) * dh, :] = o[:, gg * w:(gg + 1) * w]

    tile_max = stash(issue(0))
    sink_term = None
    yield
    for n in range(nb):
        if n + 1 < nb:
            sc = issue(n + 1)
        if n > 0:
            fold_values(sink_term, n - 1)
        sink_term = softmax(tile_max, n)
        if n + 1 < nb:
            tile_max = stash(sc)
        yield
    fold_values(sink_term, nb - 1)


def _swa_pipeline_steps(nb):
    return nb + 2


def _run_interleaved(pipelines):
    done = [0] * len(pipelines)
    while True:
        live = [k for k, (_, n) in enumerate(pipelines) if done[k] < n]
        if not live:
            return
        k = min(live, key=lambda k: done[k] / pipelines[k][1])
        next(pipelines[k][0], None)
        done[k] += 1


def _attn_kernel(bucket_ref, relb_ref, sink_ref, q_ref, k_ref, v_ref, qs_ref, ks_ref, vs_ref, o_ref, os_ref,
                 s_ref, p_ref, acc_ref, bias_ref, sinkrow_ref, ss_ref, ps_ref):
    w, g = WINDOW, SWA_GROUP

    @pl.when((pl.program_id(0) == 0) & (pl.program_id(1) == 0))
    def _():
        bucket = bucket_ref[...]
        for kvh in range(SWA_KV_HEADS):
            for gg in range(g):
                h = kvh * g + gg
                b = jnp.zeros((2 * w, w), F32)
                for r in range(REL_BUCKETS):
                    b = jnp.where(bucket == r, relb_ref[r, h] * LOG2E, b)
                bias_ref[kvh, :, gg * w:(gg + 1) * w] = b
                sinkrow_ref[kvh, :, gg * w:(gg + 1) * w] = jnp.full((1, w), sink_ref[h] * LOG2E, F32)

    _run_interleaved([
        (_mla_pipeline(q_ref, k_ref, v_ref, o_ref, s_ref, p_ref, acc_ref), _mla_pipeline_steps(q_ref.shape[0])),
        (_swa_pipeline(pl.program_id(1), qs_ref, ks_ref, vs_ref, os_ref, bias_ref, sinkrow_ref, ss_ref, ps_ref),
         _swa_pipeline_steps(qs_ref.shape[0])),
    ])


def _attn_features(om_ref, os_ref, gom_ref, gos_ref):
    om = jnp.concatenate([om_ref[t].T for t in range(om_ref.shape[0])], axis=0)
    osw = jnp.concatenate([os_ref[t].T for t in range(os_ref.shape[0])], axis=0)
    return _rms(om, gom_ref[...]).astype(BF16), _rms(osw, gos_ref[...]).astype(BF16)


def _out_ffn2_kernel(h1_ref, om_next_ref, os_next_ref, om_first_ref, os_first_ref, gom_ref, gos_ref, wom_ref,
                     wos_ref, g2_ref, wg_ref, wu_ref, wd_ref, gfin_ref, out_ref, am_ref, as_ref, h3_ref):
    @pl.when(pl.program_id(0) == 0)
    def _():
        am_ref[...], as_ref[...] = _attn_features(om_first_ref, os_first_ref, gom_ref, gos_ref)
        h3_ref[...] = jnp.zeros_like(h3_ref)

    h2 = (h1_ref[...] + jnp.dot(am_ref[...], wom_ref[...], preferred_element_type=F32)
          + jnp.dot(as_ref[...], wos_ref[...], preferred_element_type=F32))
    out_ref[...] = _rms(h3_ref[...], gfin_ref[...])
    am_ref[...], as_ref[...] = _attn_features(om_next_ref, os_next_ref, gom_ref, gos_ref)
    hn = _rms(h2, g2_ref[...]).astype(BF16)
    h3_ref[...] = h2 + 0.5 * _swiglu(hn, wg_ref, wu_ref, wd_ref)


def _pack_w_in(w_in):
    d = w_in.shape[0]
    o_kpe = Q_LORA + KV_LORA
    o_qs = o_kpe + QK_ROPE
    kpe = jnp.zeros((d, LANES), w_in.dtype).at[:, ROPE_LO:ROPE_LO + QK_ROPE].set(w_in[:, o_kpe:o_qs])
    return jnp.concatenate([w_in[:, :o_kpe], kpe, w_in[:, o_qs:]], axis=1)


def _pack_w_q_b_t(w_q_b):
    r = w_q_b.shape[0]
    w = w_q_b.reshape(r, MLA_HEADS, QK_NOPE + QK_ROPE)
    w = jnp.pad(w, ((0, 0), (0, 0), (0, LANES - QK_NOPE - QK_ROPE)))
    return w.reshape(r, MLA_HEADS * LANES).T


def _pack_w_kv_b(w_kv_b):
    r = w_kv_b.shape[0]
    w = w_kv_b.reshape(r, MLA_HEADS, QK_NOPE + V_HEAD)
    wk = jnp.pad(w[:, :, :QK_NOPE], ((0, 0), (0, 0), (0, LANES - QK_NOPE))).reshape(r, MLA_HEADS * LANES)
    wv_t = w[:, :, QK_NOPE:].reshape(r, MLA_HEADS * V_HEAD).T
    return wk, wv_t


def _rope_angles(seq):
    pos = jnp.arange(seq, dtype=F32)
    inv_freq = ROPE_THETA ** (-jnp.arange(0, QK_ROPE, 2, dtype=F32) / QK_ROPE)
    ang = pos[:, None] * inv_freq[None, :]
    return jnp.cos(ang), jnp.sin(ang)


def _rope_table_lanes(seq):
    cos, sin = _rope_angles(seq)
    zeros = jnp.zeros((seq, LANES), F32)
    c = zeros.at[:, ROPE_LO:ROPE_LO + ROPE_HALF].set(cos).at[:, ROPE_LO + ROPE_HALF:ROPE_LO + QK_ROPE].set(cos)
    sa = zeros.at[:, ROPE_LO:ROPE_LO + ROPE_HALF].set(-sin)
    sb = zeros.at[:, ROPE_LO + ROPE_HALF:ROPE_LO + QK_ROPE].set(sin)
    return jnp.concatenate([c, sa, sb], axis=1)


def _t5_bucket(dist):
    n = jnp.maximum(dist, 0)
    max_exact = REL_BUCKETS // 2
    nf = jnp.maximum(n, 1).astype(F32)
    large = max_exact + (jnp.log(nf / max_exact) / math.log(REL_MAX_DIST / max_exact)
                         * (REL_BUCKETS - max_exact)).astype(jnp.int32)
    large = jnp.minimum(large, REL_BUCKETS - 1)
    return jnp.where(n < max_exact, n, large)


def _const_spec(shape):
    nd = len(shape)
    return pl.BlockSpec(shape, lambda *_: (0,) * nd, pipeline_mode=pl.Buffered(1))


def kernel(x, g_ffn1, w_ffn1_gate, w_ffn1_up, w_ffn1_down, g_mix, w_in, g_q_a, w_q_b, g_kv_a, w_kv_b,
           attn_sinks, rel_bias, g_out_mla, g_out_swa, w_o, g_ffn2, w_ffn2_gate, w_ffn2_up, w_ffn2_down,
           g_final):
    bsz, seq, d = x.shape
    n_tok = bsz * seq
    tm = TOKEN_TILE
    assert d == D_MODEL and seq % tm == 0 and tm % MLA_TILE == 0 and tm % WINDOW == 0
    assert g_ffn1.shape[0] == 1, "single layer"

    row = lambda a: a.reshape(1, -1).astype(F32)
    xf = x.reshape(n_tok, d)

    win = _pack_w_in(w_in[0]).astype(BF16)
    wqbt = _pack_w_q_b_t(w_q_b[0]).astype(BF16)
    wkb, wvbt = _pack_w_kv_b(w_kv_b[0])
    wkb, wvbt = wkb.astype(BF16), wvbt.astype(BF16)
    q_scale = (QK_NOPE + QK_ROPE) ** -0.5 * LOG2E
    cos, sin = _rope_angles(seq)
    rope_q = jnp.concatenate([cos.T, sin.T], axis=0) * q_scale
    rope_k = _rope_table_lanes(seq)

    tiles = n_tok // tm
    tps = seq // tm
    nt_mla, nb_swa = seq // MLA_TILE, seq // WINDOW
    tile_spec = lambda width: pl.BlockSpec((tm, width), lambda i: (i, 0))
    fm_spec = lambda rows, tok: pl.BlockSpec((None, tm // tok, rows, tok), lambda i: (i // tps, i % tps, 0, 0))

    weights1 = [row(g_ffn1[0]), w_ffn1_gate[0].astype(BF16), w_ffn1_up[0].astype(BF16),
                w_ffn1_down[0].astype(BF16), row(g_mix[0]), win, row(g_q_a[0]), wqbt, row(g_kv_a[0]), wkb, wvbt]
    h1, q_mla, k_mla, v_mla, q_swa, k_swa, v_swa = pl.pallas_call(
        functools.partial(_ffn1_proj_kernel, q_scale=q_scale),
        grid=(tiles,),
        in_specs=[tile_spec(d)] + [_const_spec(a.shape) for a in weights1]
                 + [pl.BlockSpec((QK_ROPE, tm), lambda i: (0, i % tps)),
                    pl.BlockSpec((tm, 3 * LANES), lambda i: (i % tps, 0))],
        out_specs=[tile_spec(d), fm_spec(MLA_HEADS * LANES, MLA_TILE), tile_spec(MLA_HEADS * LANES),
                   fm_spec(MLA_HEADS * MLA_V_ROWS, MLA_TILE), fm_spec(SWA_OUT, WINDOW), tile_spec(SWA_KV),
                   fm_spec(SWA_KV_HEADS * SWA_V_ROWS, WINDOW)],
        out_shape=[jax.ShapeDtypeStruct((n_tok, d), F32),
                   jax.ShapeDtypeStruct((bsz, nt_mla, MLA_HEADS * LANES, MLA_TILE), BF16),
                   jax.ShapeDtypeStruct((n_tok, MLA_HEADS * LANES), BF16),
                   jax.ShapeDtypeStruct((bsz, nt_mla, MLA_HEADS * MLA_V_ROWS, MLA_TILE), BF16),
                   jax.ShapeDtypeStruct((bsz, nb_swa, SWA_OUT, WINDOW), BF16),
                   jax.ShapeDtypeStruct((n_tok, SWA_KV), BF16),
                   jax.ShapeDtypeStruct((bsz, nb_swa, SWA_KV_HEADS * SWA_V_ROWS, WINDOW), BF16)],
        compiler_params=pltpu.CompilerParams(dimension_semantics=("arbitrary",),
                                             vmem_limit_bytes=VMEM_LIMIT),
        name="ffn1_proj",
    )(xf, *weights1, rope_q, rope_k)

    hp = SWA_GROUP
    assert MLA_HEADS // hp == SWA_KV_HEADS
    kj = jnp.arange(2 * WINDOW)[:, None]
    qi = jnp.arange(WINDOW)[None, :]
    bucket_t = _t5_bucket(qi + WINDOW - kj).astype(jnp.int32)
    smem = pl.BlockSpec(memory_space=pltpu.SMEM)
    group = lambda tiles_, rows, tok: pl.BlockSpec((None, tiles_, rows, tok), lambda b, p: (b, 0, p, 0))
    o_mla, o_swa = pl.pallas_call(
        _attn_kernel,
        grid=(bsz, SWA_KV_HEADS),
        in_specs=[_const_spec(bucket_t.shape), smem, smem,
                  group(nt_mla, hp * LANES, MLA_TILE),
                  pl.BlockSpec((seq, hp * LANES), lambda b, p: (b, p)),
                  group(nt_mla, hp * MLA_V_ROWS, MLA_TILE),
                  group(nb_swa, hp * SWA_HEAD_DIM, WINDOW),
                  pl.BlockSpec((seq, SWA_KV), lambda b, p: (b, 0)),
                  group(nb_swa, SWA_V_ROWS, WINDOW)],
        out_specs=[group(nt_mla, hp * V_HEAD, MLA_TILE), group(nb_swa, hp * SWA_HEAD_DIM, WINDOW)],
        out_shape=[jax.ShapeDtypeStruct((bsz, nt_mla, MLA_OUT, MLA_TILE), F32),
                   jax.ShapeDtypeStruct((bsz, nb_swa, SWA_OUT, WINDOW), F32)],
        scratch_shapes=[pltpu.VMEM((hp, MLA_TILE, MLA_TILE), F32), pltpu.VMEM((hp, MLA_TILE, MLA_TILE), BF16),
                        pltpu.VMEM((hp, MLA_V_ROWS, MLA_TILE), F32),
                        pltpu.VMEM((SWA_KV_HEADS, 2 * WINDOW, SWA_GROUP * WINDOW), F32),
                        pltpu.VMEM((SWA_KV_HEADS, 1, SWA_GROUP * WINDOW), F32),
                        pltpu.VMEM((2 * WINDOW, SWA_GROUP * WINDOW), F32),
                        pltpu.VMEM((2 * WINDOW, SWA_GROUP * WINDOW), BF16)],
        compiler_params=pltpu.CompilerParams(dimension_semantics=("arbitrary", "arbitrary"),
                                             vmem_limit_bytes=VMEM_LIMIT),
        name="attn",
    )(bucket_t, rel_bias.astype(F32), attn_sinks[0].astype(F32), q_mla, k_mla, v_mla, q_swa, k_swa, v_swa)

    wo = w_o[0].astype(BF16)
    weights4 = [row(g_out_mla[0]), row(g_out_swa[0]), wo[:MLA_OUT], wo[MLA_OUT:], row(g_ffn2[0]),
                w_ffn2_gate[0].astype(BF16), w_ffn2_up[0].astype(BF16), w_ffn2_down[0].astype(BF16),
                row(g_final)]
    last = tiles - 1
    fm_at = lambda rows, tok, tile_of: pl.BlockSpec(
        (None, tm // tok, rows, tok), lambda s: (tile_of(s) // tps, tile_of(s) % tps, 0, 0))
    fm_first = lambda rows, tok: pl.BlockSpec((None, tm // tok, rows, tok), lambda s: (0, 0, 0, 0),
                                              pipeline_mode=pl.Buffered(1))
    nxt = lambda s: jnp.minimum(s + 1, last)
    out = pl.pallas_call(
        _out_ffn2_kernel,
        grid=(tiles + 1,),
        in_specs=[pl.BlockSpec((tm, d), lambda s: (jnp.minimum(s, last), 0)),
                  fm_at(MLA_OUT, MLA_TILE, nxt), fm_at(SWA_OUT, WINDOW, nxt),
                  fm_first(MLA_OUT, MLA_TILE), fm_first(SWA_OUT, WINDOW)]
                 + [_const_spec(a.shape) for a in weights4],
        out_specs=pl.BlockSpec((tm, d), lambda s: (jnp.maximum(s - 1, 0), 0)),
        out_shape=jax.ShapeDtypeStruct((n_tok, d), F32),
        scratch_shapes=[pltpu.VMEM((tm, MLA_OUT), BF16), pltpu.VMEM((tm, SWA_OUT), BF16),
                        pltpu.VMEM((tm, d), F32)],
        compiler_params=pltpu.CompilerParams(dimension_semantics=("arbitrary",),
                                             vmem_limit_bytes=VMEM_LIMIT),
        name="out_ffn2",
    )(h1, o_mla, o_swa, o_mla, o_swa, *weights4)
    return out.reshape(bsz, seq, d)
```

```python
import functools
import math

import jax
import jax.numpy as jnp
from jax import lax
from jax.experimental import pallas as pl
from jax.experimental.pallas import tpu as pltpu

F32 = jnp.float32
BF16 = jnp.bfloat16

EPS = 1e-6
NEG = -1e30
LOG2E = math.log2(math.e)

D_MODEL = 1024
D_FF = 2816
MLA_HEADS = 8
Q_LORA = 256
KV_LORA = 128
QK_NOPE = 64
QK_ROPE = 32
V_HEAD = 64
ROPE_THETA = 10000.0
SWA_HEADS = 8
SWA_KV_HEADS = 2
SWA_HEAD_DIM = 64
SWA_GROUP = SWA_HEADS // SWA_KV_HEADS
WINDOW = 128
REL_BUCKETS = 32
REL_MAX_DIST = 128
MLA_OUT = MLA_HEADS * V_HEAD
SWA_OUT = SWA_HEADS * SWA_HEAD_DIM
SWA_KV = SWA_KV_HEADS * SWA_HEAD_DIM

LANES = 128
ROPE_LO = QK_NOPE
ROPE_HALF = QK_ROPE // 2

P_CQ = 0
P_CKV = P_CQ + Q_LORA
P_KPE = P_CKV + KV_LORA
P_QS = P_KPE + LANES
P_KS = P_QS + SWA_OUT
P_VS = P_KS + SWA_KV
P_END = P_VS + SWA_KV

TOKEN_TILE = 512
MLA_TILE = 256
MLA_HEADS_PER_STEP = 4
ONES_ROWS = 16
MLA_V_ROWS = V_HEAD + ONES_ROWS
SWA_V_ROWS = SWA_HEAD_DIM + ONES_ROWS
VMEM_LIMIT = 56 * 1024 * 1024

_NT = (((1,), (1,)), ((), ()))


def _rms(x, g):
    return x * lax.rsqrt(jnp.mean(x * x, axis=-1, keepdims=True) + EPS) * g


def _swiglu(xn, wg_ref, wu_ref, wd_ref):
    gate = jnp.dot(xn, wg_ref[...], preferred_element_type=F32)
    up = jnp.dot(xn, wu_ref[...], preferred_element_type=F32)
    act = (gate * jax.nn.sigmoid(gate) * up).astype(BF16)
    return jnp.dot(act, wd_ref[...], preferred_element_type=F32)


def _rope_lanes(x, tab_ref):
    c = tab_ref[:, 0:LANES]
    sa = tab_ref[:, LANES:2 * LANES]
    sb = tab_ref[:, 2 * LANES:3 * LANES]
    return (x * c + pltpu.roll(x, LANES - ROPE_HALF, 1) * sa + pltpu.roll(x, ROPE_HALF, 1) * sb)


def _ffn1_proj_kernel(x_ref, g1_ref, wg_ref, wu_ref, wd_ref, gmix_ref, win_ref, gqa_ref, wqbt_ref,
                      gkva_ref, wkb_ref, wvbt_ref, ropeq_ref, ropek_ref,
                      h1_ref, qm_ref, km_ref, vm_ref, qs_ref, ks_ref, vs_ref, *, q_scale):
    tm = x_ref.shape[0]
    x = x_ref[...]
    xn = _rms(x, g1_ref[...]).astype(BF16)
    h1 = x + 0.5 * _swiglu(xn, wg_ref, wu_ref, wd_ref)
    h1_ref[...] = h1

    u = _rms(h1, gmix_ref[...]).astype(BF16)
    proj = jnp.dot(u, win_ref[...], preferred_element_type=F32)

    cq = _rms(proj[:, P_CQ:P_CQ + Q_LORA], gqa_ref[...]).astype(BF16)
    qt = lax.dot_general(wqbt_ref[...], cq, _NT, preferred_element_type=F32)
    cs = ropeq_ref[0:ROPE_HALF, :]
    sn = ropeq_ref[ROPE_HALF:QK_ROPE, :]
    for h in range(MLA_HEADS):
        r = h * LANES
        x1 = qt[r + ROPE_LO:r + ROPE_LO + ROPE_HALF]
        x2 = qt[r + ROPE_LO + ROPE_HALF:r + ROPE_LO + QK_ROPE]
        qh = jnp.concatenate([qt[r:r + QK_NOPE] * q_scale, x1 * cs - x2 * sn, x2 * cs + x1 * sn,
                              qt[r + ROPE_LO + QK_ROPE:r + LANES]], axis=0).astype(BF16)
        for t in range(tm // MLA_TILE):
            qm_ref[t, r:r + LANES, :] = qh[:, t * MLA_TILE:(t + 1) * MLA_TILE]

    ckv = _rms(proj[:, P_CKV:P_CKV + KV_LORA], gkva_ref[...]).astype(BF16)
    kn = jnp.dot(ckv, wkb_ref[...], preferred_element_type=F32)
    kpe = _rope_lanes(proj[:, P_KPE:P_KPE + LANES], ropek_ref)
    for h in range(MLA_HEADS):
        km_ref[:, h * LANES:(h + 1) * LANES] = (kn[:, h * LANES:(h + 1) * LANES] + kpe).astype(BF16)
    vt = lax.dot_general(wvbt_ref[...], ckv, _NT, preferred_element_type=F32).astype(BF16)
    ones = jnp.ones((ONES_ROWS, tm), BF16)
    vt = jnp.concatenate([blk for h in range(MLA_HEADS) for blk in (vt[h * V_HEAD:(h + 1) * V_HEAD], ones)],
                         axis=0)
    for t in range(tm // MLA_TILE):
        vm_ref[t] = vt[:, t * MLA_TILE:(t + 1) * MLA_TILE]

    qst = (proj[:, P_QS:P_QS + SWA_OUT] * (SWA_HEAD_DIM ** -0.5 * LOG2E)).T.astype(BF16)
    vst = proj[:, P_VS:P_VS + SWA_KV].T.astype(BF16)
    vst = jnp.concatenate([blk for h in range(SWA_KV_HEADS)
                           for blk in (vst[h * SWA_HEAD_DIM:(h + 1) * SWA_HEAD_DIM], ones)], axis=0)
    for t in range(tm // WINDOW):
        qs_ref[t] = qst[:, t * WINDOW:(t + 1) * WINDOW]
        vs_ref[t] = vst[:, t * WINDOW:(t + 1) * WINDOW]
    ks_ref[...] = proj[:, P_KS:P_KS + SWA_KV].astype(BF16)


def _mla_kernel(q_ref, k_ref, v_ref, o_ref, s_ref, p_ref):
    nt, rows, t = q_ref.shape
    hp = rows // LANES
    vr = v_ref.shape[1] // hp
    key = lax.broadcasted_iota(jnp.int32, (t, t), 0)
    qry = lax.broadcasted_iota(jnp.int32, (t, t), 1)
    causal = key <= qry

    def issue(i, j):
        return tuple(jnp.dot(k_ref[j * t:(j + 1) * t, hh * LANES:(hh + 1) * LANES],
                             q_ref[i, hh * LANES:(hh + 1) * LANES, :], preferred_element_type=F32)
                     for hh in range(hp))

    def stash(sc, masked):
        tile_max = []
        for hh in range(hp):
            s = jnp.where(causal, sc[hh], NEG) if masked else sc[hh]
            s_ref[hh] = s
            tile_max.append(jnp.max(s, axis=0, keepdims=True))
        return tuple(tile_max)

    def fold_values(jp, state):
        new = []
        for hh in range(hp):
            m, acc, alpha = state[hh]
            pv = jnp.dot(v_ref[jp, hh * vr:(hh + 1) * vr, :], p_ref[hh], preferred_element_type=F32)
            new.append((m, alpha * acc + pv, alpha))
        return tuple(new)

    def softmax(tile_max, state):
        new = []
        for hh in range(hp):
            m, acc, _ = state[hh]
            m_new = jnp.maximum(m, tile_max[hh])
            p_ref[hh] = jnp.exp2(s_ref[hh] - m_new).astype(BF16)
            new.append((m_new, acc, jnp.exp2(m - m_new)))
        return tuple(new)

    def finish(i, state):
        o_ref[i] = jnp.concatenate([acc[0:V_HEAD] / acc[V_HEAD:V_HEAD + 1] for (_, acc, _) in state], axis=0)

    fresh = tuple((jnp.full((1, t), NEG, F32), jnp.zeros((vr, t), F32), jnp.zeros((1, t), F32))
                  for _ in range(hp))

    tiles = [(i, j) for i in range(nt) for j in range(i + 1)]
    tile_max = stash(issue(0, 0), True)
    state = fresh
    for n, (i, j) in enumerate(tiles):
        prev = tiles[n - 1] if n > 0 else None
        nxt = tiles[n + 1] if n + 1 < len(tiles) else None
        if nxt is not None:
            sc = issue(*nxt)
        if prev is not None:
            state = fold_values(prev[1], state)
            if prev[0] != i:
                finish(prev[0], state)
                state = fresh
        state = softmax(tile_max, state)
        if nxt is not None:
            tile_max = stash(sc, nxt[0] == nxt[1])
    finish(nt - 1, fold_values(nt - 1, state))


def _swa_kernel(bucket_ref, relb_ref, sink_ref, q_ref, k_ref, v_ref, o_ref, bias_ref, sinkrow_ref,
                s_ref, p_ref):
    w, dh, g = WINDOW, SWA_HEAD_DIM, SWA_GROUP
    nb = q_ref.shape[0]
    vr = v_ref.shape[1] // SWA_KV_HEADS

    @pl.when(pl.program_id(0) == 0)
    def _():
        bucket = bucket_ref[...]
        for kvh in range(SWA_KV_HEADS):
            for gg in range(g):
                h = kvh * g + gg
                b = jnp.zeros((2 * w, w), F32)
                for r in range(REL_BUCKETS):
                    b = jnp.where(bucket == r, relb_ref[r, h] * LOG2E, b)
                bias_ref[kvh, :, gg * w:(gg + 1) * w] = b
                sinkrow_ref[kvh, :, gg * w:(gg + 1) * w] = jnp.full((1, w), sink_ref[h] * LOG2E, F32)

    kj = lax.broadcasted_iota(jnp.int32, (2 * w, w), 0)
    qi = lax.broadcasted_iota(jnp.int32, (2 * w, w), 1)
    dist = qi + w - kj
    band = jnp.concatenate([(dist >= 0) & (dist < w)] * g, axis=1)
    zeros_q = jnp.zeros((dh, g * w), BF16)

    def key_rows(n):
        return (0, w) if n == 0 else ((n - 1) * w, 2 * w)

    def issue(n, kvh):
        qt = jnp.concatenate([q_ref[n, (kvh * g + gg) * dh:(kvh * g + gg + 1) * dh, :] for gg in range(g)],
                             axis=1)
        qt = jnp.concatenate([qt, zeros_q] if kvh == 0 else [zeros_q, qt], axis=0)
        start, size = key_rows(n)
        return jnp.dot(k_ref[start:start + size, :], qt, preferred_element_type=F32)

    def stash(sc, n, kvh):
        lo = 2 * w - sc.shape[0]
        s = jnp.where(band[lo:], sc + bias_ref[kvh, lo:, :], NEG)
        s_ref[lo:, :] = s
        return jnp.max(s, axis=0, keepdims=True)

    def softmax(tile_max, n, kvh):
        lo = 2 * w - key_rows(n)[1]
        sink = sinkrow_ref[kvh]
        m = jnp.maximum(tile_max, sink)
        p_ref[lo:, :] = jnp.exp2(s_ref[lo:, :] - m).astype(BF16)
        return jnp.exp2(sink - m)

    def fold_values(sink_term, n, kvh):
        vrows = slice(kvh * vr, (kvh + 1) * vr)
        if n == 0:
            vwin_t = v_ref[0, vrows, :]
        else:
            vwin_t = jnp.concatenate([v_ref[n - 1, vrows, :], v_ref[n, vrows, :]], axis=1)
        lo = 2 * w - key_rows(n)[1]
        pv = jnp.dot(vwin_t, p_ref[lo:, :], preferred_element_type=F32)
        o = pv[0:dh] / (pv[dh:dh + 1] + sink_term)
        for gg in range(g):
            h = kvh * g + gg
            o_ref[n, h * dh:(h + 1) * dh, :] = o[:, gg * w:(gg + 1) * w]

    units = [(n, kvh) for n in range(nb) for kvh in range(SWA_KV_HEADS)]
    tile_max = stash(issue(*units[0]), *units[0])
    sink_term = None
    for idx, unit in enumerate(units):
        nxt = units[idx + 1] if idx + 1 < len(units) else None
        if nxt is not None:
            sc = issue(*nxt)
        if idx > 0:
            fold_values(sink_term, *units[idx - 1])
        sink_term = softmax(tile_max, *unit)
        if nxt is not None:
            tile_max = stash(sc, *nxt)
    fold_values(sink_term, *units[-1])


def _attn_features(om_ref, os_ref, gom_ref, gos_ref):
    om = jnp.concatenate([om_ref[t].T for t in range(om_ref.shape[0])], axis=0)
    osw = jnp.concatenate([os_ref[t].T for t in range(os_ref.shape[0])], axis=0)
    return _rms(om, gom_ref[...]).astype(BF16), _rms(osw, gos_ref[...]).astype(BF16)


def _out_ffn2_kernel(h1_ref, om_next_ref, os_next_ref, om_first_ref, os_first_ref, gom_ref, gos_ref, wom_ref,
                     wos_ref, g2_ref, wg_ref, wu_ref, wd_ref, gfin_ref, out_ref, am_ref, as_ref, h3_ref):
    @pl.when(pl.program_id(0) == 0)
    def _():
        am_ref[...], as_ref[...] = _attn_features(om_first_ref, os_first_ref, gom_ref, gos_ref)
        h3_ref[...] = jnp.zeros_like(h3_ref)

    h2 = (h1_ref[...] + jnp.dot(am_ref[...], wom_ref[...], preferred_element_type=F32)
          + jnp.dot(as_ref[...], wos_ref[...], preferred_element_type=F32))
    out_ref[...] = _rms(h3_ref[...], gfin_ref[...])
    am_ref[...], as_ref[...] = _attn_features(om_next_ref, os_next_ref, gom_ref, gos_ref)
    hn = _rms(h2, g2_ref[...]).astype(BF16)
    h3_ref[...] = h2 + 0.5 * _swiglu(hn, wg_ref, wu_ref, wd_ref)


def _pack_w_in(w_in):
    d = w_in.shape[0]
    o_kpe = Q_LORA + KV_LORA
    o_qs = o_kpe + QK_ROPE
    kpe = jnp.zeros((d, LANES), w_in.dtype).at[:, ROPE_LO:ROPE_LO + QK_ROPE].set(w_in[:, o_kpe:o_qs])
    return jnp.concatenate([w_in[:, :o_kpe], kpe, w_in[:, o_qs:]], axis=1)


def _pack_w_q_b_t(w_q_b):
    r = w_q_b.shape[0]
    w = w_q_b.reshape(r, MLA_HEADS, QK_NOPE + QK_ROPE)
    w = jnp.pad(w, ((0, 0), (0, 0), (0, LANES - QK_NOPE - QK_ROPE)))
    return w.reshape(r, MLA_HEADS * LANES).T


def _pack_w_kv_b(w_kv_b):
    r = w_kv_b.shape[0]
    w = w_kv_b.reshape(r, MLA_HEADS, QK_NOPE + V_HEAD)
    wk = jnp.pad(w[:, :, :QK_NOPE], ((0, 0), (0, 0), (0, LANES - QK_NOPE))).reshape(r, MLA_HEADS * LANES)
    wv_t = w[:, :, QK_NOPE:].reshape(r, MLA_HEADS * V_HEAD).T
    return wk, wv_t


def _rope_angles(seq):
    pos = jnp.arange(seq, dtype=F32)
    inv_freq = ROPE_THETA ** (-jnp.arange(0, QK_ROPE, 2, dtype=F32) / QK_ROPE)
    ang = pos[:, None] * inv_freq[None, :]
    return jnp.cos(ang), jnp.sin(ang)


def _rope_table_lanes(seq):
    cos, sin = _rope_angles(seq)
    zeros = jnp.zeros((seq, LANES), F32)
    c = zeros.at[:, ROPE_LO:ROPE_LO + ROPE_HALF].set(cos).at[:, ROPE_LO + ROPE_HALF:ROPE_LO + QK_ROPE].set(cos)
    sa = zeros.at[:, ROPE_LO:ROPE_LO + ROPE_HALF].set(-sin)
    sb = zeros.at[:, ROPE_LO + ROPE_HALF:ROPE_LO + QK_ROPE].set(sin)
    return jnp.concatenate([c, sa, sb], axis=1)


def _t5_bucket(dist):
    n = jnp.maximum(dist, 0)
    max_exact = REL_BUCKETS // 2
    nf = jnp.maximum(n, 1).astype(F32)
    large = max_exact + (jnp.log(nf / max_exact) / math.log(REL_MAX_DIST / max_exact)
                         * (REL_BUCKETS - max_exact)).astype(jnp.int32)
    large = jnp.minimum(large, REL_BUCKETS - 1)
    return jnp.where(n < max_exact, n, large)


def _const_spec(shape):
    nd = len(shape)
    return pl.BlockSpec(shape, lambda *_: (0,) * nd, pipeline_mode=pl.Buffered(1))


def kernel(x, g_ffn1, w_ffn1_gate, w_ffn1_up, w_ffn1_down, g_mix, w_in, g_q_a, w_q_b, g_kv_a, w_kv_b,
           attn_sinks, rel_bias, g_out_mla, g_out_swa, w_o, g_ffn2, w_ffn2_gate, w_ffn2_up, w_ffn2_down,
           g_final):
    bsz, seq, d = x.shape
    n_tok = bsz * seq
    tm = TOKEN_TILE
    assert d == D_MODEL and seq % tm == 0 and tm % MLA_TILE == 0 and tm % WINDOW == 0
    assert g_ffn1.shape[0] == 1, "single layer"

    row = lambda a: a.reshape(1, -1).astype(F32)
    xf = x.reshape(n_tok, d)

    win = _pack_w_in(w_in[0]).astype(BF16)
    wqbt = _pack_w_q_b_t(w_q_b[0]).astype(BF16)
    wkb, wvbt = _pack_w_kv_b(w_kv_b[0])
    wkb, wvbt = wkb.astype(BF16), wvbt.astype(BF16)
    q_scale = (QK_NOPE + QK_ROPE) ** -0.5 * LOG2E
    cos, sin = _rope_angles(seq)
    rope_q = jnp.concatenate([cos.T, sin.T], axis=0) * q_scale
    rope_k = _rope_table_lanes(seq)

    tiles = n_tok // tm
    tps = seq // tm
    nt_mla, nb_swa = seq // MLA_TILE, seq // WINDOW
    tile_spec = lambda width: pl.BlockSpec((tm, width), lambda i: (i, 0))
    fm_spec = lambda rows, tok: pl.BlockSpec((None, tm // tok, rows, tok), lambda i: (i // tps, i % tps, 0, 0))

    weights1 = [row(g_ffn1[0]), w_ffn1_gate[0].astype(BF16), w_ffn1_up[0].astype(BF16),
                w_ffn1_down[0].astype(BF16), row(g_mix[0]), win, row(g_q_a[0]), wqbt, row(g_kv_a[0]), wkb, wvbt]
    h1, q_mla, k_mla, v_mla, q_swa, k_swa, v_swa = pl.pallas_call(
        functools.partial(_ffn1_proj_kernel, q_scale=q_scale),
        grid=(tiles,),
        in_specs=[tile_spec(d)] + [_const_spec(a.shape) for a in weights1]
                 + [pl.BlockSpec((QK_ROPE, tm), lambda i: (0, i % tps)),
                    pl.BlockSpec((tm, 3 * LANES), lambda i: (i % tps, 0))],
        out_specs=[tile_spec(d), fm_spec(MLA_HEADS * LANES, MLA_TILE), tile_spec(MLA_HEADS * LANES),
                   fm_spec(MLA_HEADS * MLA_V_ROWS, MLA_TILE), fm_spec(SWA_OUT, WINDOW), tile_spec(SWA_KV),
                   fm_spec(SWA_KV_HEADS * SWA_V_ROWS, WINDOW)],
        out_shape=[jax.ShapeDtypeStruct((n_tok, d), F32),
                   jax.ShapeDtypeStruct((bsz, nt_mla, MLA_HEADS * LANES, MLA_TILE), BF16),
                   jax.ShapeDtypeStruct((n_tok, MLA_HEADS * LANES), BF16),
                   jax.ShapeDtypeStruct((bsz, nt_mla, MLA_HEADS * MLA_V_ROWS, MLA_TILE), BF16),
                   jax.ShapeDtypeStruct((bsz, nb_swa, SWA_OUT, WINDOW), BF16),
                   jax.ShapeDtypeStruct((n_tok, SWA_KV), BF16),
                   jax.ShapeDtypeStruct((bsz, nb_swa, SWA_KV_HEADS * SWA_V_ROWS, WINDOW), BF16)],
        compiler_params=pltpu.CompilerParams(dimension_semantics=("arbitrary",),
                                             vmem_limit_bytes=VMEM_LIMIT),
        name="ffn1_proj",
    )(xf, *weights1, rope_q, rope_k)

    hp = MLA_HEADS_PER_STEP
    o_mla = pl.pallas_call(
        _mla_kernel,
        grid=(bsz, MLA_HEADS // hp),
        in_specs=[pl.BlockSpec((None, nt_mla, hp * LANES, MLA_TILE), lambda b, p: (b, 0, p, 0)),
                  pl.BlockSpec((seq, hp * LANES), lambda b, p: (b, p)),
                  pl.BlockSpec((None, nt_mla, hp * MLA_V_ROWS, MLA_TILE), lambda b, p: (b, 0, p, 0))],
        out_specs=pl.BlockSpec((None, nt_mla, hp * V_HEAD, MLA_TILE), lambda b, p: (b, 0, p, 0)),
        out_shape=jax.ShapeDtypeStruct((bsz, nt_mla, MLA_OUT, MLA_TILE), F32),
        scratch_shapes=[pltpu.VMEM((hp, MLA_TILE, MLA_TILE), F32), pltpu.VMEM((hp, MLA_TILE, MLA_TILE), BF16)],
        compiler_params=pltpu.CompilerParams(dimension_semantics=("arbitrary", "arbitrary"),
                                             vmem_limit_bytes=VMEM_LIMIT),
        name="mla",
    )(q_mla, k_mla, v_mla)

    kj = jnp.arange(2 * WINDOW)[:, None]
    qi = jnp.arange(WINDOW)[None, :]
    bucket_t = _t5_bucket(qi + WINDOW - kj).astype(jnp.int32)
    smem = pl.BlockSpec(memory_space=pltpu.SMEM)
    o_swa = pl.pallas_call(
        _swa_kernel,
        grid=(bsz,),
        in_specs=[_const_spec(bucket_t.shape), smem, smem,
                  pl.BlockSpec((None, nb_swa, SWA_OUT, WINDOW), lambda b: (b, 0, 0, 0)),
                  pl.BlockSpec((seq, SWA_KV), lambda b: (b, 0)),
                  pl.BlockSpec((None, nb_swa, SWA_KV_HEADS * SWA_V_ROWS, WINDOW), lambda b: (b, 0, 0, 0))],
        out_specs=pl.BlockSpec((None, nb_swa, SWA_OUT, WINDOW), lambda b: (b, 0, 0, 0)),
        out_shape=jax.ShapeDtypeStruct((bsz, nb_swa, SWA_OUT, WINDOW), F32),
        scratch_shapes=[pltpu.VMEM((SWA_KV_HEADS, 2 * WINDOW, SWA_GROUP * WINDOW), F32),
                        pltpu.VMEM((SWA_KV_HEADS, 1, SWA_GROUP * WINDOW), F32),
                        pltpu.VMEM((2 * WINDOW, SWA_GROUP * WINDOW), F32),
                        pltpu.VMEM((2 * WINDOW, SWA_GROUP * WINDOW), BF16)],
        compiler_params=pltpu.CompilerParams(dimension_semantics=("arbitrary",),
                                             vmem_limit_bytes=VMEM_LIMIT),
        name="swa",
    )(bucket_t, rel_bias.astype(F32), attn_sinks[0].astype(F32), q_swa, k_swa, v_swa)

    wo = w_o[0].astype(BF16)
    weights4 = [row(g_out_mla[0]), row(g_out_swa[0]), wo[:MLA_OUT], wo[MLA_OUT:], row(g_ffn2[0]),
                w_ffn2_gate[0].astype(BF16), w_ffn2_up[0].astype(BF16), w_ffn2_down[0].astype(BF16),
                row(g_final)]
    last = tiles - 1
    nxt = lambda s: jnp.minimum(s + 1, last)
    fm_next = lambda rows, tok: pl.BlockSpec(
        (None, tm // tok, rows, tok), lambda s: (nxt(s) // tps, nxt(s) % tps, 0, 0))
    fm_first = lambda rows, tok: pl.BlockSpec((None, tm // tok, rows, tok), lambda s: (0, 0, 0, 0),
                                              pipeline_mode=pl.Buffered(1))
    out = pl.pallas_call(
        _out_ffn2_kernel,
        grid=(tiles + 1,),
        in_specs=[pl.BlockSpec((tm, d), lambda s: (jnp.minimum(s, last), 0)),
                  fm_next(MLA_OUT, MLA_TILE), fm_next(SWA_OUT, WINDOW),
                  fm_first(MLA_OUT, MLA_TILE), fm_first(SWA_OUT, WINDOW)]
                 + [_const_spec(a.shape) for a in weights4],
        out_specs=pl.BlockSpec((tm, d), lambda s: (jnp.maximum(s - 1, 0), 0)),
        out_shape=jax.ShapeDtypeStruct((n_tok, d), F32),
        scratch_shapes=[pltpu.VMEM((tm, MLA_OUT), BF16), pltpu.VMEM((tm, SWA_OUT), BF16),
                        pltpu.VMEM((tm, d), F32)],
        compiler_params=pltpu.CompilerParams(dimension_semantics=("arbitrary",),
                                             vmem_limit_bytes=VMEM_LIMIT),
        name="out_ffn2",
    )(h1, o_mla, o_swa, o_mla, o_swa, *weights4)
    return out.reshape(bsz, seq, d)
```

```python
import functools
import math

import jax
import jax.numpy as jnp
from jax import lax
from jax.experimental import pallas as pl
from jax.experimental.pallas import tpu as pltpu

F32 = jnp.float32
BF16 = jnp.bfloat16

EPS = 1e-6
NEG = -1e30
LOG2E = math.log2(math.e)

D_MODEL = 1024
D_FF = 2816
MLA_HEADS = 8
Q_LORA = 256
KV_LORA = 128
QK_NOPE = 64
QK_ROPE = 32
V_HEAD = 64
ROPE_THETA = 10000.0
SWA_HEADS = 8
SWA_KV_HEADS = 2
SWA_HEAD_DIM = 64
SWA_GROUP = SWA_HEADS // SWA_KV_HEADS
WINDOW = 128
REL_BUCKETS = 32
REL_MAX_DIST = 128
MLA_OUT = MLA_HEADS * V_HEAD
SWA_OUT = SWA_HEADS * SWA_HEAD_DIM
SWA_KV = SWA_KV_HEADS * SWA_HEAD_DIM

LANES = 128
ROPE_LO = QK_NOPE
ROPE_HALF = QK_ROPE // 2

P_CQ = 0
P_CKV = P_CQ + Q_LORA
P_KPE = P_CKV + KV_LORA
P_QS = P_KPE + LANES
P_KS = P_QS + SWA_OUT
P_VS = P_KS + SWA_KV
P_END = P_VS + SWA_KV

TOKEN_TILE = 512
MLA_TILE = 256
MLA_HEADS_PER_STEP = 2
ONES_ROWS = 16
MLA_V_ROWS = V_HEAD + ONES_ROWS
SWA_V_ROWS = SWA_HEAD_DIM + ONES_ROWS
VMEM_LIMIT = 56 * 1024 * 1024

_NT = (((1,), (1,)), ((), ()))


def _rms(x, g):
    return x * lax.rsqrt(jnp.mean(x * x, axis=-1, keepdims=True) + EPS) * g


def _swiglu(xn, wg_ref, wu_ref, wd_ref):
    gate = jnp.dot(xn, wg_ref[...], preferred_element_type=F32)
    up = jnp.dot(xn, wu_ref[...], preferred_element_type=F32)
    act = (gate * jax.nn.sigmoid(gate) * up).astype(BF16)
    return jnp.dot(act, wd_ref[...], preferred_element_type=F32)


def _rope_lanes(x, tab_ref):
    c = tab_ref[:, 0:LANES]
    sa = tab_ref[:, LANES:2 * LANES]
    sb = tab_ref[:, 2 * LANES:3 * LANES]
    return (x * c + pltpu.roll(x, LANES - ROPE_HALF, 1) * sa + pltpu.roll(x, ROPE_HALF, 1) * sb)


def _ffn1_proj_kernel(x_ref, g1_ref, wg_ref, wu_ref, wd_ref, gmix_ref, win_ref, gqa_ref, wqbt_ref,
                      gkva_ref, wkb_ref, wvbt_ref, ropeq_ref, ropek_ref,
                      h1_ref, qm_ref, km_ref, vm_ref, qs_ref, ks_ref, vs_ref, *, q_scale):
    tm = x_ref.shape[0]
    x = x_ref[...]
    xn = _rms(x, g1_ref[...]).astype(BF16)
    h1 = x + 0.5 * _swiglu(xn, wg_ref, wu_ref, wd_ref)
    h1_ref[...] = h1

    u = _rms(h1, gmix_ref[...]).astype(BF16)
    proj = jnp.dot(u, win_ref[...], preferred_element_type=F32)

    cq = _rms(proj[:, P_CQ:P_CQ + Q_LORA], gqa_ref[...]).astype(BF16)
    qt = lax.dot_general(wqbt_ref[...], cq, _NT, preferred_element_type=F32)
    cs = ropeq_ref[0:ROPE_HALF, :]
    sn = ropeq_ref[ROPE_HALF:QK_ROPE, :]
    for h in range(MLA_HEADS):
        r = h * LANES
        x1 = qt[r + ROPE_LO:r + ROPE_LO + ROPE_HALF]
        x2 = qt[r + ROPE_LO + ROPE_HALF:r + ROPE_LO + QK_ROPE]
        qh = jnp.concatenate([qt[r:r + QK_NOPE] * q_scale, x1 * cs - x2 * sn, x2 * cs + x1 * sn,
                              qt[r + ROPE_LO + QK_ROPE:r + LANES]], axis=0).astype(BF16)
        for t in range(tm // MLA_TILE):
            qm_ref[t, r:r + LANES, :] = qh[:, t * MLA_TILE:(t + 1) * MLA_TILE]

    ckv = _rms(proj[:, P_CKV:P_CKV + KV_LORA], gkva_ref[...]).astype(BF16)
    kn = jnp.dot(ckv, wkb_ref[...], preferred_element_type=F32)
    kpe = _rope_lanes(proj[:, P_KPE:P_KPE + LANES], ropek_ref)
    for h in range(MLA_HEADS):
        km_ref[:, h * LANES:(h + 1) * LANES] = (kn[:, h * LANES:(h + 1) * LANES] + kpe).astype(BF16)
    vt = lax.dot_general(wvbt_ref[...], ckv, _NT, preferred_element_type=F32).astype(BF16)
    ones = jnp.ones((ONES_ROWS, tm), BF16)
    vt = jnp.concatenate([blk for h in range(MLA_HEADS) for blk in (vt[h * V_HEAD:(h + 1) * V_HEAD], ones)],
                         axis=0)
    for t in range(tm // MLA_TILE):
        vm_ref[t] = vt[:, t * MLA_TILE:(t + 1) * MLA_TILE]

    qst = (proj[:, P_QS:P_QS + SWA_OUT] * (SWA_HEAD_DIM ** -0.5 * LOG2E)).T.astype(BF16)
    vst = proj[:, P_VS:P_VS + SWA_KV].T.astype(BF16)
    vst = jnp.concatenate([blk for h in range(SWA_KV_HEADS)
                           for blk in (vst[h * SWA_HEAD_DIM:(h + 1) * SWA_HEAD_DIM], ones)], axis=0)
    for t in range(tm // WINDOW):
        qs_ref[t] = qst[:, t * WINDOW:(t + 1) * WINDOW]
        vs_ref[t] = vst[:, t * WINDOW:(t + 1) * WINDOW]
    for h in range(SWA_KV_HEADS):
        ks_ref[:, h * LANES:h * LANES + SWA_HEAD_DIM] = (
            proj[:, P_KS + h * SWA_HEAD_DIM:P_KS + (h + 1) * SWA_HEAD_DIM].astype(BF16))
        ks_ref[:, h * LANES + SWA_HEAD_DIM:(h + 1) * LANES] = jnp.zeros((tm, LANES - SWA_HEAD_DIM), BF16)


def _mla_kernel(q_ref, k_ref, v_ref, o_ref, s_ref, p_ref):
    nt, rows, t = q_ref.shape
    hp = rows // LANES
    vr = v_ref.shape[1] // hp
    key = lax.broadcasted_iota(jnp.int32, (t, t), 0)
    qry = lax.broadcasted_iota(jnp.int32, (t, t), 1)
    causal = key <= qry

    def issue(i, j, hh):
        return jnp.dot(k_ref[j * t:(j + 1) * t, hh * LANES:(hh + 1) * LANES],
                       q_ref[i, hh * LANES:(hh + 1) * LANES, :], preferred_element_type=F32)

    def stash(s, masked, hh):
        if masked:
            s = jnp.where(causal, s, NEG)
        s_ref[hh] = s
        return jnp.max(s, axis=0, keepdims=True)

    def fold_values(jp, st, hh):
        m, acc, alpha = st
        pv = jnp.dot(v_ref[jp, hh * vr:(hh + 1) * vr, :], p_ref[hh], preferred_element_type=F32)
        return m, alpha * acc + pv, alpha

    def softmax(tile_max, st, hh):
        m, acc, _ = st
        m_new = jnp.maximum(m, tile_max)
        p_ref[hh] = jnp.exp2(s_ref[hh] - m_new).astype(BF16)
        return m_new, acc, jnp.exp2(m - m_new)

    def finish(i, st, hh):
        acc = st[1]
        o_ref[i, hh * V_HEAD:(hh + 1) * V_HEAD, :] = acc[0:V_HEAD] / acc[V_HEAD:V_HEAD + 1]

    fresh = (jnp.full((1, t), NEG, F32), jnp.zeros((vr, t), F32), jnp.zeros((1, t), F32))

    tiles = [(i, j) for i in range(nt) for j in range(i + 1)]
    tile_max = [stash(issue(0, 0, hh), True, hh) for hh in range(hp)]
    state = [fresh] * hp
    for n, (i, j) in enumerate(tiles):
        prev = tiles[n - 1] if n > 0 else None
        nxt = tiles[n + 1] if n + 1 < len(tiles) else None
        if nxt is not None:
            sc = [issue(*nxt, hh) for hh in range(hp)]
        for hh in range(hp):
            st = state[hh]
            if prev is not None:
                st = fold_values(prev[1], st, hh)
                if prev[0] != i:
                    finish(prev[0], st, hh)
                    st = fresh
            state[hh] = softmax(tile_max[hh], st, hh)
            if nxt is not None:
                tile_max[hh] = stash(sc[hh], nxt[0] == nxt[1], hh)
    for hh in range(hp):
        finish(nt - 1, fold_values(nt - 1, state[hh], hh), hh)


def _swa_kernel(bucket_ref, relb_ref, sink_ref, q_ref, k_ref, v_ref, o_ref, bias_ref, sinkrow_ref,
                s_ref, p_ref):
    w, dh, g = WINDOW, SWA_HEAD_DIM, SWA_GROUP
    nb = q_ref.shape[0]
    vr = v_ref.shape[1] // SWA_KV_HEADS

    @pl.when(pl.program_id(0) == 0)
    def _():
        bucket = bucket_ref[...]
        for kvh in range(SWA_KV_HEADS):
            for gg in range(g):
                h = kvh * g + gg
                b = jnp.zeros((2 * w, w), F32)
                for r in range(REL_BUCKETS):
                    b = jnp.where(bucket == r, relb_ref[r, h] * LOG2E, b)
                bias_ref[kvh, :, gg * w:(gg + 1) * w] = b
                sinkrow_ref[kvh, :, gg * w:(gg + 1) * w] = jnp.full((1, w), sink_ref[h] * LOG2E, F32)

    kj = lax.broadcasted_iota(jnp.int32, (2 * w, w), 0)
    qi = lax.broadcasted_iota(jnp.int32, (2 * w, w), 1)
    dist = qi + w - kj
    band = jnp.concatenate([(dist >= 0) & (dist < w)] * g, axis=1)

    def key_rows(n):
        return (0, w) if n == 0 else ((n - 1) * w, 2 * w)

    def issue(n, kvh):
        qt = jnp.concatenate([q_ref[n, (kvh * g + gg) * dh:(kvh * g + gg + 1) * dh, :] for gg in range(g)],
                             axis=1)
        start, size = key_rows(n)
        k = k_ref[start:start + size, kvh * LANES:kvh * LANES + dh]
        return jnp.dot(k, qt, preferred_element_type=F32)

    def stash(sc, n, kvh):
        lo = 2 * w - sc.shape[0]
        s = jnp.where(band[lo:], sc + bias_ref[kvh, lo:, :], NEG)
        s_ref[lo:, :] = s
        return jnp.max(s, axis=0, keepdims=True)

    def softmax(tile_max, n, kvh):
        lo = 2 * w - key_rows(n)[1]
        sink = sinkrow_ref[kvh]
        m = jnp.maximum(tile_max, sink)
        p_ref[lo:, :] = jnp.exp2(s_ref[lo:, :] - m).astype(BF16)
        return jnp.exp2(sink - m)

    def fold_values(sink_term, n, kvh):
        vrows = slice(kvh * vr, (kvh + 1) * vr)
        if n == 0:
            vwin_t = v_ref[0, vrows, :]
        else:
            vwin_t = jnp.concatenate([v_ref[n - 1, vrows, :], v_ref[n, vrows, :]], axis=1)
        lo = 2 * w - key_rows(n)[1]
        pv = jnp.dot(vwin_t, p_ref[lo:, :], preferred_element_type=F32)
        o = pv[0:dh] / (pv[dh:dh + 1] + sink_term)
        for gg in range(g):
            h = kvh * g + gg
            o_ref[n, h * dh:(h + 1) * dh, :] = o[:, gg * w:(gg + 1) * w]

    units = [(n, kvh) for n in range(nb) for kvh in range(SWA_KV_HEADS)]
    tile_max = stash(issue(*units[0]), *units[0])
    sink_term = None
    for idx, unit in enumerate(units):
        nxt = units[idx + 1] if idx + 1 < len(units) else None
        if nxt is not None:
            sc = issue(*nxt)
        if idx > 0:
            fold_values(sink_term, *units[idx - 1])
        sink_term = softmax(tile_max, *unit)
        if nxt is not None:
            tile_max = stash(sc, *nxt)
    fold_values(sink_term, *units[-1])


def _attn_features(om_ref, os_ref, gom_ref, gos_ref):
    om = jnp.concatenate([om_ref[t].T for t in range(om_ref.shape[0])], axis=0)
    osw = jnp.concatenate([os_ref[t].T for t in range(os_ref.shape[0])], axis=0)
    return _rms(om, gom_ref[...]).astype(BF16), _rms(osw, gos_ref[...]).astype(BF16)


def _out_ffn2_kernel(h1_ref, om_next_ref, os_next_ref, om_first_ref, os_first_ref, gom_ref, gos_ref, wom_ref,
                     wos_ref, g2_ref, wg_ref, wu_ref, wd_ref, gfin_ref, out_ref, am_ref, as_ref, h3_ref):
    @pl.when(pl.program_id(0) == 0)
    def _():
        am_ref[...], as_ref[...] = _attn_features(om_first_ref, os_first_ref, gom_ref, gos_ref)
        h3_ref[...] = jnp.zeros_like(h3_ref)

    h2 = (h1_ref[...] + jnp.dot(am_ref[...], wom_ref[...], preferred_element_type=F32)
          + jnp.dot(as_ref[...], wos_ref[...], preferred_element_type=F32))
    out_ref[...] = _rms(h3_ref[...], gfin_ref[...])
    am_ref[...], as_ref[...] = _attn_features(om_next_ref, os_next_ref, gom_ref, gos_ref)
    hn = _rms(h2, g2_ref[...]).astype(BF16)
    h3_ref[...] = h2 + 0.5 * _swiglu(hn, wg_ref, wu_ref, wd_ref)


def _pack_w_in(w_in):
    o_kpe = Q_LORA + KV_LORA
    o_qs = o_kpe + QK_ROPE
    kpe = jnp.pad(w_in[:, o_kpe:o_qs], ((0, 0), (ROPE_LO, LANES - ROPE_LO - QK_ROPE)))
    return jnp.concatenate([w_in[:, :o_kpe], kpe, w_in[:, o_qs:]], axis=1)


def _pack_w_q_b_t(w_q_b):
    r = w_q_b.shape[0]
    w = w_q_b.reshape(r, MLA_HEADS, QK_NOPE + QK_ROPE)
    w = jnp.pad(w, ((0, 0), (0, 0), (0, LANES - QK_NOPE - QK_ROPE)))
    return w.reshape(r, MLA_HEADS * LANES).T


def _pack_w_kv_b(w_kv_b):
    r = w_kv_b.shape[0]
    w = w_kv_b.reshape(r, MLA_HEADS, QK_NOPE + V_HEAD)
    wk = jnp.pad(w[:, :, :QK_NOPE], ((0, 0), (0, 0), (0, LANES - QK_NOPE))).reshape(r, MLA_HEADS * LANES)
    wv_t = w[:, :, QK_NOPE:].reshape(r, MLA_HEADS * V_HEAD).T
    return wk, wv_t


def _rope_angles(seq):
    pos = jnp.arange(seq, dtype=F32)
    inv_freq = ROPE_THETA ** (-jnp.arange(0, QK_ROPE, 2, dtype=F32) / QK_ROPE)
    ang = pos[:, None] * inv_freq[None, :]
    return jnp.cos(ang), jnp.sin(ang)


def _rope_table_lanes(seq):
    cos, sin = _rope_angles(seq)
    z = lambda n: jnp.zeros((seq, n), F32)
    hi = LANES - ROPE_LO - QK_ROPE
    c = [z(ROPE_LO), cos, cos, z(hi)]
    sa = [z(ROPE_LO), -sin, z(ROPE_HALF + hi)]
    sb = [z(ROPE_LO + ROPE_HALF), sin, z(hi)]
    return jnp.concatenate(c + sa + sb, axis=1)


def _t5_bucket(dist):
    n = jnp.maximum(dist, 0)
    max_exact = REL_BUCKETS // 2
    nf = jnp.maximum(n, 1).astype(F32)
    large = max_exact + (jnp.log(nf / max_exact) / math.log(REL_MAX_DIST / max_exact)
                         * (REL_BUCKETS - max_exact)).astype(jnp.int32)
    large = jnp.minimum(large, REL_BUCKETS - 1)
    return jnp.where(n < max_exact, n, large)


def _const_spec(shape):
    nd = len(shape)
    return pl.BlockSpec(shape, lambda *_: (0,) * nd, pipeline_mode=pl.Buffered(1))


def kernel(x, g_ffn1, w_ffn1_gate, w_ffn1_up, w_ffn1_down, g_mix, w_in, g_q_a, w_q_b, g_kv_a, w_kv_b,
           attn_sinks, rel_bias, g_out_mla, g_out_swa, w_o, g_ffn2, w_ffn2_gate, w_ffn2_up, w_ffn2_down,
           g_final):
    bsz, seq, d = x.shape
    n_tok = bsz * seq
    tm = TOKEN_TILE
    assert d == D_MODEL and seq % tm == 0 and tm % MLA_TILE == 0 and tm % WINDOW == 0
    assert g_ffn1.shape[0] == 1, "single layer"

    row = lambda a: a.reshape(1, -1).astype(F32)
    xf = x.reshape(n_tok, d)

    win = _pack_w_in(w_in[0]).astype(BF16)
    wqbt = _pack_w_q_b_t(w_q_b[0]).astype(BF16)
    wkb, wvbt = _pack_w_kv_b(w_kv_b[0])
    wkb, wvbt = wkb.astype(BF16), wvbt.astype(BF16)
    q_scale = (QK_NOPE + QK_ROPE) ** -0.5 * LOG2E
    cos, sin = _rope_angles(seq)
    rope_q = jnp.concatenate([cos.T, sin.T], axis=0) * q_scale
    rope_k = _rope_table_lanes(seq)

    tiles = n_tok // tm
    tps = seq // tm
    nt_mla, nb_swa = seq // MLA_TILE, seq // WINDOW

    weights1 = [row(g_ffn1[0]), w_ffn1_gate[0].astype(BF16), w_ffn1_up[0].astype(BF16),
                w_ffn1_down[0].astype(BF16), row(g_mix[0]), win, row(g_q_a[0]), wqbt, row(g_kv_a[0]), wkb, wvbt]
    last = tiles - 1
    tile_spec = lambda width: pl.BlockSpec((tm, width), lambda i: (i, 0))
    fm_spec = lambda rows, tok: pl.BlockSpec((None, tm // tok, rows, tok), lambda i: (i // tps, i % tps, 0, 0))
    h1, q_mla, k_mla, v_mla, q_swa, k_swa, v_swa = pl.pallas_call(
        functools.partial(_ffn1_proj_kernel, q_scale=q_scale),
        grid=(tiles,),
        in_specs=[tile_spec(d)] + [_const_spec(a.shape) for a in weights1]
                 + [pl.BlockSpec((QK_ROPE, tm), lambda i: (0, i % tps)),
                    pl.BlockSpec((tm, 3 * LANES), lambda i: (i % tps, 0))],
        out_specs=[tile_spec(d), fm_spec(MLA_HEADS * LANES, MLA_TILE), tile_spec(MLA_HEADS * LANES),
                   fm_spec(MLA_HEADS * MLA_V_ROWS, MLA_TILE), fm_spec(SWA_OUT, WINDOW),
                   tile_spec(SWA_KV_HEADS * LANES),
                   fm_spec(SWA_KV_HEADS * SWA_V_ROWS, WINDOW)],
        out_shape=[jax.ShapeDtypeStruct((n_tok, d), F32),
                   jax.ShapeDtypeStruct((bsz, nt_mla, MLA_HEADS * LANES, MLA_TILE), BF16),
                   jax.ShapeDtypeStruct((n_tok, MLA_HEADS * LANES), BF16),
                   jax.ShapeDtypeStruct((bsz, nt_mla, MLA_HEADS * MLA_V_ROWS, MLA_TILE), BF16),
                   jax.ShapeDtypeStruct((bsz, nb_swa, SWA_OUT, WINDOW), BF16),
                   jax.ShapeDtypeStruct((n_tok, SWA_KV_HEADS * LANES), BF16),
                   jax.ShapeDtypeStruct((bsz, nb_swa, SWA_KV_HEADS * SWA_V_ROWS, WINDOW), BF16)],
        compiler_params=pltpu.CompilerParams(dimension_semantics=("arbitrary",),
                                             vmem_limit_bytes=VMEM_LIMIT),
        name="ffn1_proj",
    )(xf, *weights1, rope_q, rope_k)

    hp = MLA_HEADS_PER_STEP
    o_mla = pl.pallas_call(
        _mla_kernel,
        grid=(bsz, MLA_HEADS // hp),
        in_specs=[pl.BlockSpec((None, nt_mla, hp * LANES, MLA_TILE), lambda b, p: (b, 0, p, 0)),
                  pl.BlockSpec((seq, hp * LANES), lambda b, p: (b, p)),
                  pl.BlockSpec((None, nt_mla, hp * MLA_V_ROWS, MLA_TILE), lambda b, p: (b, 0, p, 0))],
        out_specs=pl.BlockSpec((None, nt_mla, hp * V_HEAD, MLA_TILE), lambda b, p: (b, 0, p, 0)),
        out_shape=jax.ShapeDtypeStruct((bsz, nt_mla, MLA_OUT, MLA_TILE), F32),
        scratch_shapes=[pltpu.VMEM((hp, MLA_TILE, MLA_TILE), F32), pltpu.VMEM((hp, MLA_TILE, MLA_TILE), BF16)],
        compiler_params=pltpu.CompilerParams(dimension_semantics=("arbitrary", "arbitrary"),
                                             vmem_limit_bytes=VMEM_LIMIT),
        name="mla",
    )(q_mla, k_mla, v_mla)

    kj = jnp.arange(2 * WINDOW)[:, None]
    qi = jnp.arange(WINDOW)[None, :]
    bucket_t = _t5_bucket(qi + WINDOW - kj).astype(jnp.int32)
    smem = pl.BlockSpec(memory_space=pltpu.SMEM)
    o_swa = pl.pallas_call(
        _swa_kernel,
        grid=(bsz,),
        in_specs=[_const_spec(bucket_t.shape), smem, smem,
                  pl.BlockSpec((None, nb_swa, SWA_OUT, WINDOW), lambda b: (b, 0, 0, 0)),
                  pl.BlockSpec((seq, SWA_KV_HEADS * LANES), lambda b: (b, 0)),
                  pl.BlockSpec((None, nb_swa, SWA_KV_HEADS * SWA_V_ROWS, WINDOW), lambda b: (b, 0, 0, 0))],
        out_specs=pl.BlockSpec((None, nb_swa, SWA_OUT, WINDOW), lambda b: (b, 0, 0, 0)),
        out_shape=jax.ShapeDtypeStruct((bsz, nb_swa, SWA_OUT, WINDOW), F32),
        scratch_shapes=[pltpu.VMEM((SWA_KV_HEADS, 2 * WINDOW, SWA_GROUP * WINDOW), F32),
                        pltpu.VMEM((SWA_KV_HEADS, 1, SWA_GROUP * WINDOW), F32),
                        pltpu.VMEM((2 * WINDOW, SWA_GROUP * WINDOW), F32),
                        pltpu.VMEM((2 * WINDOW, SWA_GROUP * WINDOW), BF16)],
        compiler_params=pltpu.CompilerParams(dimension_semantics=("arbitrary",),
                                             vmem_limit_bytes=VMEM_LIMIT),
        name="swa",
    )(bucket_t, rel_bias.astype(F32), attn_sinks[0].astype(F32), q_swa, k_swa, v_swa)

    wo = w_o[0].astype(BF16)
    weights4 = [row(g_out_mla[0]), row(g_out_swa[0]), wo[:MLA_OUT], wo[MLA_OUT:], row(g_ffn2[0]),
                w_ffn2_gate[0].astype(BF16), w_ffn2_up[0].astype(BF16), w_ffn2_down[0].astype(BF16),
                row(g_final)]
    nxt = lambda s: jnp.minimum(s + 1, last)
    fm_next = lambda rows, tok: pl.BlockSpec(
        (None, tm // tok, rows, tok), lambda s: (nxt(s) // tps, nxt(s) % tps, 0, 0))
    fm_first = lambda rows, tok: pl.BlockSpec((None, tm // tok, rows, tok), lambda s: (0, 0, 0, 0),
                                              pipeline_mode=pl.Buffered(1))
    out = pl.pallas_call(
        _out_ffn2_kernel,
        grid=(tiles + 1,),
        in_specs=[pl.BlockSpec((tm, d), lambda s: (jnp.minimum(s, last), 0)),
                  fm_next(MLA_OUT, MLA_TILE), fm_next(SWA_OUT, WINDOW),
                  fm_first(MLA_OUT, MLA_TILE), fm_first(SWA_OUT, WINDOW)]
                 + [_const_spec(a.shape) for a in weights4],
        out_specs=pl.BlockSpec((tm, d), lambda s: (jnp.maximum(s - 1, 0), 0)),
        out_shape=jax.ShapeDtypeStruct((n_tok, d), F32),
        scratch_shapes=[pltpu.VMEM((tm, MLA_OUT), BF16), pltpu.VMEM((tm, SWA_OUT), BF16),
                        pltpu.VMEM((tm, d), F32)],
        compiler_params=pltpu.CompilerParams(dimension_semantics=("arbitrary",),
                                             vmem_limit_bytes=VMEM_LIMIT),
        name="out_ffn2",
    )(h1, o_mla, o_swa, o_mla, o_swa, *weights4)
    return out.reshape(bsz, seq, d)
```

```python
import functools
import math

import jax
import jax.numpy as jnp
from jax import lax
from jax.experimental import pallas as pl
from jax.experimental.pallas import tpu as pltpu

F32 = jnp.float32
BF16 = jnp.bfloat16

EPS = 1e-6
NEG = -1e30
LOG2E = math.log2(math.e)

D_MODEL = 1024
D_FF = 2816
MLA_HEADS = 8
Q_LORA = 256
KV_LORA = 128
QK_NOPE = 64
QK_ROPE = 32
V_HEAD = 64
ROPE_THETA = 10000.0
SWA_HEADS = 8
SWA_KV_HEADS = 2
SWA_HEAD_DIM = 64
SWA_GROUP = SWA_HEADS // SWA_KV_HEADS
WINDOW = 128
REL_BUCKETS = 32
REL_MAX_DIST = 128
MLA_OUT = MLA_HEADS * V_HEAD
SWA_OUT = SWA_HEADS * SWA_HEAD_DIM
SWA_KV = SWA_KV_HEADS * SWA_HEAD_DIM

LANES = 128
ROPE_LO = QK_NOPE
ROPE_HALF = QK_ROPE // 2

P_CQ = 0
P_CKV = P_CQ + Q_LORA
P_KPE = P_CKV + KV_LORA
P_QS = P_KPE + LANES
P_KS = P_QS + SWA_OUT
P_VS = P_KS + SWA_KV
P_END = P_VS + SWA_KV

TOKEN_TILE = 512
MLA_TILE = 256
MLA_HEADS_PER_STEP = 2
SWA_HEADS_PER_UNIT = 4
SWA_UNITS_PER_STEP = 2
ONES_ROWS = 16
MLA_V_ROWS = V_HEAD + ONES_ROWS
SWA_V_ROWS = SWA_HEAD_DIM + ONES_ROWS
VMEM_LIMIT = 56 * 1024 * 1024

_NT = (((1,), (1,)), ((), ()))


def _rms(x, g):
    return x * lax.rsqrt(jnp.mean(x * x, axis=-1, keepdims=True) + EPS) * g


def _swiglu(xn, wg_ref, wu_ref, wd_ref):
    gate = jnp.dot(xn, wg_ref[...], preferred_element_type=F32)
    up = jnp.dot(xn, wu_ref[...], preferred_element_type=F32)
    act = (gate * jax.nn.sigmoid(gate) * up).astype(BF16)
    return jnp.dot(act, wd_ref[...], preferred_element_type=F32)


def _rope_lanes(x, tab_ref):
    c = tab_ref[:, 0:LANES]
    sa = tab_ref[:, LANES:2 * LANES]
    sb = tab_ref[:, 2 * LANES:3 * LANES]
    return (x * c + pltpu.roll(x, LANES - ROPE_HALF, 1) * sa + pltpu.roll(x, ROPE_HALF, 1) * sb)


def _ffn1_proj_kernel(x_ref, g1_ref, wg_ref, wu_ref, wd_ref, gmix_ref, win_ref, gqa_ref, wqbt_ref,
                      gkva_ref, wkb_ref, wvbt_ref, ropeq_ref, ropek_ref,
                      h1_ref, qm_ref, km_ref, vm_ref, qs_ref, ks_ref, vs_ref, *, q_scale):
    tm = x_ref.shape[0]
    x = x_ref[...]
    xn = _rms(x, g1_ref[...]).astype(BF16)
    h1 = x + 0.5 * _swiglu(xn, wg_ref, wu_ref, wd_ref)
    h1_ref[...] = h1

    u = _rms(h1, gmix_ref[...]).astype(BF16)
    proj = jnp.dot(u, win_ref[...], preferred_element_type=F32)

    cq = _rms(proj[:, P_CQ:P_CQ + Q_LORA], gqa_ref[...]).astype(BF16)
    qt = lax.dot_general(wqbt_ref[...], cq, _NT, preferred_element_type=F32)
    cs = ropeq_ref[0:ROPE_HALF, :]
    sn = ropeq_ref[ROPE_HALF:QK_ROPE, :]
    for h in range(MLA_HEADS):
        r = h * LANES
        x1 = qt[r + ROPE_LO:r + ROPE_LO + ROPE_HALF]
        x2 = qt[r + ROPE_LO + ROPE_HALF:r + ROPE_LO + QK_ROPE]
        qh = jnp.concatenate([qt[r:r + QK_NOPE] * q_scale, x1 * cs - x2 * sn, x2 * cs + x1 * sn,
                              qt[r + ROPE_LO + QK_ROPE:r + LANES]], axis=0).astype(BF16)
        for t in range(tm // MLA_TILE):
            qm_ref[t, r:r + LANES, :] = qh[:, t * MLA_TILE:(t + 1) * MLA_TILE]

    ckv = _rms(proj[:, P_CKV:P_CKV + KV_LORA], gkva_ref[...]).astype(BF16)
    kn = jnp.dot(ckv, wkb_ref[...], preferred_element_type=F32)
    kpe = _rope_lanes(proj[:, P_KPE:P_KPE + LANES], ropek_ref)
    for h in range(MLA_HEADS):
        km_ref[:, h * LANES:(h + 1) * LANES] = (kn[:, h * LANES:(h + 1) * LANES] + kpe).astype(BF16)
    vt = lax.dot_general(wvbt_ref[...], ckv, _NT, preferred_element_type=F32).astype(BF16)
    ones = jnp.ones((ONES_ROWS, tm), BF16)
    vt = jnp.concatenate([blk for h in range(MLA_HEADS) for blk in (vt[h * V_HEAD:(h + 1) * V_HEAD], ones)],
                         axis=0)
    for t in range(tm // MLA_TILE):
        vm_ref[t] = vt[:, t * MLA_TILE:(t + 1) * MLA_TILE]

    qst = (proj[:, P_QS:P_QS + SWA_OUT] * (SWA_HEAD_DIM ** -0.5 * LOG2E)).T.astype(BF16)
    vst = proj[:, P_VS:P_VS + SWA_KV].T.astype(BF16)
    vst = jnp.concatenate([blk for h in range(SWA_KV_HEADS)
                           for blk in (vst[h * SWA_HEAD_DIM:(h + 1) * SWA_HEAD_DIM], ones)], axis=0)
    for t in range(tm // WINDOW):
        qs_ref[t] = qst[:, t * WINDOW:(t + 1) * WINDOW]
        vs_ref[t] = vst[:, t * WINDOW:(t + 1) * WINDOW]
    for h in range(SWA_KV_HEADS):
        ks_ref[:, h * LANES:h * LANES + SWA_HEAD_DIM] = (
            proj[:, P_KS + h * SWA_HEAD_DIM:P_KS + (h + 1) * SWA_HEAD_DIM].astype(BF16))
        ks_ref[:, h * LANES + SWA_HEAD_DIM:(h + 1) * LANES] = jnp.zeros((tm, LANES - SWA_HEAD_DIM), BF16)


def _mla_kernel(q_ref, k_ref, v_ref, o_ref, s_ref, p_ref):
    nt, rows, t = q_ref.shape
    hp = rows // LANES
    vr = v_ref.shape[1] // hp
    key = lax.broadcasted_iota(jnp.int32, (t, t), 0)
    qry = lax.broadcasted_iota(jnp.int32, (t, t), 1)
    causal = key <= qry

    def issue(i, j, hh):
        return jnp.dot(k_ref[j * t:(j + 1) * t, hh * LANES:(hh + 1) * LANES],
                       q_ref[i, hh * LANES:(hh + 1) * LANES, :], preferred_element_type=F32)

    def stash(s, masked, hh):
        if masked:
            s = jnp.where(causal, s, NEG)
        s_ref[hh] = s
        return jnp.max(s, axis=0, keepdims=True)

    def fold_values(jp, st, hh):
        m, acc, alpha = st
        pv = jnp.dot(v_ref[jp, hh * vr:(hh + 1) * vr, :], p_ref[hh], preferred_element_type=F32)
        return m, alpha * acc + pv, alpha

    def softmax(tile_max, st, hh):
        m, acc, _ = st
        m_new = jnp.maximum(m, tile_max)
        p_ref[hh] = jnp.exp2(s_ref[hh] - m_new).astype(BF16)
        return m_new, acc, jnp.exp2(m - m_new)

    def finish(i, st, hh):
        acc = st[1]
        o_ref[i, hh * V_HEAD:(hh + 1) * V_HEAD, :] = acc[0:V_HEAD] / acc[V_HEAD:V_HEAD + 1]

    fresh = (jnp.full((1, t), NEG, F32), jnp.zeros((vr, t), F32), jnp.zeros((1, t), F32))

    tiles = [(i, j) for i in range(nt) for j in range(i + 1)]
    tile_max = [stash(issue(0, 0, hh), True, hh) for hh in range(hp)]
    state = [fresh] * hp
    for n, (i, j) in enumerate(tiles):
        prev = tiles[n - 1] if n > 0 else None
        nxt = tiles[n + 1] if n + 1 < len(tiles) else None
        if nxt is not None:
            sc = [issue(*nxt, hh) for hh in range(hp)]
        for hh in range(hp):
            st = state[hh]
            if prev is not None:
                st = fold_values(prev[1], st, hh)
                if prev[0] != i:
                    finish(prev[0], st, hh)
                    st = fresh
            state[hh] = softmax(tile_max[hh], st, hh)
            if nxt is not None:
                tile_max[hh] = stash(sc[hh], nxt[0] == nxt[1], hh)
    for hh in range(hp):
        finish(nt - 1, fold_values(nt - 1, state[hh], hh), hh)


def _swa_kernel(bucket_ref, relb_ref, sink_ref, q_ref, k_ref, v_ref, o_ref, bias_ref, sinkrow_ref,
                s_ref, p_ref):
    w, dh, g = WINDOW, SWA_HEAD_DIM, SWA_GROUP
    nb = q_ref.shape[0]
    vr = v_ref.shape[1] // SWA_KV_HEADS

    @pl.when(pl.program_id(0) == 0)
    def _():
        bucket = bucket_ref[...]
        for kvh in range(SWA_KV_HEADS):
            for gg in range(g):
                h = kvh * g + gg
                b = jnp.zeros((2 * w, w), F32)
                for r in range(REL_BUCKETS):
                    b = jnp.where(bucket == r, relb_ref[r, h] * LOG2E, b)
                bias_ref[kvh, :, gg * w:(gg + 1) * w] = b
                sinkrow_ref[kvh, :, gg * w:(gg + 1) * w] = jnp.full((1, w), sink_ref[h] * LOG2E, F32)

    kj = lax.broadcasted_iota(jnp.int32, (2 * w, w), 0)
    qi = lax.broadcasted_iota(jnp.int32, (2 * w, w), 1)
    dist = qi + w - kj
    chains = s_ref.shape[0]
    hu = s_ref.shape[2] // w
    band = jnp.concatenate([(dist >= 0) & (dist < w)] * hu, axis=1)

    def key_rows(n):
        return (0, w) if n == 0 else ((n - 1) * w, 2 * w)

    def heads(kvh, part):
        return [kvh * g + part * hu + i for i in range(hu)]

    def cols(part):
        return slice(part * hu * w, (part + 1) * hu * w)

    def issue(n, kvh, part):
        qt = jnp.concatenate([q_ref[n, h * dh:(h + 1) * dh, :] for h in heads(kvh, part)], axis=1)
        start, size = key_rows(n)
        k = k_ref[start:start + size, kvh * LANES:kvh * LANES + dh]
        return jnp.dot(k, qt, preferred_element_type=F32)

    def stash(sc, c, n, kvh, part):
        lo = 2 * w - sc.shape[0]
        s = jnp.where(band[lo:], sc + bias_ref[kvh, lo:, cols(part)], NEG)
        s_ref[c, lo:, :] = s
        return jnp.max(s, axis=0, keepdims=True)

    def softmax(tile_max, c, n, kvh, part):
        lo = 2 * w - key_rows(n)[1]
        sink = sinkrow_ref[kvh, :, cols(part)]
        m = jnp.maximum(tile_max, sink)
        p_ref[c, lo:, :] = jnp.exp2(s_ref[c, lo:, :] - m).astype(BF16)
        return jnp.exp2(sink - m)

    def fold_values(sink_term, c, n, kvh, part):
        vrows = slice(kvh * vr, (kvh + 1) * vr)
        if n == 0:
            vwin_t = v_ref[0, vrows, :]
        else:
            vwin_t = jnp.concatenate([v_ref[n - 1, vrows, :], v_ref[n, vrows, :]], axis=1)
        lo = 2 * w - key_rows(n)[1]
        pv = jnp.dot(vwin_t, p_ref[c, lo:, :], preferred_element_type=F32)
        o = pv[0:dh] / (pv[dh:dh + 1] + sink_term)
        for i, h in enumerate(heads(kvh, part)):
            o_ref[n, h * dh:(h + 1) * dh, :] = o[:, i * w:(i + 1) * w]

    units = [(n, kvh, part) for n in range(nb) for kvh in range(SWA_KV_HEADS) for part in range(g // hu)]
    steps = [units[i:i + chains] for i in range(0, len(units), chains)]
    tile_max = [stash(issue(*u), c, *u) for c, u in enumerate(steps[0])]
    sink_term = [None] * chains
    for idx, step in enumerate(steps):
        nxt = steps[idx + 1] if idx + 1 < len(steps) else None
        if nxt is not None:
            sc = [issue(*u) for u in nxt]
        if idx > 0:
            for c, u in enumerate(steps[idx - 1]):
                fold_values(sink_term[c], c, *u)
        for c, u in enumerate(step):
            sink_term[c] = softmax(tile_max[c], c, *u)
        if nxt is not None:
            for c, u in enumerate(nxt):
                tile_max[c] = stash(sc[c], c, *u)
    for c, u in enumerate(steps[-1]):
        fold_values(sink_term[c], c, *u)


def _attn_features(om_ref, os_ref, gom_ref, gos_ref):
    om = jnp.concatenate([om_ref[t].T for t in range(om_ref.shape[0])], axis=0)
    osw = jnp.concatenate([os_ref[t].T for t in range(os_ref.shape[0])], axis=0)
    return _rms(om, gom_ref[...]).astype(BF16), _rms(osw, gos_ref[...]).astype(BF16)


def _out_ffn2_kernel(h1_ref, om_next_ref, os_next_ref, om_first_ref, os_first_ref, gom_ref, gos_ref, wom_ref,
                     wos_ref, g2_ref, wg_ref, wu_ref, wd_ref, gfin_ref, out_ref, am_ref, as_ref, h3_ref):
    s = pl.program_id(0)
    n_tiles = pl.num_programs(0) - 1

    @pl.when(s == 0)
    def _():
        am_ref[...], as_ref[...] = _attn_features(om_first_ref, os_first_ref, gom_ref, gos_ref)
        h3_ref[...] = jnp.zeros_like(h3_ref)

    @pl.when(s < n_tiles)
    def _():
        h2 = (h1_ref[...] + jnp.dot(am_ref[...], wom_ref[...], preferred_element_type=F32)
              + jnp.dot(as_ref[...], wos_ref[...], preferred_element_type=F32))
        out_ref[...] = _rms(h3_ref[...], gfin_ref[...])
        am_ref[...], as_ref[...] = _attn_features(om_next_ref, os_next_ref, gom_ref, gos_ref)
        hn = _rms(h2, g2_ref[...]).astype(BF16)
        h3_ref[...] = h2 + 0.5 * _swiglu(hn, wg_ref, wu_ref, wd_ref)

    @pl.when(s == n_tiles)
    def _():
        out_ref[...] = _rms(h3_ref[...], gfin_ref[...])


def _pack_w_in(w_in):
    o_kpe = Q_LORA + KV_LORA
    o_qs = o_kpe + QK_ROPE
    kpe = jnp.pad(w_in[:, o_kpe:o_qs], ((0, 0), (ROPE_LO, LANES - ROPE_LO - QK_ROPE)))
    return jnp.concatenate([w_in[:, :o_kpe], kpe, w_in[:, o_qs:]], axis=1)


def _pack_w_q_b_t(w_q_b):
    r = w_q_b.shape[0]
    w = w_q_b.reshape(r, MLA_HEADS, QK_NOPE + QK_ROPE)
    w = jnp.pad(w, ((0, 0), (0, 0), (0, LANES - QK_NOPE - QK_ROPE)))
    return w.reshape(r, MLA_HEADS * LANES).T


def _pack_w_kv_b(w_kv_b):
    r = w_kv_b.shape[0]
    w = w_kv_b.reshape(r, MLA_HEADS, QK_NOPE + V_HEAD)
    wk = jnp.pad(w[:, :, :QK_NOPE], ((0, 0), (0, 0), (0, LANES - QK_NOPE))).reshape(r, MLA_HEADS * LANES)
    wv_t = w[:, :, QK_NOPE:].reshape(r, MLA_HEADS * V_HEAD).T
    return wk, wv_t


def _rope_angles(seq):
    pos = jnp.arange(seq, dtype=F32)
    inv_freq = ROPE_THETA ** (-jnp.arange(0, QK_ROPE, 2, dtype=F32) / QK_ROPE)
    ang = pos[:, None] * inv_freq[None, :]
    return jnp.cos(ang), jnp.sin(ang)


def _rope_table_lanes(seq):
    cos, sin = _rope_angles(seq)
    z = lambda n: jnp.zeros((seq, n), F32)
    hi = LANES - ROPE_LO - QK_ROPE
    c = [z(ROPE_LO), cos, cos, z(hi)]
    sa = [z(ROPE_LO), -sin, z(ROPE_HALF + hi)]
    sb = [z(ROPE_LO + ROPE_HALF), sin, z(hi)]
    return jnp.concatenate(c + sa + sb, axis=1)


def _t5_bucket(dist):
    n = jnp.maximum(dist, 0)
    max_exact = REL_BUCKETS // 2
    nf = jnp.maximum(n, 1).astype(F32)
    large = max_exact + (jnp.log(nf / max_exact) / math.log(REL_MAX_DIST / max_exact)
                         * (REL_BUCKETS - max_exact)).astype(jnp.int32)
    large = jnp.minimum(large, REL_BUCKETS - 1)
    return jnp.where(n < max_exact, n, large)


def _const_spec(shape):
    nd = len(shape)
    return pl.BlockSpec(shape, lambda *_: (0,) * nd, pipeline_mode=pl.Buffered(1))


def kernel(x, g_ffn1, w_ffn1_gate, w_ffn1_up, w_ffn1_down, g_mix, w_in, g_q_a, w_q_b, g_kv_a, w_kv_b,
           attn_sinks, rel_bias, g_out_mla, g_out_swa, w_o, g_ffn2, w_ffn2_gate, w_ffn2_up, w_ffn2_down,
           g_final):
    bsz, seq, d = x.shape
    n_tok = bsz * seq
    tm = TOKEN_TILE
    assert d == D_MODEL and seq % tm == 0 and tm % MLA_TILE == 0 and tm % WINDOW == 0
    assert g_ffn1.shape[0] == 1, "single layer"

    row = lambda a: a.reshape(1, -1).astype(F32)
    xf = x.reshape(n_tok, d)

    win = _pack_w_in(w_in[0]).astype(BF16)
    wqbt = _pack_w_q_b_t(w_q_b[0]).astype(BF16)
    wkb, wvbt = _pack_w_kv_b(w_kv_b[0])
    wkb, wvbt = wkb.astype(BF16), wvbt.astype(BF16)
    q_scale = (QK_NOPE + QK_ROPE) ** -0.5 * LOG2E
    cos, sin = _rope_angles(seq)
    rope_q = jnp.concatenate([cos.T, sin.T], axis=0) * q_scale
    rope_k = _rope_table_lanes(seq)

    tiles = n_tok // tm
    tps = seq // tm
    nt_mla, nb_swa = seq // MLA_TILE, seq // WINDOW

    weights1 = [row(g_ffn1[0]), w_ffn1_gate[0].astype(BF16), w_ffn1_up[0].astype(BF16),
                w_ffn1_down[0].astype(BF16), row(g_mix[0]), win, row(g_q_a[0]), wqbt, row(g_kv_a[0]), wkb, wvbt]
    last = tiles - 1
    tile_spec = lambda width: pl.BlockSpec((tm, width), lambda i: (i, 0))
    fm_spec = lambda rows, tok: pl.BlockSpec((None, tm // tok, rows, tok), lambda i: (i // tps, i % tps, 0, 0))
    h1, q_mla, k_mla, v_mla, q_swa, k_swa, v_swa = pl.pallas_call(
        functools.partial(_ffn1_proj_kernel, q_scale=q_scale),
        grid=(tiles,),
        in_specs=[tile_spec(d)] + [_const_spec(a.shape) for a in weights1]
                 + [pl.BlockSpec((QK_ROPE, tm), lambda i: (0, i % tps)),
                    pl.BlockSpec((tm, 3 * LANES), lambda i: (i % tps, 0))],
        out_specs=[tile_spec(d), fm_spec(MLA_HEADS * LANES, MLA_TILE), tile_spec(MLA_HEADS * LANES),
                   fm_spec(MLA_HEADS * MLA_V_ROWS, MLA_TILE), fm_spec(SWA_OUT, WINDOW),
                   tile_spec(SWA_KV_HEADS * LANES),
                   fm_spec(SWA_KV_HEADS * SWA_V_ROWS, WINDOW)],
        out_shape=[jax.ShapeDtypeStruct((n_tok, d), F32),
                   jax.ShapeDtypeStruct((bsz, nt_mla, MLA_HEADS * LANES, MLA_TILE), BF16),
                   jax.ShapeDtypeStruct((n_tok, MLA_HEADS * LANES), BF16),
                   jax.ShapeDtypeStruct((bsz, nt_mla, MLA_HEADS * MLA_V_ROWS, MLA_TILE), BF16),
                   jax.ShapeDtypeStruct((bsz, nb_swa, SWA_OUT, WINDOW), BF16),
                   jax.ShapeDtypeStruct((n_tok, SWA_KV_HEADS * LANES), BF16),
                   jax.ShapeDtypeStruct((bsz, nb_swa, SWA_KV_HEADS * SWA_V_ROWS, WINDOW), BF16)],
        compiler_params=pltpu.CompilerParams(dimension_semantics=("arbitrary",),
                                             vmem_limit_bytes=VMEM_LIMIT),
        name="ffn1_proj",
    )(xf, *weights1, rope_q, rope_k)

    hp = MLA_HEADS_PER_STEP
    o_mla = pl.pallas_call(
        _mla_kernel,
        grid=(bsz, MLA_HEADS // hp),
        in_specs=[pl.BlockSpec((None, nt_mla, hp * LANES, MLA_TILE), lambda b, p: (b, 0, p, 0)),
                  pl.BlockSpec((seq, hp * LANES), lambda b, p: (b, p)),
                  pl.BlockSpec((None, nt_mla, hp * MLA_V_ROWS, MLA_TILE), lambda b, p: (b, 0, p, 0))],
        out_specs=pl.BlockSpec((None, nt_mla, hp * V_HEAD, MLA_TILE), lambda b, p: (b, 0, p, 0)),
        out_shape=jax.ShapeDtypeStruct((bsz, nt_mla, MLA_OUT, MLA_TILE), F32),
        scratch_shapes=[pltpu.VMEM((hp, MLA_TILE, MLA_TILE), F32), pltpu.VMEM((hp, MLA_TILE, MLA_TILE), BF16)],
        compiler_params=pltpu.CompilerParams(dimension_semantics=("arbitrary", "arbitrary"),
                                             vmem_limit_bytes=VMEM_LIMIT),
        name="mla",
    )(q_mla, k_mla, v_mla)

    kj = jnp.arange(2 * WINDOW)[:, None]
    qi = jnp.arange(WINDOW)[None, :]
    bucket_t = _t5_bucket(qi + WINDOW - kj).astype(jnp.int32)
    smem = pl.BlockSpec(memory_space=pltpu.SMEM)
    o_swa = pl.pallas_call(
        _swa_kernel,
        grid=(bsz,),
        in_specs=[_const_spec(bucket_t.shape), smem, smem,
                  pl.BlockSpec((None, nb_swa, SWA_OUT, WINDOW), lambda b: (b, 0, 0, 0)),
                  pl.BlockSpec((seq, SWA_KV_HEADS * LANES), lambda b: (b, 0)),
                  pl.BlockSpec((None, nb_swa, SWA_KV_HEADS * SWA_V_ROWS, WINDOW), lambda b: (b, 0, 0, 0))],
        out_specs=pl.BlockSpec((None, nb_swa, SWA_OUT, WINDOW), lambda b: (b, 0, 0, 0)),
        out_shape=jax.ShapeDtypeStruct((bsz, nb_swa, SWA_OUT, WINDOW), F32),
        scratch_shapes=[pltpu.VMEM((SWA_KV_HEADS, 2 * WINDOW, SWA_GROUP * WINDOW), F32),
                        pltpu.VMEM((SWA_KV_HEADS, 1, SWA_GROUP * WINDOW), F32),
                        pltpu.VMEM((SWA_UNITS_PER_STEP, 2 * WINDOW, SWA_HEADS_PER_UNIT * WINDOW), F32),
                        pltpu.VMEM((SWA_UNITS_PER_STEP, 2 * WINDOW, SWA_HEADS_PER_UNIT * WINDOW), BF16)],
        compiler_params=pltpu.CompilerParams(dimension_semantics=("arbitrary",),
                                             vmem_limit_bytes=VMEM_LIMIT),
        name="swa",
    )(bucket_t, rel_bias.astype(F32), attn_sinks[0].astype(F32), q_swa, k_swa, v_swa)

    wo = w_o[0].astype(BF16)
    weights4 = [row(g_out_mla[0]), row(g_out_swa[0]), wo[:MLA_OUT], wo[MLA_OUT:], row(g_ffn2[0]),
                w_ffn2_gate[0].astype(BF16), w_ffn2_up[0].astype(BF16), w_ffn2_down[0].astype(BF16),
                row(g_final)]
    nxt = lambda s: jnp.minimum(s + 1, last)
    fm_next = lambda rows, tok: pl.BlockSpec(
        (None, tm // tok, rows, tok), lambda s: (nxt(s) // tps, nxt(s) % tps, 0, 0))
    fm_first = lambda rows, tok: pl.BlockSpec((None, tm // tok, rows, tok), lambda s: (0, 0, 0, 0),
                                              pipeline_mode=pl.Buffered(1))
    out = pl.pallas_call(
        _out_ffn2_kernel,
        grid=(tiles + 1,),
        in_specs=[pl.BlockSpec((tm, d), lambda s: (jnp.minimum(s, last), 0)),
                  fm_next(MLA_OUT, MLA_TILE), fm_next(SWA_OUT, WINDOW),
                  fm_first(MLA_OUT, MLA_TILE), fm_first(SWA_OUT, WINDOW)]
                 + [_const_spec(a.shape) for a in weights4],
        out_specs=pl.BlockSpec((tm, d), lambda s: (jnp.maximum(s - 1, 0), 0)),
        out_shape=jax.ShapeDtypeStruct((n_tok, d), F32),
        scratch_shapes=[pltpu.VMEM((tm, MLA_OUT), BF16), pltpu.VMEM((tm, SWA_OUT), BF16),
                        pltpu.VMEM((tm, d), F32)],
        compiler_params=pltpu.CompilerParams(dimension_semantics=("arbitrary",),
                                             vmem_limit_bytes=VMEM_LIMIT),
        name="out_ffn2",
    )(h1, o_mla, o_swa, o_mla, o_swa, *weights4)
    return out.reshape(bsz, seq, d)
```

```python
import functools
import math

import jax
import jax.numpy as jnp
from jax import lax
from jax.experimental import pallas as pl
from jax.experimental.pallas import tpu as pltpu

F32 = jnp.float32
BF16 = jnp.bfloat16

EPS = 1e-6
NEG = -1e30
LOG2E = math.log2(math.e)

D_MODEL = 1024
D_FF = 2816
MLA_HEADS = 8
Q_LORA = 256
KV_LORA = 128
QK_NOPE = 64
QK_ROPE = 32
V_HEAD = 64
ROPE_THETA = 10000.0
SWA_HEADS = 8
SWA_KV_HEADS = 2
SWA_HEAD_DIM = 64
SWA_GROUP = SWA_HEADS // SWA_KV_HEADS
WINDOW = 128
REL_BUCKETS = 32
REL_MAX_DIST = 128
MLA_OUT = MLA_HEADS * V_HEAD
SWA_OUT = SWA_HEADS * SWA_HEAD_DIM
SWA_KV = SWA_KV_HEADS * SWA_HEAD_DIM

LANES = 128
ROPE_LO = QK_NOPE
ROPE_HALF = QK_ROPE // 2

P_CQ = 0
P_CKV = P_CQ + Q_LORA
P_KPE = P_CKV + KV_LORA
P_QS = P_KPE + LANES
P_KS = P_QS + SWA_OUT
P_VS = P_KS + SWA_KV
P_END = P_VS + SWA_KV

TOKEN_TILE = 512
MLA_TILE = 256
MLA_HEADS_PER_STEP = 2
SWA_HEADS_PER_UNIT = 4
SWA_UNITS_PER_STEP = 2
ONES_ROWS = 16
MLA_V_ROWS = V_HEAD + ONES_ROWS
SWA_V_ROWS = SWA_HEAD_DIM + ONES_ROWS
VMEM_LIMIT = 56 * 1024 * 1024

_NT = (((1,), (1,)), ((), ()))


def _rms(x, g):
    return x * lax.rsqrt(jnp.mean(x * x, axis=-1, keepdims=True) + EPS) * g


def _swiglu(xn, wg_ref, wu_ref, wd_ref):
    gate = jnp.dot(xn, wg_ref[...], preferred_element_type=F32)
    up = jnp.dot(xn, wu_ref[...], preferred_element_type=F32)
    act = (gate * jax.nn.sigmoid(gate) * up).astype(BF16)
    return jnp.dot(act, wd_ref[...], preferred_element_type=F32)


def _rope_lanes(x, tab_ref):
    c = tab_ref[:, 0:LANES]
    sa = tab_ref[:, LANES:2 * LANES]
    sb = tab_ref[:, 2 * LANES:3 * LANES]
    return (x * c + pltpu.roll(x, LANES - ROPE_HALF, 1) * sa + pltpu.roll(x, ROPE_HALF, 1) * sb)


def _ffn1_proj_kernel(x_ref, g1_ref, wg_ref, wu_ref, wd_ref, gmix_ref, win_ref, gqa_ref, wqbt_ref,
                      gkva_ref, wkb_ref, wvbt_ref, ropeq_ref, ropek_ref,
                      h1_ref, qm_ref, km_ref, vm_ref, qs_ref, ks_ref, vs_ref, *, q_scale):
    tm = x_ref.shape[0]
    x = x_ref[...]
    xn = _rms(x, g1_ref[...]).astype(BF16)
    h1 = x + 0.5 * _swiglu(xn, wg_ref, wu_ref, wd_ref)
    h1_ref[...] = h1

    u = _rms(h1, gmix_ref[...]).astype(BF16)
    proj = jnp.dot(u, win_ref[...], preferred_element_type=F32)

    cq = _rms(proj[:, P_CQ:P_CQ + Q_LORA], gqa_ref[...]).astype(BF16)
    qt = lax.dot_general(wqbt_ref[...], cq, _NT, preferred_element_type=F32)
    cs = ropeq_ref[0:ROPE_HALF, :]
    sn = ropeq_ref[ROPE_HALF:QK_ROPE, :]
    for h in range(MLA_HEADS):
        r = h * LANES
        x1 = qt[r + ROPE_LO:r + ROPE_LO + ROPE_HALF]
        x2 = qt[r + ROPE_LO + ROPE_HALF:r + ROPE_LO + QK_ROPE]
        qh = jnp.concatenate([qt[r:r + QK_NOPE] * q_scale, x1 * cs - x2 * sn, x2 * cs + x1 * sn,
                              qt[r + ROPE_LO + QK_ROPE:r + LANES]], axis=0).astype(BF16)
        for t in range(tm // MLA_TILE):
            qm_ref[t, r:r + LANES, :] = qh[:, t * MLA_TILE:(t + 1) * MLA_TILE]

    ckv = _rms(proj[:, P_CKV:P_CKV + KV_LORA], gkva_ref[...]).astype(BF16)
    kn = jnp.dot(ckv, wkb_ref[...], preferred_element_type=F32)
    kpe = _rope_lanes(proj[:, P_KPE:P_KPE + LANES], ropek_ref)
    for h in range(MLA_HEADS):
        km_ref[:, h * LANES:(h + 1) * LANES] = (kn[:, h * LANES:(h + 1) * LANES] + kpe).astype(BF16)
    vt = lax.dot_general(wvbt_ref[...], ckv, _NT, preferred_element_type=F32).astype(BF16)
    ones = jnp.ones((ONES_ROWS, tm), BF16)
    vt = jnp.concatenate([blk for h in range(MLA_HEADS) for blk in (vt[h * V_HEAD:(h + 1) * V_HEAD], ones)],
                         axis=0)
    for t in range(tm // MLA_TILE):
        vm_ref[t] = vt[:, t * MLA_TILE:(t + 1) * MLA_TILE]

    qst = (proj[:, P_QS:P_QS + SWA_OUT] * (SWA_HEAD_DIM ** -0.5 * LOG2E)).T.astype(BF16)
    vst = proj[:, P_VS:P_VS + SWA_KV].T.astype(BF16)
    vst = jnp.concatenate([blk for h in range(SWA_KV_HEADS)
                           for blk in (vst[h * SWA_HEAD_DIM:(h + 1) * SWA_HEAD_DIM], ones)], axis=0)
    for t in range(tm // WINDOW):
        qs_ref[t] = qst[:, t * WINDOW:(t + 1) * WINDOW]
        vs_ref[t] = vst[:, t * WINDOW:(t + 1) * WINDOW]
    for h in range(SWA_KV_HEADS):
        ks_ref[:, h * LANES:h * LANES + SWA_HEAD_DIM] = (
            proj[:, P_KS + h * SWA_HEAD_DIM:P_KS + (h + 1) * SWA_HEAD_DIM].astype(BF16))
        ks_ref[:, h * LANES + SWA_HEAD_DIM:(h + 1) * LANES] = jnp.zeros((tm, LANES - SWA_HEAD_DIM), BF16)


def _mla_kernel(q_ref, k_ref, v_ref, o_ref, s_ref, p_ref):
    nt, rows, t = q_ref.shape
    hp = rows // LANES
    vr = v_ref.shape[1] // hp
    key = lax.broadcasted_iota(jnp.int32, (t, t), 0)
    qry = lax.broadcasted_iota(jnp.int32, (t, t), 1)
    causal = key <= qry

    def issue(i, j, hh):
        return jnp.dot(k_ref[j * t:(j + 1) * t, hh * LANES:(hh + 1) * LANES],
                       q_ref[i, hh * LANES:(hh + 1) * LANES, :], preferred_element_type=F32)

    def stash(s, masked, hh):
        if masked:
            s = jnp.where(causal, s, NEG)
        s_ref[hh] = s
        return jnp.max(s, axis=0, keepdims=True)

    def fold_values(jp, st, hh):
        m, acc, alpha = st
        pv = jnp.dot(v_ref[jp, hh * vr:(hh + 1) * vr, :], p_ref[hh], preferred_element_type=F32)
        return m, alpha * acc + pv, alpha

    def softmax(tile_max, st, hh):
        m, acc, _ = st
        m_new = jnp.maximum(m, tile_max)
        p_ref[hh] = jnp.exp2(s_ref[hh] - m_new).astype(BF16)
        return m_new, acc, jnp.exp2(m - m_new)

    def finish(i, st, hh):
        acc = st[1]
        o_ref[i, hh * V_HEAD:(hh + 1) * V_HEAD, :] = acc[0:V_HEAD] / acc[V_HEAD:V_HEAD + 1]

    fresh = (jnp.full((1, t), NEG, F32), jnp.zeros((vr, t), F32), jnp.zeros((1, t), F32))

    tiles = [(i, j) for i in range(nt) for j in range(i + 1)]
    tile_max = [stash(issue(0, 0, hh), True, hh) for hh in range(hp)]
    state = [fresh] * hp
    for n, (i, j) in enumerate(tiles):
        prev = tiles[n - 1] if n > 0 else None
        nxt = tiles[n + 1] if n + 1 < len(tiles) else None
        if nxt is not None:
            sc = [issue(*nxt, hh) for hh in range(hp)]
        for hh in range(hp):
            st = state[hh]
            if prev is not None:
                st = fold_values(prev[1], st, hh)
                if prev[0] != i:
                    finish(prev[0], st, hh)
                    st = fresh
            state[hh] = softmax(tile_max[hh], st, hh)
            if nxt is not None:
                tile_max[hh] = stash(sc[hh], nxt[0] == nxt[1], hh)
    for hh in range(hp):
        finish(nt - 1, fold_values(nt - 1, state[hh], hh), hh)


def _swa_kernel(bucket_ref, relb_ref, sink_ref, q_ref, k_ref, v_ref, o_ref, bias_ref, sinkrow_ref,
                s_ref, p_ref):
    w, dh, g = WINDOW, SWA_HEAD_DIM, SWA_GROUP
    nb = q_ref.shape[0]
    vr = v_ref.shape[1] // SWA_KV_HEADS

    @pl.when(pl.program_id(0) == 0)
    def _():
        bucket = bucket_ref[...]
        for kvh in range(SWA_KV_HEADS):
            for gg in range(g):
                h = kvh * g + gg
                b = jnp.zeros((w, w), F32)
                for r in range(REL_BUCKETS):
                    b = jnp.where(bucket == r, relb_ref[r, h] * LOG2E, b)
                bias_ref[kvh, :, gg * w:(gg + 1) * w] = b
                sinkrow_ref[kvh, :, gg * w:(gg + 1) * w] = jnp.full((1, w), sink_ref[h] * LOG2E, F32)

    chains = s_ref.shape[0]
    hu = s_ref.shape[2] // w
    row = lax.broadcasted_iota(jnp.int32, (w, w), 0)
    qry = lax.broadcasted_iota(jnp.int32, (w, w), 1)
    from_prev = jnp.concatenate([row > qry] * hu, axis=1)

    def key_rows(n):
        return (0, w) if n == 0 else ((n - 1) * w, 2 * w)

    def heads(kvh, part):
        return [kvh * g + part * hu + i for i in range(hu)]

    def cols(part):
        return slice(part * hu * w, (part + 1) * hu * w)

    def issue(n, kvh, part):
        qt = jnp.concatenate([q_ref[n, h * dh:(h + 1) * dh, :] for h in heads(kvh, part)], axis=1)
        start, size = key_rows(n)
        k = k_ref[start:start + size, kvh * LANES:kvh * LANES + dh]
        return jnp.dot(k, qt, preferred_element_type=F32)

    def stash(sc, c, n, kvh, part):
        bias = bias_ref[kvh, :, cols(part)]
        if n == 0:
            s = jnp.where(from_prev, NEG, sc + bias)
        else:
            s = jnp.where(from_prev, sc[0:w], sc[w:2 * w]) + bias
        s_ref[c] = s
        return jnp.max(s, axis=0, keepdims=True)

    def softmax(tile_max, c, n, kvh, part):
        sink = sinkrow_ref[kvh, :, cols(part)]
        m = jnp.maximum(tile_max, sink)
        p = jnp.exp2(s_ref[c] - m)
        if n == 0:
            p_ref[c, w:2 * w, :] = p.astype(BF16)
        else:
            p_ref[c, 0:w, :] = jnp.where(from_prev, p, 0.0).astype(BF16)
            p_ref[c, w:2 * w, :] = jnp.where(from_prev, 0.0, p).astype(BF16)
        return jnp.exp2(sink - m)

    def fold_values(sink_term, c, n, kvh, part):
        vrows = slice(kvh * vr, (kvh + 1) * vr)
        if n == 0:
            vwin_t = v_ref[0, vrows, :]
        else:
            vwin_t = jnp.concatenate([v_ref[n - 1, vrows, :], v_ref[n, vrows, :]], axis=1)
        lo = 2 * w - key_rows(n)[1]
        pv = jnp.dot(vwin_t, p_ref[c, lo:, :], preferred_element_type=F32)
        o = pv[0:dh] / (pv[dh:dh + 1] + sink_term)
        for i, h in enumerate(heads(kvh, part)):
            o_ref[n, h * dh:(h + 1) * dh, :] = o[:, i * w:(i + 1) * w]

    units = [(n, kvh, part) for n in range(nb) for kvh in range(SWA_KV_HEADS) for part in range(g // hu)]
    steps = [units[i:i + chains] for i in range(0, len(units), chains)]
    tile_max = [stash(issue(*u), c, *u) for c, u in enumerate(steps[0])]
    sink_term = [None] * chains
    for idx, step in enumerate(steps):
        nxt = steps[idx + 1] if idx + 1 < len(steps) else None
        if nxt is not None:
            sc = [issue(*u) for u in nxt]
        if idx > 0:
            for c, u in enumerate(steps[idx - 1]):
                fold_values(sink_term[c], c, *u)
        for c, u in enumerate(step):
            sink_term[c] = softmax(tile_max[c], c, *u)
        if nxt is not None:
            for c, u in enumerate(nxt):
                tile_max[c] = stash(sc[c], c, *u)
    for c, u in enumerate(steps[-1]):
        fold_values(sink_term[c], c, *u)


def _attn_features(om_ref, os_ref, gom_ref, gos_ref):
    om = jnp.concatenate([om_ref[t].T for t in range(om_ref.shape[0])], axis=0)
    osw = jnp.concatenate([os_ref[t].T for t in range(os_ref.shape[0])], axis=0)
    return _rms(om, gom_ref[...]).astype(BF16), _rms(osw, gos_ref[...]).astype(BF16)


def _out_ffn2_kernel(h1_ref, om_next_ref, os_next_ref, om_first_ref, os_first_ref, gom_ref, gos_ref, wom_ref,
                     wos_ref, g2_ref, wg_ref, wu_ref, wd_ref, gfin_ref, out_ref, am_ref, as_ref, h3_ref):
    s = pl.program_id(0)
    n_tiles = pl.num_programs(0) - 1

    @pl.when(s == 0)
    def _():
        am_ref[...], as_ref[...] = _attn_features(om_first_ref, os_first_ref, gom_ref, gos_ref)
        h3_ref[...] = jnp.zeros_like(h3_ref)

    @pl.when(s < n_tiles)
    def _():
        h2 = (h1_ref[...] + jnp.dot(am_ref[...], wom_ref[...], preferred_element_type=F32)
              + jnp.dot(as_ref[...], wos_ref[...], preferred_element_type=F32))
        out_ref[...] = _rms(h3_ref[...], gfin_ref[...])
        am_ref[...], as_ref[...] = _attn_features(om_next_ref, os_next_ref, gom_ref, gos_ref)
        hn = _rms(h2, g2_ref[...]).astype(BF16)
        h3_ref[...] = h2 + 0.5 * _swiglu(hn, wg_ref, wu_ref, wd_ref)

    @pl.when(s == n_tiles)
    def _():
        out_ref[...] = _rms(h3_ref[...], gfin_ref[...])


def _pack_w_in(w_in):
    o_kpe = Q_LORA + KV_LORA
    o_qs = o_kpe + QK_ROPE
    kpe = jnp.pad(w_in[:, o_kpe:o_qs], ((0, 0), (ROPE_LO, LANES - ROPE_LO - QK_ROPE)))
    return jnp.concatenate([w_in[:, :o_kpe], kpe, w_in[:, o_qs:]], axis=1)


def _pack_w_q_b_t(w_q_b):
    r = w_q_b.shape[0]
    w = w_q_b.reshape(r, MLA_HEADS, QK_NOPE + QK_ROPE)
    w = jnp.pad(w, ((0, 0), (0, 0), (0, LANES - QK_NOPE - QK_ROPE)))
    return w.reshape(r, MLA_HEADS * LANES).T


def _pack_w_kv_b(w_kv_b):
    r = w_kv_b.shape[0]
    w = w_kv_b.reshape(r, MLA_HEADS, QK_NOPE + V_HEAD)
    wk = jnp.pad(w[:, :, :QK_NOPE], ((0, 0), (0, 0), (0, LANES - QK_NOPE))).reshape(r, MLA_HEADS * LANES)
    wv_t = w[:, :, QK_NOPE:].reshape(r, MLA_HEADS * V_HEAD).T
    return wk, wv_t


def _rope_angles(seq):
    pos = jnp.arange(seq, dtype=F32)
    inv_freq = ROPE_THETA ** (-jnp.arange(0, QK_ROPE, 2, dtype=F32) / QK_ROPE)
    ang = pos[:, None] * inv_freq[None, :]
    return jnp.cos(ang), jnp.sin(ang)


def _rope_table_lanes(seq):
    cos, sin = _rope_angles(seq)
    z = lambda n: jnp.zeros((seq, n), F32)
    hi = LANES - ROPE_LO - QK_ROPE
    c = [z(ROPE_LO), cos, cos, z(hi)]
    sa = [z(ROPE_LO), -sin, z(ROPE_HALF + hi)]
    sb = [z(ROPE_LO + ROPE_HALF), sin, z(hi)]
    return jnp.concatenate(c + sa + sb, axis=1)


def _t5_bucket(dist):
    n = jnp.maximum(dist, 0)
    max_exact = REL_BUCKETS // 2
    nf = jnp.maximum(n, 1).astype(F32)
    large = max_exact + (jnp.log(nf / max_exact) / math.log(REL_MAX_DIST / max_exact)
                         * (REL_BUCKETS - max_exact)).astype(jnp.int32)
    large = jnp.minimum(large, REL_BUCKETS - 1)
    return jnp.where(n < max_exact, n, large)


def _const_spec(shape):
    nd = len(shape)
    return pl.BlockSpec(shape, lambda *_: (0,) * nd, pipeline_mode=pl.Buffered(1))


def kernel(x, g_ffn1, w_ffn1_gate, w_ffn1_up, w_ffn1_down, g_mix, w_in, g_q_a, w_q_b, g_kv_a, w_kv_b,
           attn_sinks, rel_bias, g_out_mla, g_out_swa, w_o, g_ffn2, w_ffn2_gate, w_ffn2_up, w_ffn2_down,
           g_final):
    bsz, seq, d = x.shape
    n_tok = bsz * seq
    tm = TOKEN_TILE
    assert d == D_MODEL and seq % tm == 0 and tm % MLA_TILE == 0 and tm % WINDOW == 0
    assert g_ffn1.shape[0] == 1, "single layer"

    row = lambda a: a.reshape(1, -1).astype(F32)
    xf = x.reshape(n_tok, d)

    win = _pack_w_in(w_in[0]).astype(BF16)
    wqbt = _pack_w_q_b_t(w_q_b[0]).astype(BF16)
    wkb, wvbt = _pack_w_kv_b(w_kv_b[0])
    wkb, wvbt = wkb.astype(BF16), wvbt.astype(BF16)
    q_scale = (QK_NOPE + QK_ROPE) ** -0.5 * LOG2E
    cos, sin = _rope_angles(seq)
    rope_q = jnp.concatenate([cos.T, sin.T], axis=0) * q_scale
    rope_k = _rope_table_lanes(seq)

    tiles = n_tok // tm
    tps = seq // tm
    nt_mla, nb_swa = seq // MLA_TILE, seq // WINDOW

    weights1 = [row(g_ffn1[0]), w_ffn1_gate[0].astype(BF16), w_ffn1_up[0].astype(BF16),
                w_ffn1_down[0].astype(BF16), row(g_mix[0]), win, row(g_q_a[0]), wqbt, row(g_kv_a[0]), wkb, wvbt]
    last = tiles - 1
    tile_spec = lambda width: pl.BlockSpec((tm, width), lambda i: (i, 0))
    fm_spec = lambda rows, tok: pl.BlockSpec((None, tm // tok, rows, tok), lambda i: (i // tps, i % tps, 0, 0))
    h1, q_mla, k_mla, v_mla, q_swa, k_swa, v_swa = pl.pallas_call(
        functools.partial(_ffn1_proj_kernel, q_scale=q_scale),
        grid=(tiles,),
        in_specs=[tile_spec(d)] + [_const_spec(a.shape) for a in weights1]
                 + [pl.BlockSpec((QK_ROPE, tm), lambda i: (0, i % tps)),
                    pl.BlockSpec((tm, 3 * LANES), lambda i: (i % tps, 0))],
        out_specs=[tile_spec(d), fm_spec(MLA_HEADS * LANES, MLA_TILE), tile_spec(MLA_HEADS * LANES),
                   fm_spec(MLA_HEADS * MLA_V_ROWS, MLA_TILE), fm_spec(SWA_OUT, WINDOW),
                   tile_spec(SWA_KV_HEADS * LANES),
                   fm_spec(SWA_KV_HEADS * SWA_V_ROWS, WINDOW)],
        out_shape=[jax.ShapeDtypeStruct((n_tok, d), F32),
                   jax.ShapeDtypeStruct((bsz, nt_mla, MLA_HEADS * LANES, MLA_TILE), BF16),
                   jax.ShapeDtypeStruct((n_tok, MLA_HEADS * LANES), BF16),
                   jax.ShapeDtypeStruct((bsz, nt_mla, MLA_HEADS * MLA_V_ROWS, MLA_TILE), BF16),
                   jax.ShapeDtypeStruct((bsz, nb_swa, SWA_OUT, WINDOW), BF16),
                   jax.ShapeDtypeStruct((n_tok, SWA_KV_HEADS * LANES), BF16),
                   jax.ShapeDtypeStruct((bsz, nb_swa, SWA_KV_HEADS * SWA_V_ROWS, WINDOW), BF16)],
        compiler_params=pltpu.CompilerParams(dimension_semantics=("arbitrary",),
                                             vmem_limit_bytes=VMEM_LIMIT),
        name="ffn1_proj",
    )(xf, *weights1, rope_q, rope_k)

    hp = MLA_HEADS_PER_STEP
    o_mla = pl.pallas_call(
        _mla_kernel,
        grid=(bsz, MLA_HEADS // hp),
        in_specs=[pl.BlockSpec((None, nt_mla, hp * LANES, MLA_TILE), lambda b, p: (b, 0, p, 0)),
                  pl.BlockSpec((seq, hp * LANES), lambda b, p: (b, p)),
                  pl.BlockSpec((None, nt_mla, hp * MLA_V_ROWS, MLA_TILE), lambda b, p: (b, 0, p, 0))],
        out_specs=pl.BlockSpec((None, nt_mla, hp * V_HEAD, MLA_TILE), lambda b, p: (b, 0, p, 0)),
        out_shape=jax.ShapeDtypeStruct((bsz, nt_mla, MLA_OUT, MLA_TILE), F32),
        scratch_shapes=[pltpu.VMEM((hp, MLA_TILE, MLA_TILE), F32), pltpu.VMEM((hp, MLA_TILE, MLA_TILE), BF16)],
        compiler_params=pltpu.CompilerParams(dimension_semantics=("arbitrary", "arbitrary"),
                                             vmem_limit_bytes=VMEM_LIMIT),
        name="mla",
    )(q_mla, k_mla, v_mla)

    kj = jnp.arange(WINDOW)[:, None]
    qi = jnp.arange(WINDOW)[None, :]
    bucket_t = _t5_bucket(jnp.where(kj > qi, qi + WINDOW - kj, qi - kj)).astype(jnp.int32)
    smem = pl.BlockSpec(memory_space=pltpu.SMEM)
    o_swa = pl.pallas_call(
        _swa_kernel,
        grid=(bsz,),
        in_specs=[_const_spec(bucket_t.shape), smem, smem,
                  pl.BlockSpec((None, nb_swa, SWA_OUT, WINDOW), lambda b: (b, 0, 0, 0)),
                  pl.BlockSpec((seq, SWA_KV_HEADS * LANES), lambda b: (b, 0)),
                  pl.BlockSpec((None, nb_swa, SWA_KV_HEADS * SWA_V_ROWS, WINDOW), lambda b: (b, 0, 0, 0))],
        out_specs=pl.BlockSpec((None, nb_swa, SWA_OUT, WINDOW), lambda b: (b, 0, 0, 0)),
        out_shape=jax.ShapeDtypeStruct((bsz, nb_swa, SWA_OUT, WINDOW), F32),
        scratch_shapes=[pltpu.VMEM((SWA_KV_HEADS, WINDOW, SWA_GROUP * WINDOW), F32),
                        pltpu.VMEM((SWA_KV_HEADS, 1, SWA_GROUP * WINDOW), F32),
                        pltpu.VMEM((SWA_UNITS_PER_STEP, WINDOW, SWA_HEADS_PER_UNIT * WINDOW), F32),
                        pltpu.VMEM((SWA_UNITS_PER_STEP, 2 * WINDOW, SWA_HEADS_PER_UNIT * WINDOW), BF16)],
        compiler_params=pltpu.CompilerParams(dimension_semantics=("arbitrary",),
                                             vmem_limit_bytes=VMEM_LIMIT),
        name="swa",
    )(bucket_t, rel_bias.astype(F32), attn_sinks[0].astype(F32), q_swa, k_swa, v_swa)

    wo = w_o[0].astype(BF16)
    weights4 = [row(g_out_mla[0]), row(g_out_swa[0]), wo[:MLA_OUT], wo[MLA_OUT:], row(g_ffn2[0]),
                w_ffn2_gate[0].astype(BF16), w_ffn2_up[0].astype(BF16), w_ffn2_down[0].astype(BF16),
                row(g_final)]
    nxt = lambda s: jnp.minimum(s + 1, last)
    fm_next = lambda rows, tok: pl.BlockSpec(
        (None, tm // tok, rows, tok), lambda s: (nxt(s) // tps, nxt(s) % tps, 0, 0))
    fm_first = lambda rows, tok: pl.BlockSpec((None, tm // tok, rows, tok), lambda s: (0, 0, 0, 0),
                                              pipeline_mode=pl.Buffered(1))
    out = pl.pallas_call(
        _out_ffn2_kernel,
        grid=(tiles + 1,),
        in_specs=[pl.BlockSpec((tm, d), lambda s: (jnp.minimum(s, last), 0)),
                  fm_next(MLA_OUT, MLA_TILE), fm_next(SWA_OUT, WINDOW),
                  fm_first(MLA_OUT, MLA_TILE), fm_first(SWA_OUT, WINDOW)]
                 + [_const_spec(a.shape) for a in weights4],
        out_specs=pl.BlockSpec((tm, d), lambda s: (jnp.maximum(s - 1, 0), 0)),
        out_shape=jax.ShapeDtypeStruct((n_tok, d), F32),
        scratch_shapes=[pltpu.VMEM((tm, MLA_OUT), BF16), pltpu.VMEM((tm, SWA_OUT), BF16),
                        pltpu.VMEM((tm, d), F32)],
        compiler_params=pltpu.CompilerParams(dimension_semantics=("arbitrary",),
                                             vmem_limit_bytes=VMEM_LIMIT),
        name="out_ffn2",
    )(h1, o_mla, o_swa, o_mla, o_swa, *weights4)
    return out.reshape(bsz, seq, d)
```

```python
import functools
import math

import jax
import jax.numpy as jnp
from jax import lax
from jax.experimental import pallas as pl
from jax.experimental.pallas import tpu as pltpu

F32 = jnp.float32
BF16 = jnp.bfloat16

EPS = 1e-6
NEG = -1e30
LOG2E = math.log2(math.e)

D_MODEL = 1024
D_FF = 2816
MLA_HEADS = 8
Q_LORA = 256
KV_LORA = 128
QK_NOPE = 64
QK_ROPE = 32
V_HEAD = 64
ROPE_THETA = 10000.0
SWA_HEADS = 8
SWA_KV_HEADS = 2
SWA_HEAD_DIM = 64
SWA_GROUP = SWA_HEADS // SWA_KV_HEADS
WINDOW = 128
REL_BUCKETS = 32
REL_MAX_DIST = 128
MLA_OUT = MLA_HEADS * V_HEAD
SWA_OUT = SWA_HEADS * SWA_HEAD_DIM
SWA_KV = SWA_KV_HEADS * SWA_HEAD_DIM

LANES = 128
ROPE_LO = QK_NOPE
ROPE_HALF = QK_ROPE // 2

P_CQ = 0
P_CKV = P_CQ + Q_LORA
P_KPE = P_CKV + KV_LORA
P_QS = P_KPE + LANES
P_KS = P_QS + SWA_OUT
P_VS = P_KS + SWA_KV
P_END = P_VS + SWA_KV

TOKEN_TILE = 512
MLA_TILE = 256
MLA_HEADS_PER_STEP = 2
MLA_SCORE_SLOTS = 3
SWA_HEADS_PER_UNIT = 4
SWA_UNITS_PER_STEP = 2
ONES_ROWS = 16
MLA_V_ROWS = V_HEAD + ONES_ROWS
SWA_V_ROWS = SWA_HEAD_DIM + ONES_ROWS
VMEM_LIMIT = 56 * 1024 * 1024

_NT = (((1,), (1,)), ((), ()))


def _rms(x, g):
    return x * lax.rsqrt(jnp.mean(x * x, axis=-1, keepdims=True) + EPS) * g


def _swiglu(xn, wg_ref, wu_ref, wd_ref):
    gate = jnp.dot(xn, wg_ref[...], preferred_element_type=F32)
    up = jnp.dot(xn, wu_ref[...], preferred_element_type=F32)
    act = (gate * jax.nn.sigmoid(gate) * up).astype(BF16)
    return jnp.dot(act, wd_ref[...], preferred_element_type=F32)


def _rope_lanes(x, tab_ref):
    c = tab_ref[:, 0:LANES]
    sa = tab_ref[:, LANES:2 * LANES]
    sb = tab_ref[:, 2 * LANES:3 * LANES]
    return (x * c + pltpu.roll(x, LANES - ROPE_HALF, 1) * sa + pltpu.roll(x, ROPE_HALF, 1) * sb)


def _ffn1_proj_kernel(x_ref, g1_ref, wg_ref, wu_ref, wd_ref, gmix_ref, win_ref, gqa_ref, wqbt_ref,
                      gkva_ref, wkb_ref, wvbt_ref, ropeq_ref, ropek_ref,
                      h1_ref, qm_ref, km_ref, vm_ref, qs_ref, ks_ref, vs_ref, *, q_scale):
    tm = x_ref.shape[0]
    x = x_ref[...]
    xn = _rms(x, g1_ref[...]).astype(BF16)
    h1 = x + 0.5 * _swiglu(xn, wg_ref, wu_ref, wd_ref)
    h1_ref[...] = h1

    u = _rms(h1, gmix_ref[...]).astype(BF16)
    proj = jnp.dot(u, win_ref[...], preferred_element_type=F32)

    cq = _rms(proj[:, P_CQ:P_CQ + Q_LORA], gqa_ref[...]).astype(BF16)
    qt = lax.dot_general(wqbt_ref[...], cq, _NT, preferred_element_type=F32)
    cs = ropeq_ref[0:ROPE_HALF, :]
    sn = ropeq_ref[ROPE_HALF:QK_ROPE, :]
    for h in range(MLA_HEADS):
        r = h * LANES
        x1 = qt[r + ROPE_LO:r + ROPE_LO + ROPE_HALF]
        x2 = qt[r + ROPE_LO + ROPE_HALF:r + ROPE_LO + QK_ROPE]
        qh = jnp.concatenate([qt[r:r + QK_NOPE] * q_scale, x1 * cs - x2 * sn, x2 * cs + x1 * sn,
                              qt[r + ROPE_LO + QK_ROPE:r + LANES]], axis=0).astype(BF16)
        for t in range(tm // MLA_TILE):
            qm_ref[t, r:r + LANES, :] = qh[:, t * MLA_TILE:(t + 1) * MLA_TILE]

    ckv = _rms(proj[:, P_CKV:P_CKV + KV_LORA], gkva_ref[...]).astype(BF16)
    kn = jnp.dot(ckv, wkb_ref[...], preferred_element_type=F32)
    kpe = _rope_lanes(proj[:, P_KPE:P_KPE + LANES], ropek_ref)
    for h in range(MLA_HEADS):
        km_ref[:, h * LANES:(h + 1) * LANES] = (kn[:, h * LANES:(h + 1) * LANES] + kpe).astype(BF16)
    vt = lax.dot_general(wvbt_ref[...], ckv, _NT, preferred_element_type=F32).astype(BF16)
    ones = jnp.ones((ONES_ROWS, tm), BF16)
    vt = jnp.concatenate([blk for h in range(MLA_HEADS) for blk in (vt[h * V_HEAD:(h + 1) * V_HEAD], ones)],
                         axis=0)
    for t in range(tm // MLA_TILE):
        vm_ref[t] = vt[:, t * MLA_TILE:(t + 1) * MLA_TILE]

    qst = (proj[:, P_QS:P_QS + SWA_OUT] * (SWA_HEAD_DIM ** -0.5 * LOG2E)).T.astype(BF16)
    vst = proj[:, P_VS:P_VS + SWA_KV].T.astype(BF16)
    vst = jnp.concatenate([blk for h in range(SWA_KV_HEADS)
                           for blk in (vst[h * SWA_HEAD_DIM:(h + 1) * SWA_HEAD_DIM], ones)], axis=0)
    for t in range(tm // WINDOW):
        qs_ref[t] = qst[:, t * WINDOW:(t + 1) * WINDOW]
        vs_ref[t] = vst[:, t * WINDOW:(t + 1) * WINDOW]
    for h in range(SWA_KV_HEADS):
        ks_ref[:, h * LANES:h * LANES + SWA_HEAD_DIM] = (
            proj[:, P_KS + h * SWA_HEAD_DIM:P_KS + (h + 1) * SWA_HEAD_DIM].astype(BF16))
        ks_ref[:, h * LANES + SWA_HEAD_DIM:(h + 1) * LANES] = jnp.zeros((tm, LANES - SWA_HEAD_DIM), BF16)


def _mla_kernel(q_ref, k_ref, v_ref, o_ref, s_ref, p_ref):
    nt, rows, t = q_ref.shape
    hp = rows // LANES
    vr = v_ref.shape[1] // hp
    key = lax.broadcasted_iota(jnp.int32, (t, t), 0)
    qry = lax.broadcasted_iota(jnp.int32, (t, t), 1)
    causal = key <= qry

    def issue(i, j, hh):
        return jnp.dot(k_ref[j * t:(j + 1) * t, hh * LANES:(hh + 1) * LANES],
                       q_ref[i, hh * LANES:(hh + 1) * LANES, :], preferred_element_type=F32)

    def fold_values(jp, st, hh):
        m, acc, alpha = st
        pv = jnp.dot(v_ref[jp, hh * vr:(hh + 1) * vr, :], p_ref[hh], preferred_element_type=F32)
        return m, alpha * acc + pv, alpha

    def softmax(tile_max, st, hh, slot):
        m, acc, _ = st
        m_new = jnp.maximum(m, tile_max)
        p_ref[hh] = jnp.exp2(s_ref[slot, hh] - m_new).astype(BF16)
        return m_new, acc, jnp.exp2(m - m_new)

    def finish(i, st, hh):
        acc = st[1]
        o_ref[i, hh * V_HEAD:(hh + 1) * V_HEAD, :] = acc[0:V_HEAD] / acc[V_HEAD:V_HEAD + 1]

    fresh = (jnp.full((1, t), NEG, F32), jnp.zeros((vr, t), F32), jnp.zeros((1, t), F32))

    tiles = [(i, j) for i in range(nt) for j in range(i + 1)]
    slots = s_ref.shape[0]

    def issue_to_scratch(tile, hh, slot):
        s = issue(*tile, hh)
        if tile[0] == tile[1]:
            s = jnp.where(causal, s, NEG)
        s_ref[slot, hh] = s

    ahead = slots - 1
    for a in range(min(ahead, len(tiles))):
        for hh in range(hp):
            issue_to_scratch(tiles[a], hh, a)
    tile_max = [jnp.max(s_ref[0, hh], axis=0, keepdims=True) for hh in range(hp)]
    state = [fresh] * hp
    for n, (i, j) in enumerate(tiles):
        prev = tiles[n - 1] if n > 0 else None
        nxt = tiles[n + 1] if n + 1 < len(tiles) else None
        if n + ahead < len(tiles):
            for hh in range(hp):
                issue_to_scratch(tiles[n + ahead], hh, (n + ahead) % slots)
        for hh in range(hp):
            st = state[hh]
            if prev is not None:
                st = fold_values(prev[1], st, hh)
                if prev[0] != i:
                    finish(prev[0], st, hh)
                    st = fresh
            state[hh] = softmax(tile_max[hh], st, hh, n % slots)
            if nxt is not None:
                tile_max[hh] = jnp.max(s_ref[(n + 1) % slots, hh], axis=0, keepdims=True)
    for hh in range(hp):
        finish(nt - 1, fold_values(nt - 1, state[hh], hh), hh)


def _swa_kernel(bucket_ref, relb_ref, sink_ref, q_ref, k_ref, v_ref, o_ref, bias_ref, sinkrow_ref,
                s_ref, p_ref):
    w, dh, g = WINDOW, SWA_HEAD_DIM, SWA_GROUP
    nb = q_ref.shape[0]
    vr = v_ref.shape[1] // SWA_KV_HEADS

    @pl.when(pl.program_id(0) == 0)
    def _():
        bucket = bucket_ref[...]
        for kvh in range(SWA_KV_HEADS):
            for gg in range(g):
                h = kvh * g + gg
                b = jnp.zeros((w, w), F32)
                for r in range(REL_BUCKETS):
                    b = jnp.where(bucket == r, relb_ref[r, h] * LOG2E, b)
                bias_ref[kvh, :, gg * w:(gg + 1) * w] = b
                sinkrow_ref[kvh, :, gg * w:(gg + 1) * w] = jnp.full((1, w), sink_ref[h] * LOG2E, F32)

    chains = s_ref.shape[0]
    hu = s_ref.shape[2] // w
    row = lax.broadcasted_iota(jnp.int32, (w, w), 0)
    qry = lax.broadcasted_iota(jnp.int32, (w, w), 1)
    from_prev = jnp.concatenate([row > qry] * hu, axis=1)

    def key_rows(n):
        return (0, w) if n == 0 else ((n - 1) * w, 2 * w)

    def heads(kvh, part):
        return [kvh * g + part * hu + i for i in range(hu)]

    def cols(part):
        return slice(part * hu * w, (part + 1) * hu * w)

    def issue(n, kvh, part):
        qt = jnp.concatenate([q_ref[n, h * dh:(h + 1) * dh, :] for h in heads(kvh, part)], axis=1)
        start, size = key_rows(n)
        k = k_ref[start:start + size, kvh * LANES:kvh * LANES + dh]
        return jnp.dot(k, qt, preferred_element_type=F32)

    def stash(sc, c, n, kvh, part):
        bias = bias_ref[kvh, :, cols(part)]
        if n == 0:
            s = jnp.where(from_prev, NEG, sc + bias)
        else:
            s = jnp.where(from_prev, sc[0:w], sc[w:2 * w]) + bias
        s_ref[c] = s
        return jnp.max(s, axis=0, keepdims=True)

    def softmax(tile_max, c, n, kvh, part):
        sink = sinkrow_ref[kvh, :, cols(part)]
        m = jnp.maximum(tile_max, sink)
        p = jnp.exp2(s_ref[c] - m)
        if n == 0:
            p_ref[c, w:2 * w, :] = p.astype(BF16)
        else:
            p_ref[c, 0:w, :] = jnp.where(from_prev, p, 0.0).astype(BF16)
            p_ref[c, w:2 * w, :] = jnp.where(from_prev, 0.0, p).astype(BF16)
        return jnp.exp2(sink - m)

    def fold_values(sink_term, c, n, kvh, part):
        vrows = slice(kvh * vr, (kvh + 1) * vr)
        if n == 0:
            vwin_t = v_ref[0, vrows, :]
        else:
            vwin_t = jnp.concatenate([v_ref[n - 1, vrows, :], v_ref[n, vrows, :]], axis=1)
        lo = 2 * w - key_rows(n)[1]
        pv = jnp.dot(vwin_t, p_ref[c, lo:, :], preferred_element_type=F32)
        o = pv[0:dh] / (pv[dh:dh + 1] + sink_term)
        for i, h in enumerate(heads(kvh, part)):
            o_ref[n, h * dh:(h + 1) * dh, :] = o[:, i * w:(i + 1) * w]

    units = [(n, kvh, part) for n in range(nb) for kvh in range(SWA_KV_HEADS) for part in range(g // hu)]
    steps = [units[i:i + chains] for i in range(0, len(units), chains)]
    tile_max = [stash(issue(*u), c, *u) for c, u in enumerate(steps[0])]
    sink_term = [None] * chains
    for idx, step in enumerate(steps):
        nxt = steps[idx + 1] if idx + 1 < len(steps) else None
        if nxt is not None:
            sc = [issue(*u) for u in nxt]
        if idx > 0:
            for c, u in enumerate(steps[idx - 1]):
                fold_values(sink_term[c], c, *u)
        for c, u in enumerate(step):
            sink_term[c] = softmax(tile_max[c], c, *u)
        if nxt is not None:
            for c, u in enumerate(nxt):
                tile_max[c] = stash(sc[c], c, *u)
    for c, u in enumerate(steps[-1]):
        fold_values(sink_term[c], c, *u)


def _attn_features(om_ref, os_ref, gom_ref, gos_ref):
    om = jnp.concatenate([om_ref[t].T for t in range(om_ref.shape[0])], axis=0)
    osw = jnp.concatenate([os_ref[t].T for t in range(os_ref.shape[0])], axis=0)
    return _rms(om, gom_ref[...]).astype(BF16), _rms(osw, gos_ref[...]).astype(BF16)


def _out_ffn2_kernel(h1_ref, om_next_ref, os_next_ref, om_first_ref, os_first_ref, gom_ref, gos_ref, wom_ref,
                     wos_ref, g2_ref, wg_ref, wu_ref, wd_ref, gfin_ref, out_ref, am_ref, as_ref, h3_ref):
    s = pl.program_id(0)
    n_tiles = pl.num_programs(0) - 1

    @pl.when(s == 0)
    def _():
        am_ref[...], as_ref[...] = _attn_features(om_first_ref, os_first_ref, gom_ref, gos_ref)
        h3_ref[...] = jnp.zeros_like(h3_ref)

    @pl.when(s < n_tiles)
    def _():
        h2 = (h1_ref[...] + jnp.dot(am_ref[...], wom_ref[...], preferred_element_type=F32)
              + jnp.dot(as_ref[...], wos_ref[...], preferred_element_type=F32))
        out_ref[...] = _rms(h3_ref[...], gfin_ref[...])
        am_ref[...], as_ref[...] = _attn_features(om_next_ref, os_next_ref, gom_ref, gos_ref)
        hn = _rms(h2, g2_ref[...]).astype(BF16)
        h3_ref[...] = h2 + 0.5 * _swiglu(hn, wg_ref, wu_ref, wd_ref)

    @pl.when(s == n_tiles)
    def _():
        out_ref[...] = _rms(h3_ref[...], gfin_ref[...])


def _pack_w_in(w_in):
    o_kpe = Q_LORA + KV_LORA
    o_qs = o_kpe + QK_ROPE
    kpe = jnp.pad(w_in[:, o_kpe:o_qs], ((0, 0), (ROPE_LO, LANES - ROPE_LO - QK_ROPE)))
    return jnp.concatenate([w_in[:, :o_kpe], kpe, w_in[:, o_qs:]], axis=1)


def _pack_w_q_b_t(w_q_b):
    r = w_q_b.shape[0]
    w = w_q_b.reshape(r, MLA_HEADS, QK_NOPE + QK_ROPE)
    w = jnp.pad(w, ((0, 0), (0, 0), (0, LANES - QK_NOPE - QK_ROPE)))
    return w.reshape(r, MLA_HEADS * LANES).T


def _pack_w_kv_b(w_kv_b):
    r = w_kv_b.shape[0]
    w = w_kv_b.reshape(r, MLA_HEADS, QK_NOPE + V_HEAD)
    wk = jnp.pad(w[:, :, :QK_NOPE], ((0, 0), (0, 0), (0, LANES - QK_NOPE))).reshape(r, MLA_HEADS * LANES)
    wv_t = w[:, :, QK_NOPE:].reshape(r, MLA_HEADS * V_HEAD).T
    return wk, wv_t


def _rope_angles(seq):
    pos = jnp.arange(seq, dtype=F32)
    inv_freq = ROPE_THETA ** (-jnp.arange(0, QK_ROPE, 2, dtype=F32) / QK_ROPE)
    ang = pos[:, None] * inv_freq[None, :]
    return jnp.cos(ang), jnp.sin(ang)


def _rope_table_lanes(seq):
    cos, sin = _rope_angles(seq)
    z = lambda n: jnp.zeros((seq, n), F32)
    hi = LANES - ROPE_LO - QK_ROPE
    c = [z(ROPE_LO), cos, cos, z(hi)]
    sa = [z(ROPE_LO), -sin, z(ROPE_HALF + hi)]
    sb = [z(ROPE_LO + ROPE_HALF), sin, z(hi)]
    return jnp.concatenate(c + sa + sb, axis=1)


def _t5_bucket(dist):
    n = jnp.maximum(dist, 0)
    max_exact = REL_BUCKETS // 2
    nf = jnp.maximum(n, 1).astype(F32)
    large = max_exact + (jnp.log(nf / max_exact) / math.log(REL_MAX_DIST / max_exact)
                         * (REL_BUCKETS - max_exact)).astype(jnp.int32)
    large = jnp.minimum(large, REL_BUCKETS - 1)
    return jnp.where(n < max_exact, n, large)


def _const_spec(shape):
    nd = len(shape)
    return pl.BlockSpec(shape, lambda *_: (0,) * nd, pipeline_mode=pl.Buffered(1))


def kernel(x, g_ffn1, w_ffn1_gate, w_ffn1_up, w_ffn1_down, g_mix, w_in, g_q_a, w_q_b, g_kv_a, w_kv_b,
           attn_sinks, rel_bias, g_out_mla, g_out_swa, w_o, g_ffn2, w_ffn2_gate, w_ffn2_up, w_ffn2_down,
           g_final):
    bsz, seq, d = x.shape
    n_tok = bsz * seq
    tm = TOKEN_TILE
    assert d == D_MODEL and seq % tm == 0 and tm % MLA_TILE == 0 and tm % WINDOW == 0
    assert g_ffn1.shape[0] == 1, "single layer"

    row = lambda a: a.reshape(1, -1).astype(F32)
    xf = x.reshape(n_tok, d)

    win = _pack_w_in(w_in[0]).astype(BF16)
    wqbt = _pack_w_q_b_t(w_q_b[0]).astype(BF16)
    wkb, wvbt = _pack_w_kv_b(w_kv_b[0])
    wkb, wvbt = wkb.astype(BF16), wvbt.astype(BF16)
    q_scale = (QK_NOPE + QK_ROPE) ** -0.5 * LOG2E
    cos, sin = _rope_angles(seq)
    rope_q = jnp.concatenate([cos.T, sin.T], axis=0) * q_scale
    rope_k = _rope_table_lanes(seq)

    tiles = n_tok // tm
    tps = seq // tm
    nt_mla, nb_swa = seq // MLA_TILE, seq // WINDOW

    weights1 = [row(g_ffn1[0]), w_ffn1_gate[0].astype(BF16), w_ffn1_up[0].astype(BF16),
                w_ffn1_down[0].astype(BF16), row(g_mix[0]), win, row(g_q_a[0]), wqbt, row(g_kv_a[0]), wkb, wvbt]
    last = tiles - 1
    tile_spec = lambda width: pl.BlockSpec((tm, width), lambda i: (i, 0))
    fm_spec = lambda rows, tok: pl.BlockSpec((None, tm // tok, rows, tok), lambda i: (i // tps, i % tps, 0, 0))
    h1, q_mla, k_mla, v_mla, q_swa, k_swa, v_swa = pl.pallas_call(
        functools.partial(_ffn1_proj_kernel, q_scale=q_scale),
        grid=(tiles,),
        in_specs=[tile_spec(d)] + [_const_spec(a.shape) for a in weights1]
                 + [pl.BlockSpec((QK_ROPE, tm), lambda i: (0, i % tps)),
                    pl.BlockSpec((tm, 3 * LANES), lambda i: (i % tps, 0))],
        out_specs=[tile_spec(d), fm_spec(MLA_HEADS * LANES, MLA_TILE), tile_spec(MLA_HEADS * LANES),
                   fm_spec(MLA_HEADS * MLA_V_ROWS, MLA_TILE), fm_spec(SWA_OUT, WINDOW),
                   tile_spec(SWA_KV_HEADS * LANES),
                   fm_spec(SWA_KV_HEADS * SWA_V_ROWS, WINDOW)],
        out_shape=[jax.ShapeDtypeStruct((n_tok, d), F32),
                   jax.ShapeDtypeStruct((bsz, nt_mla, MLA_HEADS * LANES, MLA_TILE), BF16),
                   jax.ShapeDtypeStruct((n_tok, MLA_HEADS * LANES), BF16),
                   jax.ShapeDtypeStruct((bsz, nt_mla, MLA_HEADS * MLA_V_ROWS, MLA_TILE), BF16),
                   jax.ShapeDtypeStruct((bsz, nb_swa, SWA_OUT, WINDOW), BF16),
                   jax.ShapeDtypeStruct((n_tok, SWA_KV_HEADS * LANES), BF16),
                   jax.ShapeDtypeStruct((bsz, nb_swa, SWA_KV_HEADS * SWA_V_ROWS, WINDOW), BF16)],
        compiler_params=pltpu.CompilerParams(dimension_semantics=("arbitrary",),
                                             vmem_limit_bytes=VMEM_LIMIT),
        name="ffn1_proj",
    )(xf, *weights1, rope_q, rope_k)

    hp = MLA_HEADS_PER_STEP
    o_mla = pl.pallas_call(
        _mla_kernel,
        grid=(bsz, MLA_HEADS // hp),
        in_specs=[pl.BlockSpec((None, nt_mla, hp * LANES, MLA_TILE), lambda b, p: (b, 0, p, 0)),
                  pl.BlockSpec((seq, hp * LANES), lambda b, p: (b, p)),
                  pl.BlockSpec((None, nt_mla, hp * MLA_V_ROWS, MLA_TILE), lambda b, p: (b, 0, p, 0))],
        out_specs=pl.BlockSpec((None, nt_mla, hp * V_HEAD, MLA_TILE), lambda b, p: (b, 0, p, 0)),
        out_shape=jax.ShapeDtypeStruct((bsz, nt_mla, MLA_OUT, MLA_TILE), F32),
        scratch_shapes=[pltpu.VMEM((MLA_SCORE_SLOTS, hp, MLA_TILE, MLA_TILE), F32),
                        pltpu.VMEM((hp, MLA_TILE, MLA_TILE), BF16)],
        compiler_params=pltpu.CompilerParams(dimension_semantics=("arbitrary", "arbitrary"),
                                             vmem_limit_bytes=VMEM_LIMIT),
        name="mla",
    )(q_mla, k_mla, v_mla)

    kj = jnp.arange(WINDOW)[:, None]
    qi = jnp.arange(WINDOW)[None, :]
    bucket_t = _t5_bucket(jnp.where(kj > qi, qi + WINDOW - kj, qi - kj)).astype(jnp.int32)
    smem = pl.BlockSpec(memory_space=pltpu.SMEM)
    o_swa = pl.pallas_call(
        _swa_kernel,
        grid=(bsz,),
        in_specs=[_const_spec(bucket_t.shape), smem, smem,
                  pl.BlockSpec((None, nb_swa, SWA_OUT, WINDOW), lambda b: (b, 0, 0, 0)),
                  pl.BlockSpec((seq, SWA_KV_HEADS * LANES), lambda b: (b, 0)),
                  pl.BlockSpec((None, nb_swa, SWA_KV_HEADS * SWA_V_ROWS, WINDOW), lambda b: (b, 0, 0, 0))],
        out_specs=pl.BlockSpec((None, nb_swa, SWA_OUT, WINDOW), lambda b: (b, 0, 0, 0)),
        out_shape=jax.ShapeDtypeStruct((bsz, nb_swa, SWA_OUT, WINDOW), F32),
        scratch_shapes=[pltpu.VMEM((SWA_KV_HEADS, WINDOW, SWA_GROUP * WINDOW), F32),
                        pltpu.VMEM((SWA_KV_HEADS, 1, SWA_GROUP * WINDOW), F32),
                        pltpu.VMEM((SWA_UNITS_PER_STEP, WINDOW, SWA_HEADS_PER_UNIT * WINDOW), F32),
                        pltpu.VMEM((SWA_UNITS_PER_STEP, 2 * WINDOW, SWA_HEADS_PER_UNIT * WINDOW), BF16)],
        compiler_params=pltpu.CompilerParams(dimension_semantics=("arbitrary",),
                                             vmem_limit_bytes=VMEM_LIMIT),
        name="swa",
    )(bucket_t, rel_bias.astype(F32), attn_sinks[0].astype(F32), q_swa, k_swa, v_swa)

    wo = w_o[0].astype(BF16)
    weights4 = [row(g_out_mla[0]), row(g_out_swa[0]), wo[:MLA_OUT], wo[MLA_OUT:], row(g_ffn2[0]),
                w_ffn2_gate[0].astype(BF16), w_ffn2_up[0].astype(BF16), w_ffn2_down[0].astype(BF16),
                row(g_final)]
    nxt = lambda s: jnp.minimum(s + 1, last)
    fm_next = lambda rows, tok: pl.BlockSpec(
        (None, tm // tok, rows, tok), lambda s: (nxt(s) // tps, nxt(s) % tps, 0, 0))
    fm_first = lambda rows, tok: pl.BlockSpec((None, tm // tok, rows, tok), lambda s: (0, 0, 0, 0),
                                              pipeline_mode=pl.Buffered(1))
    out = pl.pallas_call(
        _out_ffn2_kernel,
        grid=(tiles + 1,),
        in_specs=[pl.BlockSpec((tm, d), lambda s: (jnp.minimum(s, last), 0)),
                  fm_next(MLA_OUT, MLA_TILE), fm_next(SWA_OUT, WINDOW),
                  fm_first(MLA_OUT, MLA_TILE), fm_first(SWA_OUT, WINDOW)]
                 + [_const_spec(a.shape) for a in weights4],
        out_specs=pl.BlockSpec((tm, d), lambda s: (jnp.maximum(s - 1, 0), 0)),
        out_shape=jax.ShapeDtypeStruct((n_tok, d), F32),
        scratch_shapes=[pltpu.VMEM((tm, MLA_OUT), BF16), pltpu.VMEM((tm, SWA_OUT), BF16),
                        pltpu.VMEM((tm, d), F32)],
        compiler_params=pltpu.CompilerParams(dimension_semantics=("arbitrary",),
                                             vmem_limit_bytes=VMEM_LIMIT),
        name="out_ffn2",
    )(h1, o_mla, o_swa, o_mla, o_swa, *weights4)
    return out.reshape(bsz, seq, d)
```

```python
import functools
import math

import jax
import jax.numpy as jnp
from jax import lax
from jax.experimental import pallas as pl
from jax.experimental.pallas import tpu as pltpu

F32 = jnp.float32
BF16 = jnp.bfloat16

EPS = 1e-6
NEG = -1e30
LOG2E = math.log2(math.e)

D_MODEL = 1024
D_FF = 2816
MLA_HEADS = 8
Q_LORA = 256
KV_LORA = 128
QK_NOPE = 64
QK_ROPE = 32
V_HEAD = 64
ROPE_THETA = 10000.0
SWA_HEADS = 8
SWA_KV_HEADS = 2
SWA_HEAD_DIM = 64
SWA_GROUP = SWA_HEADS // SWA_KV_HEADS
WINDOW = 128
REL_BUCKETS = 32
REL_MAX_DIST = 128
MLA_OUT = MLA_HEADS * V_HEAD
SWA_OUT = SWA_HEADS * SWA_HEAD_DIM
SWA_KV = SWA_KV_HEADS * SWA_HEAD_DIM

LANES = 128
ROPE_LO = QK_NOPE
ROPE_HALF = QK_ROPE // 2

P_CQ = 0
P_CKV = P_CQ + Q_LORA
P_KPE = P_CKV + KV_LORA
P_QS = P_KPE + LANES
P_KS = P_QS + SWA_OUT
P_VS = P_KS + SWA_KV
P_END = P_VS + SWA_KV

TOKEN_TILE = 512
MLA_TILE = 256
MLA_HEADS_PER_STEP = 2
SWA_HEADS_PER_UNIT = 4
SWA_UNITS_PER_STEP = 2
ONES_ROWS = 16
MLA_V_ROWS = V_HEAD + ONES_ROWS
SWA_V_ROWS = SWA_HEAD_DIM + ONES_ROWS
VMEM_LIMIT = 56 * 1024 * 1024

_NT = (((1,), (1,)), ((), ()))


def _rms(x, g):
    return x * lax.rsqrt(jnp.mean(x * x, axis=-1, keepdims=True) + EPS) * g


def _swiglu(xn, wg_ref, wu_ref, wd_ref):
    gate = jnp.dot(xn, wg_ref[...], preferred_element_type=F32)
    up = jnp.dot(xn, wu_ref[...], preferred_element_type=F32)
    act = (gate * jax.nn.sigmoid(gate) * up).astype(BF16)
    return jnp.dot(act, wd_ref[...], preferred_element_type=F32)


def _rope_lanes(x, tab_ref):
    c = tab_ref[:, 0:LANES]
    sa = tab_ref[:, LANES:2 * LANES]
    sb = tab_ref[:, 2 * LANES:3 * LANES]
    return (x * c + pltpu.roll(x, LANES - ROPE_HALF, 1) * sa + pltpu.roll(x, ROPE_HALF, 1) * sb)


def _ffn1_proj_kernel(x_ref, xnext_ref, g1_ref, wg_ref, wu_ref, wd_ref, gmix_ref, win_ref, gqa_ref, wqbt_ref,
                      gkva_ref, wkb_ref, wvbt_ref, ropeq_ref, ropek_ref,
                      h1_ref, qm_ref, km_ref, vm_ref, qs_ref, ks_ref, vs_ref, xn_ref, h1s_ref, *, q_scale):
    s = pl.program_id(0)
    n_tiles = pl.num_programs(0) - 1
    up_project = functools.partial(_up_projection_stage, gqa_ref, wqbt_ref, gkva_ref, wkb_ref, wvbt_ref,
                                   ropeq_ref, ropek_ref, qm_ref, km_ref, vm_ref, qs_ref, ks_ref, vs_ref,
                                   q_scale=q_scale)

    def project(h1):
        u = _rms(h1, gmix_ref[...]).astype(BF16)
        return jnp.dot(u, win_ref[...], preferred_element_type=F32)

    @pl.when(s == 0)
    def _():
        xn_ref[...] = _rms(x_ref[...], g1_ref[...]).astype(BF16)
        h1s_ref[...] = jnp.zeros_like(h1s_ref)

    @pl.when(s < n_tiles)
    def _():
        xn = xn_ref[...]
        gate = jnp.dot(xn, wg_ref[...], preferred_element_type=F32)
        proj = project(h1s_ref[...])
        up = jnp.dot(xn, wu_ref[...], preferred_element_type=F32)
        up_project(proj)
        act = (gate * jax.nn.sigmoid(gate) * up).astype(BF16)
        h1 = x_ref[...] + 0.5 * jnp.dot(act, wd_ref[...], preferred_element_type=F32)
        h1_ref[...] = h1
        h1s_ref[...] = h1
        xn_ref[...] = _rms(xnext_ref[...], g1_ref[...]).astype(BF16)

    @pl.when(s == n_tiles)
    def _():
        up_project(project(h1s_ref[...]))


def _up_projection_stage(gqa_ref, wqbt_ref, gkva_ref, wkb_ref, wvbt_ref, ropeq_ref, ropek_ref,
                         qm_ref, km_ref, vm_ref, qs_ref, ks_ref, vs_ref, proj, *, q_scale):
    tm = proj.shape[0]

    cq = _rms(proj[:, P_CQ:P_CQ + Q_LORA], gqa_ref[...]).astype(BF16)
    qt = lax.dot_general(wqbt_ref[...], cq, _NT, preferred_element_type=F32)
    cs = ropeq_ref[0:ROPE_HALF, :]
    sn = ropeq_ref[ROPE_HALF:QK_ROPE, :]
    for h in range(MLA_HEADS):
        r = h * LANES
        x1 = qt[r + ROPE_LO:r + ROPE_LO + ROPE_HALF]
        x2 = qt[r + ROPE_LO + ROPE_HALF:r + ROPE_LO + QK_ROPE]
        qh = jnp.concatenate([qt[r:r + QK_NOPE] * q_scale, x1 * cs - x2 * sn, x2 * cs + x1 * sn,
                              qt[r + ROPE_LO + QK_ROPE:r + LANES]], axis=0).astype(BF16)
        for t in range(tm // MLA_TILE):
            qm_ref[t, r:r + LANES, :] = qh[:, t * MLA_TILE:(t + 1) * MLA_TILE]

    ckv = _rms(proj[:, P_CKV:P_CKV + KV_LORA], gkva_ref[...]).astype(BF16)
    kn = jnp.dot(ckv, wkb_ref[...], preferred_element_type=F32)
    kpe = _rope_lanes(proj[:, P_KPE:P_KPE + LANES], ropek_ref)
    for h in range(MLA_HEADS):
        km_ref[:, h * LANES:(h + 1) * LANES] = (kn[:, h * LANES:(h + 1) * LANES] + kpe).astype(BF16)
    vt = lax.dot_general(wvbt_ref[...], ckv, _NT, preferred_element_type=F32).astype(BF16)
    ones = jnp.ones((ONES_ROWS, tm), BF16)
    vt = jnp.concatenate([blk for h in range(MLA_HEADS) for blk in (vt[h * V_HEAD:(h + 1) * V_HEAD], ones)],
                         axis=0)
    for t in range(tm // MLA_TILE):
        vm_ref[t] = vt[:, t * MLA_TILE:(t + 1) * MLA_TILE]

    qst = (proj[:, P_QS:P_QS + SWA_OUT] * (SWA_HEAD_DIM ** -0.5 * LOG2E)).T.astype(BF16)
    vst = proj[:, P_VS:P_VS + SWA_KV].T.astype(BF16)
    vst = jnp.concatenate([blk for h in range(SWA_KV_HEADS)
                           for blk in (vst[h * SWA_HEAD_DIM:(h + 1) * SWA_HEAD_DIM], ones)], axis=0)
    for t in range(tm // WINDOW):
        qs_ref[t] = qst[:, t * WINDOW:(t + 1) * WINDOW]
        vs_ref[t] = vst[:, t * WINDOW:(t + 1) * WINDOW]
    for h in range(SWA_KV_HEADS):
        ks_ref[:, h * LANES:h * LANES + SWA_HEAD_DIM] = (
            proj[:, P_KS + h * SWA_HEAD_DIM:P_KS + (h + 1) * SWA_HEAD_DIM].astype(BF16))
        ks_ref[:, h * LANES + SWA_HEAD_DIM:(h + 1) * LANES] = jnp.zeros((tm, LANES - SWA_HEAD_DIM), BF16)


def _mla_kernel(q_ref, k_ref, v_ref, o_ref, s_ref, p_ref):
    nt, rows, t = q_ref.shape
    hp = rows // LANES
    vr = v_ref.shape[1] // hp
    key = lax.broadcasted_iota(jnp.int32, (t, t), 0)
    qry = lax.broadcasted_iota(jnp.int32, (t, t), 1)
    causal = key <= qry

    def issue(i, j, hh):
        return jnp.dot(k_ref[j * t:(j + 1) * t, hh * LANES:(hh + 1) * LANES],
                       q_ref[i, hh * LANES:(hh + 1) * LANES, :], preferred_element_type=F32)

    def stash(s, masked, hh):
        if masked:
            s = jnp.where(causal, s, NEG)
        s_ref[hh] = s
        return jnp.max(s, axis=0, keepdims=True)

    def fold_values(jp, st, hh):
        m, acc, alpha = st
        pv = jnp.dot(v_ref[jp, hh * vr:(hh + 1) * vr, :], p_ref[hh], preferred_element_type=F32)
        return m, alpha * acc + pv, alpha

    def softmax(tile_max, st, hh):
        m, acc, _ = st
        m_new = jnp.maximum(m, tile_max)
        p_ref[hh] = jnp.exp2(s_ref[hh] - m_new).astype(BF16)
        return m_new, acc, jnp.exp2(m - m_new)

    def finish(i, st, hh):
        acc = st[1]
        o_ref[i, hh * V_HEAD:(hh + 1) * V_HEAD, :] = acc[0:V_HEAD] / acc[V_HEAD:V_HEAD + 1]

    fresh = (jnp.full((1, t), NEG, F32), jnp.zeros((vr, t), F32), jnp.zeros((1, t), F32))

    tiles = [(i, j) for i in range(nt) for j in range(i + 1)]
    tile_max = [stash(issue(0, 0, hh), True, hh) for hh in range(hp)]
    state = [fresh] * hp
    for n, (i, j) in enumerate(tiles):
        prev = tiles[n - 1] if n > 0 else None
        nxt = tiles[n + 1] if n + 1 < len(tiles) else None
        if nxt is not None:
            sc = [issue(*nxt, hh) for hh in range(hp)]
        for hh in range(hp):
            st = state[hh]
            if prev is not None:
                st = fold_values(prev[1], st, hh)
                if prev[0] != i:
                    finish(prev[0], st, hh)
                    st = fresh
            state[hh] = softmax(tile_max[hh], st, hh)
            if nxt is not None:
                tile_max[hh] = stash(sc[hh], nxt[0] == nxt[1], hh)
    for hh in range(hp):
        finish(nt - 1, fold_values(nt - 1, state[hh], hh), hh)


def _swa_kernel(bucket_ref, relb_ref, sink_ref, q_ref, k_ref, v_ref, o_ref, bias_ref, sinkrow_ref,
                s_ref, p_ref):
    w, dh, g = WINDOW, SWA_HEAD_DIM, SWA_GROUP
    nb = q_ref.shape[0]
    vr = v_ref.shape[1] // SWA_KV_HEADS

    @pl.when(pl.program_id(0) == 0)
    def _():
        bucket = bucket_ref[...]
        for kvh in range(SWA_KV_HEADS):
            for gg in range(g):
                h = kvh * g + gg
                b = jnp.zeros((w, w), F32)
                for r in range(REL_BUCKETS):
                    b = jnp.where(bucket == r, relb_ref[r, h] * LOG2E, b)
                bias_ref[kvh, :, gg * w:(gg + 1) * w] = b
                sinkrow_ref[kvh, :, gg * w:(gg + 1) * w] = jnp.full((1, w), sink_ref[h] * LOG2E, F32)

    chains = s_ref.shape[0]
    hu = s_ref.shape[2] // w
    row = lax.broadcasted_iota(jnp.int32, (w, w), 0)
    qry = lax.broadcasted_iota(jnp.int32, (w, w), 1)
    from_prev = jnp.concatenate([row > qry] * hu, axis=1)

    def key_rows(n):
        return (0, w) if n == 0 else ((n - 1) * w, 2 * w)

    def heads(kvh, part):
        return [kvh * g + part * hu + i for i in range(hu)]

    def cols(part):
        return slice(part * hu * w, (part + 1) * hu * w)

    def issue(n, kvh, part):
        qt = jnp.concatenate([q_ref[n, h * dh:(h + 1) * dh, :] for h in heads(kvh, part)], axis=1)
        start, size = key_rows(n)
        k = k_ref[start:start + size, kvh * LANES:kvh * LANES + dh]
        return jnp.dot(k, qt, preferred_element_type=F32)

    def stash(sc, c, n, kvh, part):
        bias = bias_ref[kvh, :, cols(part)]
        if n == 0:
            s = jnp.where(from_prev, NEG, sc + bias)
        else:
            s = jnp.where(from_prev, sc[0:w], sc[w:2 * w]) + bias
        s_ref[c] = s
        return jnp.max(s, axis=0, keepdims=True)

    def softmax(tile_max, c, n, kvh, part):
        sink = sinkrow_ref[kvh, :, cols(part)]
        m = jnp.maximum(tile_max, sink)
        p = jnp.exp2(s_ref[c] - m)
        if n == 0:
            p_ref[c, w:2 * w, :] = p.astype(BF16)
        else:
            p_ref[c, 0:w, :] = jnp.where(from_prev, p, 0.0).astype(BF16)
            p_ref[c, w:2 * w, :] = jnp.where(from_prev, 0.0, p).astype(BF16)
        return jnp.exp2(sink - m)

    def fold_values(sink_term, c, n, kvh, part):
        vrows = slice(kvh * vr, (kvh + 1) * vr)
        if n == 0:
            vwin_t = v_ref[0, vrows, :]
        else:
            vwin_t = jnp.concatenate([v_ref[n - 1, vrows, :], v_ref[n, vrows, :]], axis=1)
        lo = 2 * w - key_rows(n)[1]
        pv = jnp.dot(vwin_t, p_ref[c, lo:, :], preferred_element_type=F32)
        o = pv[0:dh] / (pv[dh:dh + 1] + sink_term)
        for i, h in enumerate(heads(kvh, part)):
            o_ref[n, h * dh:(h + 1) * dh, :] = o[:, i * w:(i + 1) * w]

    units = [(n, kvh, part) for n in range(nb) for kvh in range(SWA_KV_HEADS) for part in range(g // hu)]
    steps = [units[i:i + chains] for i in range(0, len(units), chains)]
    tile_max = [stash(issue(*u), c, *u) for c, u in enumerate(steps[0])]
    sink_term = [None] * chains
    for idx, step in enumerate(steps):
        nxt = steps[idx + 1] if idx + 1 < len(steps) else None
        if nxt is not None:
            sc = [issue(*u) for u in nxt]
        if idx > 0:
            for c, u in enumerate(steps[idx - 1]):
                fold_values(sink_term[c], c, *u)
        for c, u in enumerate(step):
            sink_term[c] = softmax(tile_max[c], c, *u)
        if nxt is not None:
            for c, u in enumerate(nxt):
                tile_max[c] = stash(sc[c], c, *u)
    for c, u in enumerate(steps[-1]):
        fold_values(sink_term[c], c, *u)


def _attn_features(om_ref, os_ref, gom_ref, gos_ref):
    om = jnp.concatenate([om_ref[t].T for t in range(om_ref.shape[0])], axis=0)
    osw = jnp.concatenate([os_ref[t].T for t in range(os_ref.shape[0])], axis=0)
    return _rms(om, gom_ref[...]).astype(BF16), _rms(osw, gos_ref[...]).astype(BF16)


def _out_ffn2_kernel(h1_ref, om_next_ref, os_next_ref, om_first_ref, os_first_ref, gom_ref, gos_ref, wom_ref,
                     wos_ref, g2_ref, wg_ref, wu_ref, wd_ref, gfin_ref, out_ref, am_ref, as_ref, h3_ref):
    s = pl.program_id(0)
    n_tiles = pl.num_programs(0) - 1

    @pl.when(s == 0)
    def _():
        am_ref[...], as_ref[...] = _attn_features(om_first_ref, os_first_ref, gom_ref, gos_ref)
        h3_ref[...] = jnp.zeros_like(h3_ref)

    @pl.when(s < n_tiles)
    def _():
        h2 = (h1_ref[...] + jnp.dot(am_ref[...], wom_ref[...], preferred_element_type=F32)
              + jnp.dot(as_ref[...], wos_ref[...], preferred_element_type=F32))
        out_ref[...] = _rms(h3_ref[...], gfin_ref[...])
        am_ref[...], as_ref[...] = _attn_features(om_next_ref, os_next_ref, gom_ref, gos_ref)
        hn = _rms(h2, g2_ref[...]).astype(BF16)
        h3_ref[...] = h2 + 0.5 * _swiglu(hn, wg_ref, wu_ref, wd_ref)

    @pl.when(s == n_tiles)
    def _():
        out_ref[...] = _rms(h3_ref[...], gfin_ref[...])


def _pack_w_in(w_in):
    o_kpe = Q_LORA + KV_LORA
    o_qs = o_kpe + QK_ROPE
    kpe = jnp.pad(w_in[:, o_kpe:o_qs], ((0, 0), (ROPE_LO, LANES - ROPE_LO - QK_ROPE)))
    return jnp.concatenate([w_in[:, :o_kpe], kpe, w_in[:, o_qs:]], axis=1)


def _pack_w_q_b_t(w_q_b):
    r = w_q_b.shape[0]
    w = w_q_b.reshape(r, MLA_HEADS, QK_NOPE + QK_ROPE)
    w = jnp.pad(w, ((0, 0), (0, 0), (0, LANES - QK_NOPE - QK_ROPE)))
    return w.reshape(r, MLA_HEADS * LANES).T


def _pack_w_kv_b(w_kv_b):
    r = w_kv_b.shape[0]
    w = w_kv_b.reshape(r, MLA_HEADS, QK_NOPE + V_HEAD)
    wk = jnp.pad(w[:, :, :QK_NOPE], ((0, 0), (0, 0), (0, LANES - QK_NOPE))).reshape(r, MLA_HEADS * LANES)
    wv_t = w[:, :, QK_NOPE:].reshape(r, MLA_HEADS * V_HEAD).T
    return wk, wv_t


def _rope_angles(seq):
    pos = jnp.arange(seq, dtype=F32)
    inv_freq = ROPE_THETA ** (-jnp.arange(0, QK_ROPE, 2, dtype=F32) / QK_ROPE)
    ang = pos[:, None] * inv_freq[None, :]
    return jnp.cos(ang), jnp.sin(ang)


def _rope_table_lanes(seq):
    cos, sin = _rope_angles(seq)
    z = lambda n: jnp.zeros((seq, n), F32)
    hi = LANES - ROPE_LO - QK_ROPE
    c = [z(ROPE_LO), cos, cos, z(hi)]
    sa = [z(ROPE_LO), -sin, z(ROPE_HALF + hi)]
    sb = [z(ROPE_LO + ROPE_HALF), sin, z(hi)]
    return jnp.concatenate(c + sa + sb, axis=1)


def _t5_bucket(dist):
    n = jnp.maximum(dist, 0)
    max_exact = REL_BUCKETS // 2
    nf = jnp.maximum(n, 1).astype(F32)
    large = max_exact + (jnp.log(nf / max_exact) / math.log(REL_MAX_DIST / max_exact)
                         * (REL_BUCKETS - max_exact)).astype(jnp.int32)
    large = jnp.minimum(large, REL_BUCKETS - 1)
    return jnp.where(n < max_exact, n, large)


def _const_spec(shape):
    nd = len(shape)
    return pl.BlockSpec(shape, lambda *_: (0,) * nd, pipeline_mode=pl.Buffered(1))


def kernel(x, g_ffn1, w_ffn1_gate, w_ffn1_up, w_ffn1_down, g_mix, w_in, g_q_a, w_q_b, g_kv_a, w_kv_b,
           attn_sinks, rel_bias, g_out_mla, g_out_swa, w_o, g_ffn2, w_ffn2_gate, w_ffn2_up, w_ffn2_down,
           g_final):
    bsz, seq, d = x.shape
    n_tok = bsz * seq
    tm = TOKEN_TILE
    assert d == D_MODEL and seq % tm == 0 and tm % MLA_TILE == 0 and tm % WINDOW == 0
    assert g_ffn1.shape[0] == 1, "single layer"

    row = lambda a: a.reshape(1, -1).astype(F32)
    xf = x.reshape(n_tok, d)

    win = _pack_w_in(w_in[0]).astype(BF16)
    wqbt = _pack_w_q_b_t(w_q_b[0]).astype(BF16)
    wkb, wvbt = _pack_w_kv_b(w_kv_b[0])
    wkb, wvbt = wkb.astype(BF16), wvbt.astype(BF16)
    q_scale = (QK_NOPE + QK_ROPE) ** -0.5 * LOG2E
    cos, sin = _rope_angles(seq)
    rope_q = jnp.concatenate([cos.T, sin.T], axis=0) * q_scale
    rope_k = _rope_table_lanes(seq)

    tiles = n_tok // tm
    tps = seq // tm
    nt_mla, nb_swa = seq // MLA_TILE, seq // WINDOW

    weights1 = [row(g_ffn1[0]), w_ffn1_gate[0].astype(BF16), w_ffn1_up[0].astype(BF16),
                w_ffn1_down[0].astype(BF16), row(g_mix[0]), win, row(g_q_a[0]), wqbt, row(g_kv_a[0]), wkb, wvbt]
    last = tiles - 1
    cur = lambda s: jnp.minimum(s, last)
    prv = lambda s: jnp.maximum(s - 1, 0)
    tok_spec = lambda width, tile_of: pl.BlockSpec((tm, width), lambda s: (tile_of(s), 0))
    fm_spec = lambda rows, tok: pl.BlockSpec(
        (None, tm // tok, rows, tok), lambda s: (prv(s) // tps, prv(s) % tps, 0, 0))
    h1, q_mla, k_mla, v_mla, q_swa, k_swa, v_swa = pl.pallas_call(
        functools.partial(_ffn1_proj_kernel, q_scale=q_scale),
        grid=(tiles + 1,),
        in_specs=[tok_spec(d, cur), tok_spec(d, lambda s: jnp.minimum(s + 1, last))]
                 + [_const_spec(a.shape) for a in weights1]
                 + [pl.BlockSpec((QK_ROPE, tm), lambda s: (0, prv(s) % tps)),
                    pl.BlockSpec((tm, 3 * LANES), lambda s: (prv(s) % tps, 0))],
        out_specs=[tok_spec(d, cur), fm_spec(MLA_HEADS * LANES, MLA_TILE), tok_spec(MLA_HEADS * LANES, prv),
                   fm_spec(MLA_HEADS * MLA_V_ROWS, MLA_TILE), fm_spec(SWA_OUT, WINDOW),
                   tok_spec(SWA_KV_HEADS * LANES, prv),
                   fm_spec(SWA_KV_HEADS * SWA_V_ROWS, WINDOW)],
        scratch_shapes=[pltpu.VMEM((tm, d), BF16), pltpu.VMEM((tm, d), F32)],
        out_shape=[jax.ShapeDtypeStruct((n_tok, d), F32),
                   jax.ShapeDtypeStruct((bsz, nt_mla, MLA_HEADS * LANES, MLA_TILE), BF16),
                   jax.ShapeDtypeStruct((n_tok, MLA_HEADS * LANES), BF16),
                   jax.ShapeDtypeStruct((bsz, nt_mla, MLA_HEADS * MLA_V_ROWS, MLA_TILE), BF16),
                   jax.ShapeDtypeStruct((bsz, nb_swa, SWA_OUT, WINDOW), BF16),
                   jax.ShapeDtypeStruct((n_tok, SWA_KV_HEADS * LANES), BF16),
                   jax.ShapeDtypeStruct((bsz, nb_swa, SWA_KV_HEADS * SWA_V_ROWS, WINDOW), BF16)],
        compiler_params=pltpu.CompilerParams(dimension_semantics=("arbitrary",),
                                             vmem_limit_bytes=VMEM_LIMIT),
        name="ffn1_proj",
    )(xf, xf, *weights1, rope_q, rope_k)

    hp = MLA_HEADS_PER_STEP
    o_mla = pl.pallas_call(
        _mla_kernel,
        grid=(bsz, MLA_HEADS // hp),
        in_specs=[pl.BlockSpec((None, nt_mla, hp * LANES, MLA_TILE), lambda b, p: (b, 0, p, 0)),
                  pl.BlockSpec((seq, hp * LANES), lambda b, p: (b, p)),
                  pl.BlockSpec((None, nt_mla, hp * MLA_V_ROWS, MLA_TILE), lambda b, p: (b, 0, p, 0))],
        out_specs=pl.BlockSpec((None, nt_mla, hp * V_HEAD, MLA_TILE), lambda b, p: (b, 0, p, 0)),
        out_shape=jax.ShapeDtypeStruct((bsz, nt_mla, MLA_OUT, MLA_TILE), F32),
        scratch_shapes=[pltpu.VMEM((hp, MLA_TILE, MLA_TILE), F32), pltpu.VMEM((hp, MLA_TILE, MLA_TILE), BF16)],
        compiler_params=pltpu.CompilerParams(dimension_semantics=("arbitrary", "arbitrary"),
                                             vmem_limit_bytes=VMEM_LIMIT),
        name="mla",
    )(q_mla, k_mla, v_mla)

    kj = jnp.arange(WINDOW)[:, None]
    qi = jnp.arange(WINDOW)[None, :]
    bucket_t = _t5_bucket(jnp.where(kj > qi, qi + WINDOW - kj, qi - kj)).astype(jnp.int32)
    smem = pl.BlockSpec(memory_space=pltpu.SMEM)
    o_swa = pl.pallas_call(
        _swa_kernel,
        grid=(bsz,),
        in_specs=[_const_spec(bucket_t.shape), smem, smem,
                  pl.BlockSpec((None, nb_swa, SWA_OUT, WINDOW), lambda b: (b, 0, 0, 0)),
                  pl.BlockSpec((seq, SWA_KV_HEADS * LANES), lambda b: (b, 0)),
                  pl.BlockSpec((None, nb_swa, SWA_KV_HEADS * SWA_V_ROWS, WINDOW), lambda b: (b, 0, 0, 0))],
        out_specs=pl.BlockSpec((None, nb_swa, SWA_OUT, WINDOW), lambda b: (b, 0, 0, 0)),
        out_shape=jax.ShapeDtypeStruct((bsz, nb_swa, SWA_OUT, WINDOW), F32),
        scratch_shapes=[pltpu.VMEM((SWA_KV_HEADS, WINDOW, SWA_GROUP * WINDOW), F32),
                        pltpu.VMEM((SWA_KV_HEADS, 1, SWA_GROUP * WINDOW), F32),
                        pltpu.VMEM((SWA_UNITS_PER_STEP, WINDOW, SWA_HEADS_PER_UNIT * WINDOW), F32),
                        pltpu.VMEM((SWA_UNITS_PER_STEP, 2 * WINDOW, SWA_HEADS_PER_UNIT * WINDOW), BF16)],
        compiler_params=pltpu.CompilerParams(dimension_semantics=("arbitrary",),
                                             vmem_limit_bytes=VMEM_LIMIT),
        name="swa",
    )(bucket_t, rel_bias.astype(F32), attn_sinks[0].astype(F32), q_swa, k_swa, v_swa)

    wo = w_o[0].astype(BF16)
    weights4 = [row(g_out_mla[0]), row(g_out_swa[0]), wo[:MLA_OUT], wo[MLA_OUT:], row(g_ffn2[0]),
                w_ffn2_gate[0].astype(BF16), w_ffn2_up[0].astype(BF16), w_ffn2_down[0].astype(BF16),
                row(g_final)]
    nxt = lambda s: jnp.minimum(s + 1, last)
    fm_next = lambda rows, tok: pl.BlockSpec(
        (None, tm // tok, rows, tok), lambda s: (nxt(s) // tps, nxt(s) % tps, 0, 0))
    fm_first = lambda rows, tok: pl.BlockSpec((None, tm // tok, rows, tok), lambda s: (0, 0, 0, 0),
                                              pipeline_mode=pl.Buffered(1))
    out = pl.pallas_call(
        _out_ffn2_kernel,
        grid=(tiles + 1,),
        in_specs=[pl.BlockSpec((tm, d), lambda s: (jnp.minimum(s, last), 0)),
                  fm_next(MLA_OUT, MLA_TILE), fm_next(SWA_OUT, WINDOW),
                  fm_first(MLA_OUT, MLA_TILE), fm_first(SWA_OUT, WINDOW)]
                 + [_const_spec(a.shape) for a in weights4],
        out_specs=pl.BlockSpec((tm, d), lambda s: (jnp.maximum(s - 1, 0), 0)),
        out_shape=jax.ShapeDtypeStruct((n_tok, d), F32),
        scratch_shapes=[pltpu.VMEM((tm, MLA_OUT), BF16), pltpu.VMEM((tm, SWA_OUT), BF16),
                        pltpu.VMEM((tm, d), F32)],
        compiler_params=pltpu.CompilerParams(dimension_semantics=("arbitrary",),
                                             vmem_limit_bytes=VMEM_LIMIT),
        name="out_ffn2",
    )(h1, o_mla, o_swa, o_mla, o_swa, *weights4)
    return out.reshape(bsz, seq, d)
```

```python
import functools
import math

import jax
import jax.numpy as jnp
from jax import lax
from jax.experimental import pallas as pl
from jax.experimental.pallas import tpu as pltpu

F32 = jnp.float32
BF16 = jnp.bfloat16

EPS = 1e-6
NEG = -1e30
LOG2E = math.log2(math.e)

D_MODEL = 1024
D_FF = 2816
MLA_HEADS = 8
Q_LORA = 256
KV_LORA = 128
QK_NOPE = 64
QK_ROPE = 32
V_HEAD = 64
ROPE_THETA = 10000.0
SWA_HEADS = 8
SWA_KV_HEADS = 2
SWA_HEAD_DIM = 64
SWA_GROUP = SWA_HEADS // SWA_KV_HEADS
WINDOW = 128
REL_BUCKETS = 32
REL_MAX_DIST = 128
MLA_OUT = MLA_HEADS * V_HEAD
SWA_OUT = SWA_HEADS * SWA_HEAD_DIM
SWA_KV = SWA_KV_HEADS * SWA_HEAD_DIM

LANES = 128
ROPE_LO = QK_NOPE
ROPE_HALF = QK_ROPE // 2

P_CQ = 0
P_CKV = P_CQ + Q_LORA
P_KPE = P_CKV + KV_LORA
P_QS = P_KPE + LANES
P_KS = P_QS + SWA_OUT
P_VS = P_KS + SWA_KV
P_END = P_VS + SWA_KV

TOKEN_TILE = 512
MLA_TILE = 256
MLA_HEADS_PER_STEP = 2
SWA_HEADS_PER_UNIT = 4
SWA_UNITS_PER_STEP = 2
FF_TILE = 256
ONES_ROWS = 16
MLA_V_ROWS = V_HEAD + ONES_ROWS
SWA_V_ROWS = SWA_HEAD_DIM + ONES_ROWS
VMEM_LIMIT = 56 * 1024 * 1024

_NT = (((1,), (1,)), ((), ()))


def _rms(x, g):
    return x * lax.rsqrt(jnp.mean(x * x, axis=-1, keepdims=True) + EPS) * g


def _swiglu(xn, wgu_ref, wd_ref):
    gu = jnp.dot(xn, wgu_ref[...], preferred_element_type=F32)
    tiles = []
    for j in range(gu.shape[1] // (2 * FF_TILE)):
        gate = gu[:, 2 * j * FF_TILE:(2 * j + 1) * FF_TILE]
        up = gu[:, (2 * j + 1) * FF_TILE:(2 * j + 2) * FF_TILE]
        tiles.append((gate * jax.nn.sigmoid(gate) * up).astype(BF16))
    return jnp.dot(jnp.concatenate(tiles, axis=1), wd_ref[...], preferred_element_type=F32)


def _rope_lanes(x, tab_ref):
    c = tab_ref[:, 0:LANES]
    sa = tab_ref[:, LANES:2 * LANES]
    sb = tab_ref[:, 2 * LANES:3 * LANES]
    return (x * c + pltpu.roll(x, LANES - ROPE_HALF, 1) * sa + pltpu.roll(x, ROPE_HALF, 1) * sb)


def _ffn1_proj_kernel(x_ref, xnext_ref, g1_ref, wg_ref, wu_ref, wd_ref, gmix_ref, win_ref, gqa_ref, wqbt_ref,
                      gkva_ref, wkb_ref, wvbt_ref, ropeq_ref, ropek_ref,
                      h1_ref, qm_ref, km_ref, vm_ref, qs_ref, ks_ref, vs_ref, xn_ref, h1s_ref, *, q_scale):
    s = pl.program_id(0)
    n_tiles = pl.num_programs(0) - 1
    up_project = functools.partial(_up_projection_stage, gqa_ref, wqbt_ref, gkva_ref, wkb_ref, wvbt_ref,
                                   ropeq_ref, ropek_ref, qm_ref, km_ref, vm_ref, qs_ref, ks_ref, vs_ref,
                                   q_scale=q_scale)

    def project(h1):
        u = _rms(h1, gmix_ref[...]).astype(BF16)
        return jnp.dot(u, win_ref[...], preferred_element_type=F32)

    @pl.when(s == 0)
    def _():
        xn_ref[...] = _rms(x_ref[...], g1_ref[...]).astype(BF16)
        h1s_ref[...] = jnp.zeros_like(h1s_ref)

    @pl.when(s < n_tiles)
    def _():
        xn = xn_ref[...]
        gate = jnp.dot(xn, wg_ref[...], preferred_element_type=F32)
        proj = project(h1s_ref[...])
        up = jnp.dot(xn, wu_ref[...], preferred_element_type=F32)
        up_project(proj)
        act = (gate * jax.nn.sigmoid(gate) * up).astype(BF16)
        h1 = x_ref[...] + 0.5 * jnp.dot(act, wd_ref[...], preferred_element_type=F32)
        h1_ref[...] = h1
        h1s_ref[...] = h1
        xn_ref[...] = _rms(xnext_ref[...], g1_ref[...]).astype(BF16)

    @pl.when(s == n_tiles)
    def _():
        up_project(project(h1s_ref[...]))


def _up_projection_stage(gqa_ref, wqbt_ref, gkva_ref, wkb_ref, wvbt_ref, ropeq_ref, ropek_ref,
                         qm_ref, km_ref, vm_ref, qs_ref, ks_ref, vs_ref, proj, *, q_scale):
    tm = proj.shape[0]

    cq = _rms(proj[:, P_CQ:P_CQ + Q_LORA], gqa_ref[...]).astype(BF16)
    qt = lax.dot_general(wqbt_ref[...], cq, _NT, preferred_element_type=F32)
    cs = ropeq_ref[0:ROPE_HALF, :]
    sn = ropeq_ref[ROPE_HALF:QK_ROPE, :]
    for h in range(MLA_HEADS):
        r = h * LANES
        x1 = qt[r + ROPE_LO:r + ROPE_LO + ROPE_HALF]
        x2 = qt[r + ROPE_LO + ROPE_HALF:r + ROPE_LO + QK_ROPE]
        qh = jnp.concatenate([qt[r:r + QK_NOPE] * q_scale, x1 * cs - x2 * sn, x2 * cs + x1 * sn,
                              qt[r + ROPE_LO + QK_ROPE:r + LANES]], axis=0).astype(BF16)
        for t in range(tm // MLA_TILE):
            qm_ref[t, r:r + LANES, :] = qh[:, t * MLA_TILE:(t + 1) * MLA_TILE]

    ckv = _rms(proj[:, P_CKV:P_CKV + KV_LORA], gkva_ref[...]).astype(BF16)
    kn = jnp.dot(ckv, wkb_ref[...], preferred_element_type=F32)
    kpe = _rope_lanes(proj[:, P_KPE:P_KPE + LANES], ropek_ref)
    for h in range(MLA_HEADS):
        km_ref[:, h * LANES:(h + 1) * LANES] = (kn[:, h * LANES:(h + 1) * LANES] + kpe).astype(BF16)
    vt = lax.dot_general(wvbt_ref[...], ckv, _NT, preferred_element_type=F32).astype(BF16)
    ones = jnp.ones((ONES_ROWS, tm), BF16)
    vt = jnp.concatenate([blk for h in range(MLA_HEADS) for blk in (vt[h * V_HEAD:(h + 1) * V_HEAD], ones)],
                         axis=0)
    for t in range(tm // MLA_TILE):
        vm_ref[t] = vt[:, t * MLA_TILE:(t + 1) * MLA_TILE]

    qst = (proj[:, P_QS:P_QS + SWA_OUT] * (SWA_HEAD_DIM ** -0.5 * LOG2E)).T.astype(BF16)
    vst = proj[:, P_VS:P_VS + SWA_KV].T.astype(BF16)
    vst = jnp.concatenate([blk for h in range(SWA_KV_HEADS)
                           for blk in (vst[h * SWA_HEAD_DIM:(h + 1) * SWA_HEAD_DIM], ones)], axis=0)
    for t in range(tm // WINDOW):
        qs_ref[t] = qst[:, t * WINDOW:(t + 1) * WINDOW]
        vs_ref[t] = vst[:, t * WINDOW:(t + 1) * WINDOW]
    for h in range(SWA_KV_HEADS):
        ks_ref[:, h * LANES:h * LANES + SWA_HEAD_DIM] = (
            proj[:, P_KS + h * SWA_HEAD_DIM:P_KS + (h + 1) * SWA_HEAD_DIM].astype(BF16))
        ks_ref[:, h * LANES + SWA_HEAD_DIM:(h + 1) * LANES] = jnp.zeros((tm, LANES - SWA_HEAD_DIM), BF16)


def _mla_kernel(q_ref, k_ref, v_ref, o_ref, s_ref, p_ref):
    nt, rows, t = q_ref.shape
    hp = rows // LANES
    vr = v_ref.shape[1] // hp
    key = lax.broadcasted_iota(jnp.int32, (t, t), 0)
    qry = lax.broadcasted_iota(jnp.int32, (t, t), 1)
    causal = key <= qry

    def issue(i, j, hh):
        return jnp.dot(k_ref[j * t:(j + 1) * t, hh * LANES:(hh + 1) * LANES],
                       q_ref[i, hh * LANES:(hh + 1) * LANES, :], preferred_element_type=F32)

    def stash(s, masked, hh):
        if masked:
            s = jnp.where(causal, s, NEG)
        s_ref[hh] = s
        return jnp.max(s, axis=0, keepdims=True)

    def fold_values(jp, st, hh):
        m, acc, alpha = st
        pv = jnp.dot(v_ref[jp, hh * vr:(hh + 1) * vr, :], p_ref[hh], preferred_element_type=F32)
        return m, alpha * acc + pv, alpha

    def softmax(tile_max, st, hh):
        m, acc, _ = st
        m_new = jnp.maximum(m, tile_max)
        p_ref[hh] = jnp.exp2(s_ref[hh] - m_new).astype(BF16)
        return m_new, acc, jnp.exp2(m - m_new)

    def finish(i, st, hh):
        acc = st[1]
        o_ref[i, hh * V_HEAD:(hh + 1) * V_HEAD, :] = acc[0:V_HEAD] / acc[V_HEAD:V_HEAD + 1]

    fresh = (jnp.full((1, t), NEG, F32), jnp.zeros((vr, t), F32), jnp.zeros((1, t), F32))

    tiles = [(i, j) for i in range(nt) for j in range(i + 1)]
    tile_max = [stash(issue(0, 0, hh), True, hh) for hh in range(hp)]
    state = [fresh] * hp
    for n, (i, j) in enumerate(tiles):
        prev = tiles[n - 1] if n > 0 else None
        nxt = tiles[n + 1] if n + 1 < len(tiles) else None
        if nxt is not None:
            sc = [issue(*nxt, hh) for hh in range(hp)]
        for hh in range(hp):
            st = state[hh]
            if prev is not None:
                st = fold_values(prev[1], st, hh)
                if prev[0] != i:
                    finish(prev[0], st, hh)
                    st = fresh
            state[hh] = softmax(tile_max[hh], st, hh)
            if nxt is not None:
                tile_max[hh] = stash(sc[hh], nxt[0] == nxt[1], hh)
    for hh in range(hp):
        finish(nt - 1, fold_values(nt - 1, state[hh], hh), hh)


def _swa_kernel(bucket_ref, relb_ref, sink_ref, q_ref, k_ref, v_ref, o_ref, bias_ref, sinkrow_ref,
                s_ref, p_ref):
    w, dh, g = WINDOW, SWA_HEAD_DIM, SWA_GROUP
    nb = q_ref.shape[0]
    vr = v_ref.shape[1] // SWA_KV_HEADS

    @pl.when(pl.program_id(0) == 0)
    def _():
        bucket = bucket_ref[...]
        for kvh in range(SWA_KV_HEADS):
            for gg in range(g):
                h = kvh * g + gg
                b = jnp.zeros((w, w), F32)
                for r in range(REL_BUCKETS):
                    b = jnp.where(bucket == r, relb_ref[r, h] * LOG2E, b)
                bias_ref[kvh, :, gg * w:(gg + 1) * w] = b
                sinkrow_ref[kvh, :, gg * w:(gg + 1) * w] = jnp.full((1, w), sink_ref[h] * LOG2E, F32)

    chains = s_ref.shape[0]
    hu = s_ref.shape[2] // w
    row = lax.broadcasted_iota(jnp.int32, (w, w), 0)
    qry = lax.broadcasted_iota(jnp.int32, (w, w), 1)
    from_prev = jnp.concatenate([row > qry] * hu, axis=1)

    def key_rows(n):
        return (0, w) if n == 0 else ((n - 1) * w, 2 * w)

    def heads(kvh, part):
        return [kvh * g + part * hu + i for i in range(hu)]

    def cols(part):
        return slice(part * hu * w, (part + 1) * hu * w)

    def issue(n, kvh, part):
        qt = jnp.concatenate([q_ref[n, h * dh:(h + 1) * dh, :] for h in heads(kvh, part)], axis=1)
        start, size = key_rows(n)
        k = k_ref[start:start + size, kvh * LANES:kvh * LANES + dh]
        return jnp.dot(k, qt, preferred_element_type=F32)

    def stash(sc, c, n, kvh, part):
        bias = bias_ref[kvh, :, cols(part)]
        if n == 0:
            s = jnp.where(from_prev, NEG, sc + bias)
        else:
            s = jnp.where(from_prev, sc[0:w], sc[w:2 * w]) + bias
        s_ref[c] = s
        return jnp.max(s, axis=0, keepdims=True)

    def softmax(tile_max, c, n, kvh, part):
        sink = sinkrow_ref[kvh, :, cols(part)]
        m = jnp.maximum(tile_max, sink)
        p = jnp.exp2(s_ref[c] - m)
        if n == 0:
            p_ref[c, w:2 * w, :] = p.astype(BF16)
        else:
            p_ref[c, 0:w, :] = jnp.where(from_prev, p, 0.0).astype(BF16)
            p_ref[c, w:2 * w, :] = jnp.where(from_prev, 0.0, p).astype(BF16)
        return jnp.exp2(sink - m)

    def fold_values(sink_term, c, n, kvh, part):
        vrows = slice(kvh * vr, (kvh + 1) * vr)
        if n == 0:
            vwin_t = v_ref[0, vrows, :]
        else:
            vwin_t = jnp.concatenate([v_ref[n - 1, vrows, :], v_ref[n, vrows, :]], axis=1)
        lo = 2 * w - key_rows(n)[1]
        pv = jnp.dot(vwin_t, p_ref[c, lo:, :], preferred_element_type=F32)
        o = pv[0:dh] / (pv[dh:dh + 1] + sink_term)
        for i, h in enumerate(heads(kvh, part)):
            o_ref[n, h * dh:(h + 1) * dh, :] = o[:, i * w:(i + 1) * w]

    units = [(n, kvh, part) for n in range(nb) for kvh in range(SWA_KV_HEADS) for part in range(g // hu)]
    steps = [units[i:i + chains] for i in range(0, len(units), chains)]
    tile_max = [stash(issue(*u), c, *u) for c, u in enumerate(steps[0])]
    sink_term = [None] * chains
    for idx, step in enumerate(steps):
        nxt = steps[idx + 1] if idx + 1 < len(steps) else None
        if nxt is not None:
            sc = [issue(*u) for u in nxt]
        if idx > 0:
            for c, u in enumerate(steps[idx - 1]):
                fold_values(sink_term[c], c, *u)
        for c, u in enumerate(step):
            sink_term[c] = softmax(tile_max[c], c, *u)
        if nxt is not None:
            for c, u in enumerate(nxt):
                tile_max[c] = stash(sc[c], c, *u)
    for c, u in enumerate(steps[-1]):
        fold_values(sink_term[c], c, *u)


def _attn_features(om_ref, os_ref, gom_ref, gos_ref):
    om = jnp.concatenate([om_ref[t].T for t in range(om_ref.shape[0])], axis=0)
    osw = jnp.concatenate([os_ref[t].T for t in range(os_ref.shape[0])], axis=0)
    return _rms(om, gom_ref[...]).astype(BF16), _rms(osw, gos_ref[...]).astype(BF16)


def _out_ffn2_kernel(h1_ref, om_next_ref, os_next_ref, om_first_ref, os_first_ref, gom_ref, gos_ref, wom_ref,
                     wos_ref, g2_ref, wgu_ref, wd_ref, gfin_ref, out_ref, am_ref, as_ref, h3_ref):
    s = pl.program_id(0)
    n_tiles = pl.num_programs(0) - 1

    @pl.when(s == 0)
    def _():
        am_ref[...], as_ref[...] = _attn_features(om_first_ref, os_first_ref, gom_ref, gos_ref)
        h3_ref[...] = jnp.zeros_like(h3_ref)

    @pl.when(s < n_tiles)
    def _():
        half = h1_ref.shape[0] // 2
        h2_parts, hn_parts = [], []
        for r in (slice(0, half), slice(half, 2 * half)):
            part = (h1_ref[r, :] + jnp.dot(am_ref[r, :], wom_ref[...], preferred_element_type=F32)
                    + jnp.dot(as_ref[r, :], wos_ref[...], preferred_element_type=F32))
            h2_parts.append(part)
            hn_parts.append(_rms(part, g2_ref[...]).astype(BF16))
        h2 = jnp.concatenate(h2_parts, axis=0)
        hn = jnp.concatenate(hn_parts, axis=0)
        out_ref[...] = _rms(h3_ref[...], gfin_ref[...])
        am_ref[...], as_ref[...] = _attn_features(om_next_ref, os_next_ref, gom_ref, gos_ref)
        h3_ref[...] = h2 + 0.5 * _swiglu(hn, wgu_ref, wd_ref)

    @pl.when(s == n_tiles)
    def _():
        out_ref[...] = _rms(h3_ref[...], gfin_ref[...])


def _pack_gate_up(w_gate, w_up):
    d, ff = w_gate.shape
    both = jnp.stack([w_gate.reshape(d, ff // FF_TILE, FF_TILE), w_up.reshape(d, ff // FF_TILE, FF_TILE)], axis=2)
    return both.reshape(d, 2 * ff)


def _pack_w_in(w_in):
    o_kpe = Q_LORA + KV_LORA
    o_qs = o_kpe + QK_ROPE
    kpe = jnp.pad(w_in[:, o_kpe:o_qs], ((0, 0), (ROPE_LO, LANES - ROPE_LO - QK_ROPE)))
    return jnp.concatenate([w_in[:, :o_kpe], kpe, w_in[:, o_qs:]], axis=1)


def _pack_w_q_b_t(w_q_b):
    r = w_q_b.shape[0]
    w = w_q_b.reshape(r, MLA_HEADS, QK_NOPE + QK_ROPE)
    w = jnp.pad(w, ((0, 0), (0, 0), (0, LANES - QK_NOPE - QK_ROPE)))
    return w.reshape(r, MLA_HEADS * LANES).T


def _pack_w_kv_b(w_kv_b):
    r = w_kv_b.shape[0]
    w = w_kv_b.reshape(r, MLA_HEADS, QK_NOPE + V_HEAD)
    wk = jnp.pad(w[:, :, :QK_NOPE], ((0, 0), (0, 0), (0, LANES - QK_NOPE))).reshape(r, MLA_HEADS * LANES)
    wv_t = w[:, :, QK_NOPE:].reshape(r, MLA_HEADS * V_HEAD).T
    return wk, wv_t


def _rope_angles(seq):
    pos = jnp.arange(seq, dtype=F32)
    inv_freq = ROPE_THETA ** (-jnp.arange(0, QK_ROPE, 2, dtype=F32) / QK_ROPE)
    ang = pos[:, None] * inv_freq[None, :]
    return jnp.cos(ang), jnp.sin(ang)


def _rope_table_lanes(seq):
    cos, sin = _rope_angles(seq)
    z = lambda n: jnp.zeros((seq, n), F32)
    hi = LANES - ROPE_LO - QK_ROPE
    c = [z(ROPE_LO), cos, cos, z(hi)]
    sa = [z(ROPE_LO), -sin, z(ROPE_HALF + hi)]
    sb = [z(ROPE_LO + ROPE_HALF), sin, z(hi)]
    return jnp.concatenate(c + sa + sb, axis=1)


def _t5_bucket(dist):
    n = jnp.maximum(dist, 0)
    max_exact = REL_BUCKETS // 2
    nf = jnp.maximum(n, 1).astype(F32)
    large = max_exact + (jnp.log(nf / max_exact) / math.log(REL_MAX_DIST / max_exact)
                         * (REL_BUCKETS - max_exact)).astype(jnp.int32)
    large = jnp.minimum(large, REL_BUCKETS - 1)
    return jnp.where(n < max_exact, n, large)


def _const_spec(shape):
    nd = len(shape)
    return pl.BlockSpec(shape, lambda *_: (0,) * nd, pipeline_mode=pl.Buffered(1))


def kernel(x, g_ffn1, w_ffn1_gate, w_ffn1_up, w_ffn1_down, g_mix, w_in, g_q_a, w_q_b, g_kv_a, w_kv_b,
           attn_sinks, rel_bias, g_out_mla, g_out_swa, w_o, g_ffn2, w_ffn2_gate, w_ffn2_up, w_ffn2_down,
           g_final):
    bsz, seq, d = x.shape
    n_tok = bsz * seq
    tm = TOKEN_TILE
    assert d == D_MODEL and seq % tm == 0 and tm % MLA_TILE == 0 and tm % WINDOW == 0
    assert g_ffn1.shape[0] == 1, "single layer"

    row = lambda a: a.reshape(1, -1).astype(F32)
    xf = x.reshape(n_tok, d)

    win = _pack_w_in(w_in[0]).astype(BF16)
    wqbt = _pack_w_q_b_t(w_q_b[0]).astype(BF16)
    wkb, wvbt = _pack_w_kv_b(w_kv_b[0])
    wkb, wvbt = wkb.astype(BF16), wvbt.astype(BF16)
    q_scale = (QK_NOPE + QK_ROPE) ** -0.5 * LOG2E
    cos, sin = _rope_angles(seq)
    rope_q = jnp.concatenate([cos.T, sin.T], axis=0) * q_scale
    rope_k = _rope_table_lanes(seq)

    tiles = n_tok // tm
    tps = seq // tm
    nt_mla, nb_swa = seq // MLA_TILE, seq // WINDOW

    weights1 = [row(g_ffn1[0]), w_ffn1_gate[0].astype(BF16), w_ffn1_up[0].astype(BF16),
                w_ffn1_down[0].astype(BF16), row(g_mix[0]), win, row(g_q_a[0]), wqbt, row(g_kv_a[0]), wkb, wvbt]
    last = tiles - 1
    cur = lambda s: jnp.minimum(s, last)
    prv = lambda s: jnp.maximum(s - 1, 0)
    tok_spec = lambda width, tile_of: pl.BlockSpec((tm, width), lambda s: (tile_of(s), 0))
    fm_spec = lambda rows, tok: pl.BlockSpec(
        (None, tm // tok, rows, tok), lambda s: (prv(s) // tps, prv(s) % tps, 0, 0))
    h1, q_mla, k_mla, v_mla, q_swa, k_swa, v_swa = pl.pallas_call(
        functools.partial(_ffn1_proj_kernel, q_scale=q_scale),
        grid=(tiles + 1,),
        in_specs=[tok_spec(d, cur), tok_spec(d, lambda s: jnp.minimum(s + 1, last))]
                 + [_const_spec(a.shape) for a in weights1]
                 + [pl.BlockSpec((QK_ROPE, tm), lambda s: (0, prv(s) % tps)),
                    pl.BlockSpec((tm, 3 * LANES), lambda s: (prv(s) % tps, 0))],
        out_specs=[tok_spec(d, cur), fm_spec(MLA_HEADS * LANES, MLA_TILE), tok_spec(MLA_HEADS * LANES, prv),
                   fm_spec(MLA_HEADS * MLA_V_ROWS, MLA_TILE), fm_spec(SWA_OUT, WINDOW),
                   tok_spec(SWA_KV_HEADS * LANES, prv),
                   fm_spec(SWA_KV_HEADS * SWA_V_ROWS, WINDOW)],
        scratch_shapes=[pltpu.VMEM((tm, d), BF16), pltpu.VMEM((tm, d), F32)],
        out_shape=[jax.ShapeDtypeStruct((n_tok, d), F32),
                   jax.ShapeDtypeStruct((bsz, nt_mla, MLA_HEADS * LANES, MLA_TILE), BF16),
                   jax.ShapeDtypeStruct((n_tok, MLA_HEADS * LANES), BF16),
                   jax.ShapeDtypeStruct((bsz, nt_mla, MLA_HEADS * MLA_V_ROWS, MLA_TILE), BF16),
                   jax.ShapeDtypeStruct((bsz, nb_swa, SWA_OUT, WINDOW), BF16),
                   jax.ShapeDtypeStruct((n_tok, SWA_KV_HEADS * LANES), BF16),
                   jax.ShapeDtypeStruct((bsz, nb_swa, SWA_KV_HEADS * SWA_V_ROWS, WINDOW), BF16)],
        compiler_params=pltpu.CompilerParams(dimension_semantics=("arbitrary",),
                                             vmem_limit_bytes=VMEM_LIMIT),
        name="ffn1_proj",
    )(xf, xf, *weights1, rope_q, rope_k)

    hp = MLA_HEADS_PER_STEP
    o_mla = pl.pallas_call(
        _mla_kernel,
        grid=(bsz, MLA_HEADS // hp),
        in_specs=[pl.BlockSpec((None, nt_mla, hp * LANES, MLA_TILE), lambda b, p: (b, 0, p, 0)),
                  pl.BlockSpec((seq, hp * LANES), lambda b, p: (b, p)),
                  pl.BlockSpec((None, nt_mla, hp * MLA_V_ROWS, MLA_TILE), lambda b, p: (b, 0, p, 0))],
        out_specs=pl.BlockSpec((None, nt_mla, hp * V_HEAD, MLA_TILE), lambda b, p: (b, 0, p, 0)),
        out_shape=jax.ShapeDtypeStruct((bsz, nt_mla, MLA_OUT, MLA_TILE), F32),
        scratch_shapes=[pltpu.VMEM((hp, MLA_TILE, MLA_TILE), F32), pltpu.VMEM((hp, MLA_TILE, MLA_TILE), BF16)],
        compiler_params=pltpu.CompilerParams(dimension_semantics=("arbitrary", "arbitrary"),
                                             vmem_limit_bytes=VMEM_LIMIT),
        name="mla",
    )(q_mla, k_mla, v_mla)

    kj = jnp.arange(WINDOW)[:, None]
    qi = jnp.arange(WINDOW)[None, :]
    bucket_t = _t5_bucket(jnp.where(kj > qi, qi + WINDOW - kj, qi - kj)).astype(jnp.int32)
    smem = pl.BlockSpec(memory_space=pltpu.SMEM)
    o_swa = pl.pallas_call(
        _swa_kernel,
        grid=(bsz,),
        in_specs=[_const_spec(bucket_t.shape), smem, smem,
                  pl.BlockSpec((None, nb_swa, SWA_OUT, WINDOW), lambda b: (b, 0, 0, 0)),
                  pl.BlockSpec((seq, SWA_KV_HEADS * LANES), lambda b: (b, 0)),
                  pl.BlockSpec((None, nb_swa, SWA_KV_HEADS * SWA_V_ROWS, WINDOW), lambda b: (b, 0, 0, 0))],
        out_specs=pl.BlockSpec((None, nb_swa, SWA_OUT, WINDOW), lambda b: (b, 0, 0, 0)),
        out_shape=jax.ShapeDtypeStruct((bsz, nb_swa, SWA_OUT, WINDOW), F32),
        scratch_shapes=[pltpu.VMEM((SWA_KV_HEADS, WINDOW, SWA_GROUP * WINDOW), F32),
                        pltpu.VMEM((SWA_KV_HEADS, 1, SWA_GROUP * WINDOW), F32),
                        pltpu.VMEM((SWA_UNITS_PER_STEP, WINDOW, SWA_HEADS_PER_UNIT * WINDOW), F32),
                        pltpu.VMEM((SWA_UNITS_PER_STEP, 2 * WINDOW, SWA_HEADS_PER_UNIT * WINDOW), BF16)],
        compiler_params=pltpu.CompilerParams(dimension_semantics=("arbitrary",),
                                             vmem_limit_bytes=VMEM_LIMIT),
        name="swa",
    )(bucket_t, rel_bias.astype(F32), attn_sinks[0].astype(F32), q_swa, k_swa, v_swa)

    wo = w_o[0].astype(BF16)
    weights4 = [row(g_out_mla[0]), row(g_out_swa[0]), wo[:MLA_OUT], wo[MLA_OUT:], row(g_ffn2[0]),
                _pack_gate_up(w_ffn2_gate[0], w_ffn2_up[0]).astype(BF16), w_ffn2_down[0].astype(BF16),
                row(g_final)]
    nxt = lambda s: jnp.minimum(s + 1, last)
    fm_next = lambda rows, tok: pl.BlockSpec(
        (None, tm // tok, rows, tok), lambda s: (nxt(s) // tps, nxt(s) % tps, 0, 0))
    fm_first = lambda rows, tok: pl.BlockSpec((None, tm // tok, rows, tok), lambda s: (0, 0, 0, 0),
                                              pipeline_mode=pl.Buffered(1))
    out = pl.pallas_call(
        _out_ffn2_kernel,
        grid=(tiles + 1,),
        in_specs=[pl.BlockSpec((tm, d), lambda s: (jnp.minimum(s, last), 0)),
                  fm_next(MLA_OUT, MLA_TILE), fm_next(SWA_OUT, WINDOW),
                  fm_first(MLA_OUT, MLA_TILE), fm_first(SWA_OUT, WINDOW)]
                 + [_const_spec(a.shape) for a in weights4],
        out_specs=pl.BlockSpec((tm, d), lambda s: (jnp.maximum(s - 1, 0), 0)),
        out_shape=jax.ShapeDtypeStruct((n_tok, d), F32),
        scratch_shapes=[pltpu.VMEM((tm, MLA_OUT), BF16), pltpu.VMEM((tm, SWA_OUT), BF16),
                        pltpu.VMEM((tm, d), F32)],
        compiler_params=pltpu.CompilerParams(dimension_semantics=("arbitrary",),
                                             vmem_limit_bytes=VMEM_LIMIT),
        name="out_ffn2",
    )(h1, o_mla, o_swa, o_mla, o_swa, *weights4)
    return out.reshape(bsz, seq, d)
```

```python
import functools
import math

import jax
import jax.numpy as jnp
from jax import lax
from jax.experimental import pallas as pl
from jax.experimental.pallas import tpu as pltpu

F32 = jnp.float32
BF16 = jnp.bfloat16

EPS = 1e-6
NEG = -1e30
LOG2E = math.log2(math.e)

D_MODEL = 1024
D_FF = 2816
MLA_HEADS = 8
Q_LORA = 256
KV_LORA = 128
QK_NOPE = 64
QK_ROPE = 32
V_HEAD = 64
ROPE_THETA = 10000.0
SWA_HEADS = 8
SWA_KV_HEADS = 2
SWA_HEAD_DIM = 64
SWA_GROUP = SWA_HEADS // SWA_KV_HEADS
WINDOW = 128
REL_BUCKETS = 32
REL_MAX_DIST = 128
MLA_OUT = MLA_HEADS * V_HEAD
SWA_OUT = SWA_HEADS * SWA_HEAD_DIM
SWA_KV = SWA_KV_HEADS * SWA_HEAD_DIM

LANES = 128
ROPE_LO = QK_NOPE
ROPE_HALF = QK_ROPE // 2

P_CQ = 0
P_CKV = P_CQ + Q_LORA
P_KPE = P_CKV + KV_LORA
P_QS = P_KPE + LANES
P_KS = P_QS + SWA_OUT
P_VS = P_KS + SWA_KV
P_END = P_VS + SWA_KV

TOKEN_TILE = 512
MLA_TILE = 256
MLA_HEADS_PER_STEP = 2
SWA_HEADS_PER_UNIT = 4
SWA_UNITS_PER_STEP = 2
FF_TILE = 256
ONES_ROWS = 16
MLA_V_ROWS = V_HEAD + ONES_ROWS
SWA_V_ROWS = SWA_HEAD_DIM + ONES_ROWS
VMEM_LIMIT = 56 * 1024 * 1024

_NT = (((1,), (1,)), ((), ()))


def _rms(x, g):
    return x * lax.rsqrt(jnp.mean(x * x, axis=-1, keepdims=True) + EPS) * g


def _swiglu(xn, wg_ref, wu_ref, wd_ref):
    ff = wg_ref.shape[1]
    wgu = jnp.concatenate([w[:, c:c + FF_TILE] for c in range(0, ff, FF_TILE) for w in (wg_ref, wu_ref)], axis=1)
    gu = jnp.dot(xn, wgu, preferred_element_type=F32)
    tiles = []
    for j in range(ff // FF_TILE):
        gate = gu[:, 2 * j * FF_TILE:(2 * j + 1) * FF_TILE]
        up = gu[:, (2 * j + 1) * FF_TILE:(2 * j + 2) * FF_TILE]
        tiles.append((gate * jax.nn.sigmoid(gate) * up).astype(BF16))
    return jnp.dot(jnp.concatenate(tiles, axis=1), wd_ref[...], preferred_element_type=F32)


def _rope_lanes(x, tab_ref):
    c = tab_ref[:, 0:LANES]
    sa = tab_ref[:, LANES:2 * LANES]
    sb = tab_ref[:, 2 * LANES:3 * LANES]
    return (x * c + pltpu.roll(x, LANES - ROPE_HALF, 1) * sa + pltpu.roll(x, ROPE_HALF, 1) * sb)


def _ffn1_proj_kernel(x_ref, xnext_ref, g1_ref, wg_ref, wu_ref, wd_ref, gmix_ref, win_ref, gqa_ref, wqbt_ref,
                      gkva_ref, wkb_ref, wvbt_ref, ropeq_ref, ropek_ref,
                      h1_ref, qm_ref, km_ref, vm_ref, qs_ref, ks_ref, vs_ref, xn_ref, h1s_ref, *, q_scale):
    s = pl.program_id(0)
    n_tiles = pl.num_programs(0) - 1
    up_project = functools.partial(_up_projection_stage, gqa_ref, wqbt_ref, gkva_ref, wkb_ref, wvbt_ref,
                                   ropeq_ref, ropek_ref, qm_ref, km_ref, vm_ref, qs_ref, ks_ref, vs_ref,
                                   q_scale=q_scale)

    def project(h1):
        u = _rms(h1, gmix_ref[...]).astype(BF16)
        return jnp.dot(u, win_ref[...], preferred_element_type=F32)

    @pl.when(s == 0)
    def _():
        xn_ref[...] = _rms(x_ref[...], g1_ref[...]).astype(BF16)
        h1s_ref[...] = jnp.zeros_like(h1s_ref)

    @pl.when(s < n_tiles)
    def _():
        xn = xn_ref[...]
        gate = jnp.dot(xn, wg_ref[...], preferred_element_type=F32)
        proj = project(h1s_ref[...])
        up = jnp.dot(xn, wu_ref[...], preferred_element_type=F32)
        up_project(proj)
        act = (gate * jax.nn.sigmoid(gate) * up).astype(BF16)
        h1 = x_ref[...] + 0.5 * jnp.dot(act, wd_ref[...], preferred_element_type=F32)
        h1_ref[...] = h1
        h1s_ref[...] = h1
        xn_ref[...] = _rms(xnext_ref[...], g1_ref[...]).astype(BF16)

    @pl.when(s == n_tiles)
    def _():
        up_project(project(h1s_ref[...]))


def _up_projection_stage(gqa_ref, wqbt_ref, gkva_ref, wkb_ref, wvbt_ref, ropeq_ref, ropek_ref,
                         qm_ref, km_ref, vm_ref, qs_ref, ks_ref, vs_ref, proj, *, q_scale):
    tm = proj.shape[0]

    cq = _rms(proj[:, P_CQ:P_CQ + Q_LORA], gqa_ref[...]).astype(BF16)
    qt = lax.dot_general(wqbt_ref[...], cq, _NT, preferred_element_type=F32)
    cs = ropeq_ref[0:ROPE_HALF, :]
    sn = ropeq_ref[ROPE_HALF:QK_ROPE, :]
    for h in range(MLA_HEADS):
        r = h * LANES
        x1 = qt[r + ROPE_LO:r + ROPE_LO + ROPE_HALF]
        x2 = qt[r + ROPE_LO + ROPE_HALF:r + ROPE_LO + QK_ROPE]
        qh = jnp.concatenate([qt[r:r + QK_NOPE] * q_scale, x1 * cs - x2 * sn, x2 * cs + x1 * sn,
                              qt[r + ROPE_LO + QK_ROPE:r + LANES]], axis=0).astype(BF16)
        for t in range(tm // MLA_TILE):
            qm_ref[t, r:r + LANES, :] = qh[:, t * MLA_TILE:(t + 1) * MLA_TILE]

    ckv = _rms(proj[:, P_CKV:P_CKV + KV_LORA], gkva_ref[...]).astype(BF16)
    kn = jnp.dot(ckv, wkb_ref[...], preferred_element_type=F32)
    kpe = _rope_lanes(proj[:, P_KPE:P_KPE + LANES], ropek_ref)
    for h in range(MLA_HEADS):
        km_ref[:, h * LANES:(h + 1) * LANES] = (kn[:, h * LANES:(h + 1) * LANES] + kpe).astype(BF16)
    vt = lax.dot_general(wvbt_ref[...], ckv, _NT, preferred_element_type=F32).astype(BF16)
    ones = jnp.ones((ONES_ROWS, tm), BF16)
    vt = jnp.concatenate([blk for h in range(MLA_HEADS) for blk in (vt[h * V_HEAD:(h + 1) * V_HEAD], ones)],
                         axis=0)
    for t in range(tm // MLA_TILE):
        vm_ref[t] = vt[:, t * MLA_TILE:(t + 1) * MLA_TILE]

    qst = (proj[:, P_QS:P_QS + SWA_OUT] * (SWA_HEAD_DIM ** -0.5 * LOG2E)).T.astype(BF16)
    vst = proj[:, P_VS:P_VS + SWA_KV].T.astype(BF16)
    vst = jnp.concatenate([blk for h in range(SWA_KV_HEADS)
                           for blk in (vst[h * SWA_HEAD_DIM:(h + 1) * SWA_HEAD_DIM], ones)], axis=0)
    for t in range(tm // WINDOW):
        qs_ref[t] = qst[:, t * WINDOW:(t + 1) * WINDOW]
        vs_ref[t] = vst[:, t * WINDOW:(t + 1) * WINDOW]
    for h in range(SWA_KV_HEADS):
        ks_ref[:, h * LANES:h * LANES + SWA_HEAD_DIM] = (
            proj[:, P_KS + h * SWA_HEAD_DIM:P_KS + (h + 1) * SWA_HEAD_DIM].astype(BF16))
        ks_ref[:, h * LANES + SWA_HEAD_DIM:(h + 1) * LANES] = jnp.zeros((tm, LANES - SWA_HEAD_DIM), BF16)


def _mla_kernel(q_ref, k_ref, v_ref, o_ref, s_ref, p_ref):
    nt, rows, t = q_ref.shape
    hp = rows // LANES
    vr = v_ref.shape[1] // hp
    key = lax.broadcasted_iota(jnp.int32, (t, t), 0)
    qry = lax.broadcasted_iota(jnp.int32, (t, t), 1)
    causal = key <= qry

    def issue(i, j, hh):
        return jnp.dot(k_ref[j * t:(j + 1) * t, hh * LANES:(hh + 1) * LANES],
                       q_ref[i, hh * LANES:(hh + 1) * LANES, :], preferred_element_type=F32)

    def stash(s, masked, hh):
        if masked:
            s = jnp.where(causal, s, NEG)
        s_ref[hh] = s
        return jnp.max(s, axis=0, keepdims=True)

    def fold_values(jp, st, hh):
        m, acc, alpha = st
        pv = jnp.dot(v_ref[jp, hh * vr:(hh + 1) * vr, :], p_ref[hh], preferred_element_type=F32)
        return m, alpha * acc + pv, alpha

    def softmax(tile_max, st, hh):
        m, acc, _ = st
        m_new = jnp.maximum(m, tile_max)
        p_ref[hh] = jnp.exp2(s_ref[hh] - m_new).astype(BF16)
        return m_new, acc, jnp.exp2(m - m_new)

    def finish(i, st, hh):
        acc = st[1]
        o_ref[i, hh * V_HEAD:(hh + 1) * V_HEAD, :] = acc[0:V_HEAD] / acc[V_HEAD:V_HEAD + 1]

    fresh = (jnp.full((1, t), NEG, F32), jnp.zeros((vr, t), F32), jnp.zeros((1, t), F32))

    tiles = [(i, j) for i in range(nt) for j in range(i + 1)]
    tile_max = [stash(issue(0, 0, hh), True, hh) for hh in range(hp)]
    state = [fresh] * hp
    for n, (i, j) in enumerate(tiles):
        prev = tiles[n - 1] if n > 0 else None
        nxt = tiles[n + 1] if n + 1 < len(tiles) else None
        if nxt is not None:
            sc = [issue(*nxt, hh) for hh in range(hp)]
        for hh in range(hp):
            st = state[hh]
            if prev is not None:
                st = fold_values(prev[1], st, hh)
                if prev[0] != i:
                    finish(prev[0], st, hh)
                    st = fresh
            state[hh] = softmax(tile_max[hh], st, hh)
            if nxt is not None:
                tile_max[hh] = stash(sc[hh], nxt[0] == nxt[1], hh)
    for hh in range(hp):
        finish(nt - 1, fold_values(nt - 1, state[hh], hh), hh)


def _swa_kernel(bucket_ref, relb_ref, sink_ref, q_ref, k_ref, v_ref, o_ref, bias_ref, sinkrow_ref,
                s_ref, p_ref):
    w, dh, g = WINDOW, SWA_HEAD_DIM, SWA_GROUP
    nb = q_ref.shape[0]
    vr = v_ref.shape[1] // SWA_KV_HEADS

    @pl.when(pl.program_id(0) == 0)
    def _():
        bucket = bucket_ref[...]
        for kvh in range(SWA_KV_HEADS):
            for gg in range(g):
                h = kvh * g + gg
                b = jnp.zeros((w, w), F32)
                for r in range(REL_BUCKETS):
                    b = jnp.where(bucket == r, relb_ref[r, h] * LOG2E, b)
                bias_ref[kvh, :, gg * w:(gg + 1) * w] = b
                sinkrow_ref[kvh, :, gg * w:(gg + 1) * w] = jnp.full((1, w), sink_ref[h] * LOG2E, F32)

    chains = s_ref.shape[0]
    hu = s_ref.shape[2] // w
    row = lax.broadcasted_iota(jnp.int32, (w, w), 0)
    qry = lax.broadcasted_iota(jnp.int32, (w, w), 1)
    from_prev = jnp.concatenate([row > qry] * hu, axis=1)

    def key_rows(n):
        return (0, w) if n == 0 else ((n - 1) * w, 2 * w)

    def heads(kvh, part):
        return [kvh * g + part * hu + i for i in range(hu)]

    def cols(part):
        return slice(part * hu * w, (part + 1) * hu * w)

    def issue(n, kvh, part):
        qt = jnp.concatenate([q_ref[n, h * dh:(h + 1) * dh, :] for h in heads(kvh, part)], axis=1)
        start, size = key_rows(n)
        k = k_ref[start:start + size, kvh * LANES:kvh * LANES + dh]
        return jnp.dot(k, qt, preferred_element_type=F32)

    def stash(sc, c, n, kvh, part):
        bias = bias_ref[kvh, :, cols(part)]
        if n == 0:
            s = jnp.where(from_prev, NEG, sc + bias)
        else:
            s = jnp.where(from_prev, sc[0:w], sc[w:2 * w]) + bias
        s_ref[c] = s
        return jnp.max(s, axis=0, keepdims=True)

    def softmax(tile_max, c, n, kvh, part):
        sink = sinkrow_ref[kvh, :, cols(part)]
        m = jnp.maximum(tile_max, sink)
        p = jnp.exp2(s_ref[c] - m)
        if n == 0:
            p_ref[c, w:2 * w, :] = p.astype(BF16)
        else:
            p_ref[c, 0:w, :] = jnp.where(from_prev, p, 0.0).astype(BF16)
            p_ref[c, w:2 * w, :] = jnp.where(from_prev, 0.0, p).astype(BF16)
        return jnp.exp2(sink - m)

    def fold_values(sink_term, c, n, kvh, part):
        vrows = slice(kvh * vr, (kvh + 1) * vr)
        if n == 0:
            vwin_t = v_ref[0, vrows, :]
        else:
            vwin_t = jnp.concatenate([v_ref[n - 1, vrows, :], v_ref[n, vrows, :]], axis=1)
        lo = 2 * w - key_rows(n)[1]
        pv = jnp.dot(vwin_t, p_ref[c, lo:, :], preferred_element_type=F32)
        o = pv[0:dh] / (pv[dh:dh + 1] + sink_term)
        for i, h in enumerate(heads(kvh, part)):
            o_ref[n, h * dh:(h + 1) * dh, :] = o[:, i * w:(i + 1) * w]

    units = [(n, kvh, part) for n in range(nb) for kvh in range(SWA_KV_HEADS) for part in range(g // hu)]
    steps = [units[i:i + chains] for i in range(0, len(units), chains)]
    tile_max = [stash(issue(*u), c, *u) for c, u in enumerate(steps[0])]
    sink_term = [None] * chains
    for idx, step in enumerate(steps):
        nxt = steps[idx + 1] if idx + 1 < len(steps) else None
        if nxt is not None:
            sc = [issue(*u) for u in nxt]
        if idx > 0:
            for c, u in enumerate(steps[idx - 1]):
                fold_values(sink_term[c], c, *u)
        for c, u in enumerate(step):
            sink_term[c] = softmax(tile_max[c], c, *u)
        if nxt is not None:
            for c, u in enumerate(nxt):
                tile_max[c] = stash(sc[c], c, *u)
    for c, u in enumerate(steps[-1]):
        fold_values(sink_term[c], c, *u)


def _attn_features(om_ref, os_ref, gom_ref, gos_ref):
    om = jnp.concatenate([om_ref[t].T for t in range(om_ref.shape[0])], axis=0)
    osw = jnp.concatenate([os_ref[t].T for t in range(os_ref.shape[0])], axis=0)
    return _rms(om, gom_ref[...]).astype(BF16), _rms(osw, gos_ref[...]).astype(BF16)


def _out_ffn2_kernel(h1_ref, om_next_ref, os_next_ref, om_first_ref, os_first_ref, gom_ref, gos_ref, wom_ref,
                     wos_ref, g2_ref, wg_ref, wu_ref, wd_ref, gfin_ref, out_ref, am_ref, as_ref, h3_ref):
    s = pl.program_id(0)
    n_tiles = pl.num_programs(0) - 1

    @pl.when(s == 0)
    def _():
        am_ref[...], as_ref[...] = _attn_features(om_first_ref, os_first_ref, gom_ref, gos_ref)
        h3_ref[...] = jnp.zeros_like(h3_ref)

    @pl.when(s < n_tiles)
    def _():
        half = h1_ref.shape[0] // 2
        h2_parts, hn_parts = [], []
        for r in (slice(0, half), slice(half, 2 * half)):
            part = (h1_ref[r, :] + jnp.dot(am_ref[r, :], wom_ref[...], preferred_element_type=F32)
                    + jnp.dot(as_ref[r, :], wos_ref[...], preferred_element_type=F32))
            h2_parts.append(part)
            hn_parts.append(_rms(part, g2_ref[...]).astype(BF16))
        h2 = jnp.concatenate(h2_parts, axis=0)
        hn = jnp.concatenate(hn_parts, axis=0)
        out_ref[...] = _rms(h3_ref[...], gfin_ref[...])
        am_ref[...], as_ref[...] = _attn_features(om_next_ref, os_next_ref, gom_ref, gos_ref)
        h3_ref[...] = h2 + 0.5 * _swiglu(hn, wg_ref, wu_ref, wd_ref)

    @pl.when(s == n_tiles)
    def _():
        out_ref[...] = _rms(h3_ref[...], gfin_ref[...])


def _pack_w_in(w_in):
    o_kpe = Q_LORA + KV_LORA
    o_qs = o_kpe + QK_ROPE
    kpe = jnp.pad(w_in[:, o_kpe:o_qs], ((0, 0), (ROPE_LO, LANES - ROPE_LO - QK_ROPE)))
    return jnp.concatenate([w_in[:, :o_kpe], kpe, w_in[:, o_qs:]], axis=1)


def _pack_w_q_b_t(w_q_b):
    r = w_q_b.shape[0]
    w = w_q_b.reshape(r, MLA_HEADS, QK_NOPE + QK_ROPE)
    w = jnp.pad(w, ((0, 0), (0, 0), (0, LANES - QK_NOPE - QK_ROPE)))
    return w.reshape(r, MLA_HEADS * LANES).T


def _pack_w_kv_b(w_kv_b):
    r = w_kv_b.shape[0]
    w = w_kv_b.reshape(r, MLA_HEADS, QK_NOPE + V_HEAD)
    wk = jnp.pad(w[:, :, :QK_NOPE], ((0, 0), (0, 0), (0, LANES - QK_NOPE))).reshape(r, MLA_HEADS * LANES)
    wv_t = w[:, :, QK_NOPE:].reshape(r, MLA_HEADS * V_HEAD).T
    return wk, wv_t


def _rope_angles(seq):
    pos = jnp.arange(seq, dtype=F32)
    inv_freq = ROPE_THETA ** (-jnp.arange(0, QK_ROPE, 2, dtype=F32) / QK_ROPE)
    ang = pos[:, None] * inv_freq[None, :]
    return jnp.cos(ang), jnp.sin(ang)


def _rope_table_lanes(seq):
    cos, sin = _rope_angles(seq)
    z = lambda n: jnp.zeros((seq, n), F32)
    hi = LANES - ROPE_LO - QK_ROPE
    c = [z(ROPE_LO), cos, cos, z(hi)]
    sa = [z(ROPE_LO), -sin, z(ROPE_HALF + hi)]
    sb = [z(ROPE_LO + ROPE_HALF), sin, z(hi)]
    return jnp.concatenate(c + sa + sb, axis=1)


def _t5_bucket(dist):
    n = jnp.maximum(dist, 0)
    max_exact = REL_BUCKETS // 2
    nf = jnp.maximum(n, 1).astype(F32)
    large = max_exact + (jnp.log(nf / max_exact) / math.log(REL_MAX_DIST / max_exact)
                         * (REL_BUCKETS - max_exact)).astype(jnp.int32)
    large = jnp.minimum(large, REL_BUCKETS - 1)
    return jnp.where(n < max_exact, n, large)


def _const_spec(shape):
    nd = len(shape)
    return pl.BlockSpec(shape, lambda *_: (0,) * nd, pipeline_mode=pl.Buffered(1))


def kernel(x, g_ffn1, w_ffn1_gate, w_ffn1_up, w_ffn1_down, g_mix, w_in, g_q_a, w_q_b, g_kv_a, w_kv_b,
           attn_sinks, rel_bias, g_out_mla, g_out_swa, w_o, g_ffn2, w_ffn2_gate, w_ffn2_up, w_ffn2_down,
           g_final):
    bsz, seq, d = x.shape
    n_tok = bsz * seq
    tm = TOKEN_TILE
    assert d == D_MODEL and seq % tm == 0 and tm % MLA_TILE == 0 and tm % WINDOW == 0
    assert g_ffn1.shape[0] == 1, "single layer"

    row = lambda a: a.reshape(1, -1).astype(F32)
    xf = x.reshape(n_tok, d)

    win = _pack_w_in(w_in[0]).astype(BF16)
    wqbt = _pack_w_q_b_t(w_q_b[0]).astype(BF16)
    wkb, wvbt = _pack_w_kv_b(w_kv_b[0])
    wkb, wvbt = wkb.astype(BF16), wvbt.astype(BF16)
    q_scale = (QK_NOPE + QK_ROPE) ** -0.5 * LOG2E
    cos, sin = _rope_angles(seq)
    rope_q = jnp.concatenate([cos.T, sin.T], axis=0) * q_scale
    rope_k = _rope_table_lanes(seq)

    tiles = n_tok // tm
    tps = seq // tm
    nt_mla, nb_swa = seq // MLA_TILE, seq // WINDOW

    weights1 = [row(g_ffn1[0]), w_ffn1_gate[0].astype(BF16), w_ffn1_up[0].astype(BF16),
                w_ffn1_down[0].astype(BF16), row(g_mix[0]), win, row(g_q_a[0]), wqbt, row(g_kv_a[0]), wkb, wvbt]
    last = tiles - 1
    cur = lambda s: jnp.minimum(s, last)
    prv = lambda s: jnp.maximum(s - 1, 0)
    tok_spec = lambda width, tile_of: pl.BlockSpec((tm, width), lambda s: (tile_of(s), 0))
    fm_spec = lambda rows, tok: pl.BlockSpec(
        (None, tm // tok, rows, tok), lambda s: (prv(s) // tps, prv(s) % tps, 0, 0))
    h1, q_mla, k_mla, v_mla, q_swa, k_swa, v_swa = pl.pallas_call(
        functools.partial(_ffn1_proj_kernel, q_scale=q_scale),
        grid=(tiles + 1,),
        in_specs=[tok_spec(d, cur), tok_spec(d, lambda s: jnp.minimum(s + 1, last))]
                 + [_const_spec(a.shape) for a in weights1]
                 + [pl.BlockSpec((QK_ROPE, tm), lambda s: (0, prv(s) % tps)),
                    pl.BlockSpec((tm, 3 * LANES), lambda s: (prv(s) % tps, 0))],
        out_specs=[tok_spec(d, cur), fm_spec(MLA_HEADS * LANES, MLA_TILE), tok_spec(MLA_HEADS * LANES, prv),
                   fm_spec(MLA_HEADS * MLA_V_ROWS, MLA_TILE), fm_spec(SWA_OUT, WINDOW),
                   tok_spec(SWA_KV_HEADS * LANES, prv),
                   fm_spec(SWA_KV_HEADS * SWA_V_ROWS, WINDOW)],
        scratch_shapes=[pltpu.VMEM((tm, d), BF16), pltpu.VMEM((tm, d), F32)],
        out_shape=[jax.ShapeDtypeStruct((n_tok, d), F32),
                   jax.ShapeDtypeStruct((bsz, nt_mla, MLA_HEADS * LANES, MLA_TILE), BF16),
                   jax.ShapeDtypeStruct((n_tok, MLA_HEADS * LANES), BF16),
                   jax.ShapeDtypeStruct((bsz, nt_mla, MLA_HEADS * MLA_V_ROWS, MLA_TILE), BF16),
                   jax.ShapeDtypeStruct((bsz, nb_swa, SWA_OUT, WINDOW), BF16),
                   jax.ShapeDtypeStruct((n_tok, SWA_KV_HEADS * LANES), BF16),
                   jax.ShapeDtypeStruct((bsz, nb_swa, SWA_KV_HEADS * SWA_V_ROWS, WINDOW), BF16)],
        compiler_params=pltpu.CompilerParams(dimension_semantics=("arbitrary",),
                                             vmem_limit_bytes=VMEM_LIMIT),
        name="ffn1_proj",
    )(xf, xf, *weights1, rope_q, rope_k)

    hp = MLA_HEADS_PER_STEP
    o_mla = pl.pallas_call(
        _mla_kernel,
        grid=(bsz, MLA_HEADS // hp),
        in_specs=[pl.BlockSpec((None, nt_mla, hp * LANES, MLA_TILE), lambda b, p: (b, 0, p, 0)),
                  pl.BlockSpec((seq, hp * LANES), lambda b, p: (b, p)),
                  pl.BlockSpec((None, nt_mla, hp * MLA_V_ROWS, MLA_TILE), lambda b, p: (b, 0, p, 0))],
        out_specs=pl.BlockSpec((None, nt_mla, hp * V_HEAD, MLA_TILE), lambda b, p: (b, 0, p, 0)),
        out_shape=jax.ShapeDtypeStruct((bsz, nt_mla, MLA_OUT, MLA_TILE), F32),
        scratch_shapes=[pltpu.VMEM((hp, MLA_TILE, MLA_TILE), F32), pltpu.VMEM((hp, MLA_TILE, MLA_TILE), BF16)],
        compiler_params=pltpu.CompilerParams(dimension_semantics=("arbitrary", "arbitrary"),
                                             vmem_limit_bytes=VMEM_LIMIT),
        name="mla",
    )(q_mla, k_mla, v_mla)

    kj = jnp.arange(WINDOW)[:, None]
    qi = jnp.arange(WINDOW)[None, :]
    bucket_t = _t5_bucket(jnp.where(kj > qi, qi + WINDOW - kj, qi - kj)).astype(jnp.int32)
    smem = pl.BlockSpec(memory_space=pltpu.SMEM)
    o_swa = pl.pallas_call(
        _swa_kernel,
        grid=(bsz,),
        in_specs=[_const_spec(bucket_t.shape), smem, smem,
                  pl.BlockSpec((None, nb_swa, SWA_OUT, WINDOW), lambda b: (b, 0, 0, 0)),
                  pl.BlockSpec((seq, SWA_KV_HEADS * LANES), lambda b: (b, 0)),
                  pl.BlockSpec((None, nb_swa, SWA_KV_HEADS * SWA_V_ROWS, WINDOW), lambda b: (b, 0, 0, 0))],
        out_specs=pl.BlockSpec((None, nb_swa, SWA_OUT, WINDOW), lambda b: (b, 0, 0, 0)),
        out_shape=jax.ShapeDtypeStruct((bsz, nb_swa, SWA_OUT, WINDOW), F32),
        scratch_shapes=[pltpu.VMEM((SWA_KV_HEADS, WINDOW, SWA_GROUP * WINDOW), F32),
                        pltpu.VMEM((SWA_KV_HEADS, 1, SWA_GROUP * WINDOW), F32),
                        pltpu.VMEM((SWA_UNITS_PER_STEP, WINDOW, SWA_HEADS_PER_UNIT * WINDOW), F32),
                        pltpu.VMEM((SWA_UNITS_PER_STEP, 2 * WINDOW, SWA_HEADS_PER_UNIT * WINDOW), BF16)],
        compiler_params=pltpu.CompilerParams(dimension_semantics=("arbitrary",),
                                             vmem_limit_bytes=VMEM_LIMIT),
        name="swa",
    )(bucket_t, rel_bias.astype(F32), attn_sinks[0].astype(F32), q_swa, k_swa, v_swa)

    wo = w_o[0].astype(BF16)
    weights4 = [row(g_out_mla[0]), row(g_out_swa[0]), wo[:MLA_OUT], wo[MLA_OUT:], row(g_ffn2[0]),
                w_ffn2_gate[0].astype(BF16), w_ffn2_up[0].astype(BF16), w_ffn2_down[0].astype(BF16),
                row(g_final)]
    nxt = lambda s: jnp.minimum(s + 1, last)
    fm_next = lambda rows, tok: pl.BlockSpec(
        (None, tm // tok, rows, tok), lambda s: (nxt(s) // tps, nxt(s) % tps, 0, 0))
    fm_first = lambda rows, tok: pl.BlockSpec((None, tm // tok, rows, tok), lambda s: (0, 0, 0, 0),
                                              pipeline_mode=pl.Buffered(1))
    out = pl.pallas_call(
        _out_ffn2_kernel,
        grid=(tiles + 1,),
        in_specs=[pl.BlockSpec((tm, d), lambda s: (jnp.minimum(s, last), 0)),
                  fm_next(MLA_OUT, MLA_TILE), fm_next(SWA_OUT, WINDOW),
                  fm_first(MLA_OUT, MLA_TILE), fm_first(SWA_OUT, WINDOW)]
                 + [_const_spec(a.shape) for a in weights4],
        out_specs=pl.BlockSpec((tm, d), lambda s: (jnp.maximum(s - 1, 0), 0)),
        out_shape=jax.ShapeDtypeStruct((n_tok, d), F32),
        scratch_shapes=[pltpu.VMEM((tm, MLA_OUT), BF16), pltpu.VMEM((tm, SWA_OUT), BF16),
                        pltpu.VMEM((tm, d), F32)],
        compiler_params=pltpu.CompilerParams(dimension_semantics=("arbitrary",),
                                             vmem_limit_bytes=VMEM_LIMIT),
        name="out_ffn2",
    )(h1, o_mla, o_swa, o_mla, o_swa, *weights4)
    return out.reshape(bsz, seq, d)
```

```python
import functools
import math

import jax
import jax.numpy as jnp
from jax import lax
from jax.experimental import pallas as pl
from jax.experimental.pallas import tpu as pltpu

F32 = jnp.float32
BF16 = jnp.bfloat16

EPS = 1e-6
NEG = -1e30
LOG2E = math.log2(math.e)

D_MODEL = 1024
D_FF = 2816
MLA_HEADS = 8
Q_LORA = 256
KV_LORA = 128
QK_NOPE = 64
QK_ROPE = 32
V_HEAD = 64
ROPE_THETA = 10000.0
SWA_HEADS = 8
SWA_KV_HEADS = 2
SWA_HEAD_DIM = 64
SWA_GROUP = SWA_HEADS // SWA_KV_HEADS
WINDOW = 128
REL_BUCKETS = 32
REL_MAX_DIST = 128
MLA_OUT = MLA_HEADS * V_HEAD
SWA_OUT = SWA_HEADS * SWA_HEAD_DIM
SWA_KV = SWA_KV_HEADS * SWA_HEAD_DIM

LANES = 128
ROPE_LO = QK_NOPE
ROPE_HALF = QK_ROPE // 2

P_CQ = 0
P_CKV = P_CQ + Q_LORA
P_KPE = P_CKV + KV_LORA
P_QS = P_KPE + LANES
P_KS = P_QS + SWA_OUT
P_VS = P_KS + SWA_KV
P_END = P_VS + SWA_KV

TOKEN_TILE = 512
MLA_TILE = 256
MLA_HEADS_PER_STEP = 2
SWA_HEADS_PER_UNIT = 4
SWA_UNITS_PER_STEP = 2
FF_TILE = 256
ONES_ROWS = 16
MLA_V_ROWS = V_HEAD + ONES_ROWS
SWA_V_ROWS = SWA_HEAD_DIM + ONES_ROWS
VMEM_LIMIT = 56 * 1024 * 1024

_NT = (((1,), (1,)), ((), ()))


def _rms(x, g):
    return x * lax.rsqrt(jnp.mean(x * x, axis=-1, keepdims=True) + EPS) * g


def _swiglu(xn, wg_ref, wu_ref, wd_ref):
    ff = wg_ref.shape[1]
    wgu = jnp.concatenate([w[:, c:c + FF_TILE] for c in range(0, ff, FF_TILE) for w in (wg_ref, wu_ref)], axis=1)
    gu = jnp.dot(xn, wgu, preferred_element_type=F32)
    tiles = []
    for j in range(ff // FF_TILE):
        gate = gu[:, 2 * j * FF_TILE:(2 * j + 1) * FF_TILE]
        up = gu[:, (2 * j + 1) * FF_TILE:(2 * j + 2) * FF_TILE]
        tiles.append((gate * jax.nn.sigmoid(gate) * up).astype(BF16))
    return jnp.dot(jnp.concatenate(tiles, axis=1), wd_ref[...], preferred_element_type=F32)


def _rope_lanes(x, tab_ref):
    c = tab_ref[:, 0:LANES]
    sa = tab_ref[:, LANES:2 * LANES]
    sb = tab_ref[:, 2 * LANES:3 * LANES]
    return (x * c + pltpu.roll(x, LANES - ROPE_HALF, 1) * sa + pltpu.roll(x, ROPE_HALF, 1) * sb)


def _ffn1_proj_kernel(x_ref, xnext_ref, g1_ref, wg_ref, wu_ref, wd_ref, gmix_ref, win_ref, gqa_ref, wqbt_ref,
                      gkva_ref, wkb_ref, wvbt_ref, ropeq_ref, ropek_ref,
                      h1_ref, qm_ref, km_ref, vm_ref, qs_ref, ks_ref, vs_ref, xn_ref, h1s_ref, *, q_scale):
    s = pl.program_id(0)
    n_tiles = pl.num_programs(0) - 1
    up_project = functools.partial(_up_projection_stage, gqa_ref, wqbt_ref, gkva_ref, wkb_ref, wvbt_ref,
                                   ropeq_ref, ropek_ref, qm_ref, km_ref, vm_ref, qs_ref, ks_ref, vs_ref,
                                   q_scale=q_scale)

    def project(h1):
        u = _rms(h1, gmix_ref[...]).astype(BF16)
        return jnp.dot(u, win_ref[...], preferred_element_type=F32)

    @pl.when(s == 0)
    def _():
        xn_ref[...] = _rms(x_ref[...], g1_ref[...]).astype(BF16)
        h1s_ref[...] = jnp.zeros_like(h1s_ref)

    @pl.when(s < n_tiles)
    def _():
        xn = xn_ref[...]
        gate = jnp.dot(xn, wg_ref[...], preferred_element_type=F32)
        proj = project(h1s_ref[...])
        up = jnp.dot(xn, wu_ref[...], preferred_element_type=F32)
        up_project(proj)
        act = (gate * jax.nn.sigmoid(gate) * up).astype(BF16)
        h1 = x_ref[...] + 0.5 * jnp.dot(act, wd_ref[...], preferred_element_type=F32)
        h1_ref[...] = h1
        h1s_ref[...] = h1
        xn_ref[...] = _rms(xnext_ref[...], g1_ref[...]).astype(BF16)

    @pl.when(s == n_tiles)
    def _():
        up_project(project(h1s_ref[...]))


def _up_projection_stage(gqa_ref, wqbt_ref, gkva_ref, wkb_ref, wvbt_ref, ropeq_ref, ropek_ref,
                         qm_ref, km_ref, vm_ref, qs_ref, ks_ref, vs_ref, proj, *, q_scale):
    tm = proj.shape[0]

    cq = _rms(proj[:, P_CQ:P_CQ + Q_LORA], gqa_ref[...]).astype(BF16)
    qt = lax.dot_general(wqbt_ref[...], cq, _NT, preferred_element_type=F32)
    cs = ropeq_ref[0:ROPE_HALF, :]
    sn = ropeq_ref[ROPE_HALF:QK_ROPE, :]
    for h in range(MLA_HEADS):
        r = h * LANES
        x1 = qt[r + ROPE_LO:r + ROPE_LO + ROPE_HALF]
        x2 = qt[r + ROPE_LO + ROPE_HALF:r + ROPE_LO + QK_ROPE]
        qh = jnp.concatenate([qt[r:r + QK_NOPE] * q_scale, x1 * cs - x2 * sn, x2 * cs + x1 * sn,
                              qt[r + ROPE_LO + QK_ROPE:r + LANES]], axis=0).astype(BF16)
        for t in range(tm // MLA_TILE):
            qm_ref[t, r:r + LANES, :] = qh[:, t * MLA_TILE:(t + 1) * MLA_TILE]

    ckv = _rms(proj[:, P_CKV:P_CKV + KV_LORA], gkva_ref[...]).astype(BF16)
    kn = jnp.dot(ckv, wkb_ref[...], preferred_element_type=F32)
    kpe = _rope_lanes(proj[:, P_KPE:P_KPE + LANES], ropek_ref)
    for h in range(MLA_HEADS):
        km_ref[:, h * LANES:(h + 1) * LANES] = (kn[:, h * LANES:(h + 1) * LANES] + kpe).astype(BF16)
    vt = lax.dot_general(wvbt_ref[...], ckv, _NT, preferred_element_type=F32).astype(BF16)
    ones = jnp.ones((ONES_ROWS, tm), BF16)
    vt = jnp.concatenate([blk for h in range(MLA_HEADS) for blk in (vt[h * V_HEAD:(h + 1) * V_HEAD], ones)],
                         axis=0)
    for t in range(tm // MLA_TILE):
        vm_ref[t] = vt[:, t * MLA_TILE:(t + 1) * MLA_TILE]

    qst = (proj[:, P_QS:P_QS + SWA_OUT] * (SWA_HEAD_DIM ** -0.5 * LOG2E)).T.astype(BF16)
    vst = proj[:, P_VS:P_VS + SWA_KV].T.astype(BF16)
    vst = jnp.concatenate([blk for h in range(SWA_KV_HEADS)
                           for blk in (vst[h * SWA_HEAD_DIM:(h + 1) * SWA_HEAD_DIM], ones)], axis=0)
    for t in range(tm // WINDOW):
        qs_ref[t] = qst[:, t * WINDOW:(t + 1) * WINDOW]
        vs_ref[t] = vst[:, t * WINDOW:(t + 1) * WINDOW]
    for h in range(SWA_KV_HEADS):
        ks_ref[:, h * LANES:h * LANES + SWA_HEAD_DIM] = (
            proj[:, P_KS + h * SWA_HEAD_DIM:P_KS + (h + 1) * SWA_HEAD_DIM].astype(BF16))
        ks_ref[:, h * LANES + SWA_HEAD_DIM:(h + 1) * LANES] = jnp.zeros((tm, LANES - SWA_HEAD_DIM), BF16)


def _mla_kernel(q_ref, k_ref, v_ref, o_ref, s_ref, p_ref):
    nt, rows, t = q_ref.shape
    hp = rows // LANES
    vr = v_ref.shape[1] // hp
    key = lax.broadcasted_iota(jnp.int32, (t, t), 0)
    qry = lax.broadcasted_iota(jnp.int32, (t, t), 1)
    causal = key <= qry

    def issue(i, j, hh):
        return jnp.dot(k_ref[j * t:(j + 1) * t, hh * LANES:(hh + 1) * LANES],
                       q_ref[i, hh * LANES:(hh + 1) * LANES, :], preferred_element_type=F32)

    def stash(s, masked, hh):
        if masked:
            s = jnp.where(causal, s, NEG)
        s_ref[hh] = s
        return jnp.max(s, axis=0, keepdims=True)

    def fold_values(jp, st, hh):
        m, acc, alpha = st
        pv = jnp.dot(v_ref[jp, hh * vr:(hh + 1) * vr, :], p_ref[hh], preferred_element_type=F32)
        return m, alpha * acc + pv, alpha

    def softmax(tile_max, st, hh):
        m, acc, _ = st
        m_new = jnp.maximum(m, tile_max)
        p_ref[hh] = jnp.exp2(s_ref[hh] - m_new).astype(BF16)
        return m_new, acc, jnp.exp2(m - m_new)

    def finish(i, st, hh):
        acc = st[1]
        o_ref[i, hh * V_HEAD:(hh + 1) * V_HEAD, :] = acc[0:V_HEAD] / acc[V_HEAD:V_HEAD + 1]

    fresh = (jnp.full((1, t), NEG, F32), jnp.zeros((vr, t), F32), jnp.zeros((1, t), F32))

    tiles = [(i, j) for i in range(nt) for j in range(i + 1)]
    tile_max = [stash(issue(0, 0, hh), True, hh) for hh in range(hp)]
    state = [fresh] * hp
    for n, (i, j) in enumerate(tiles):
        prev = tiles[n - 1] if n > 0 else None
        nxt = tiles[n + 1] if n + 1 < len(tiles) else None
        if nxt is not None:
            sc = [issue(*nxt, hh) for hh in range(hp)]
        for hh in range(hp):
            st = state[hh]
            if prev is not None:
                st = fold_values(prev[1], st, hh)
                if prev[0] != i:
                    finish(prev[0], st, hh)
                    st = fresh
            state[hh] = softmax(tile_max[hh], st, hh)
            if nxt is not None:
                tile_max[hh] = stash(sc[hh], nxt[0] == nxt[1], hh)
    for hh in range(hp):
        finish(nt - 1, fold_values(nt - 1, state[hh], hh), hh)


def _swa_kernel(bucket_ref, relb_ref, sink_ref, q_ref, k_ref, v_ref, o_ref, bias_ref, sinkrow_ref,
                s_ref, p_ref):
    w, dh, g = WINDOW, SWA_HEAD_DIM, SWA_GROUP
    nb = q_ref.shape[0]
    vr = v_ref.shape[1] // SWA_KV_HEADS

    @pl.when(pl.program_id(0) == 0)
    def _():
        bucket = bucket_ref[...]
        for kvh in range(SWA_KV_HEADS):
            for gg in range(g):
                h = kvh * g + gg
                b = jnp.zeros((w, w), F32)
                for r in range(REL_BUCKETS):
                    b = jnp.where(bucket == r, relb_ref[r, h] * LOG2E, b)
                bias_ref[kvh, :, gg * w:(gg + 1) * w] = b
                sinkrow_ref[kvh, :, gg * w:(gg + 1) * w] = jnp.full((1, w), sink_ref[h] * LOG2E, F32)

    chains = s_ref.shape[0]
    hu = s_ref.shape[2] // w
    row = lax.broadcasted_iota(jnp.int32, (w, w), 0)
    qry = lax.broadcasted_iota(jnp.int32, (w, w), 1)
    from_prev = jnp.concatenate([row > qry] * hu, axis=1)

    def key_rows(n):
        return (0, w) if n == 0 else ((n - 1) * w, 2 * w)

    def heads(kvh, part):
        return [kvh * g + part * hu + i for i in range(hu)]

    def cols(part):
        return slice(part * hu * w, (part + 1) * hu * w)

    def issue(n, kvh, part):
        qt = jnp.concatenate([q_ref[n, h * dh:(h + 1) * dh, :] for h in heads(kvh, part)], axis=1)
        start, size = key_rows(n)
        k = k_ref[start:start + size, kvh * LANES:kvh * LANES + dh]
        return jnp.dot(k, qt, preferred_element_type=F32)

    def stash(sc, c, n, kvh, part):
        bias = bias_ref[kvh, :, cols(part)]
        if n == 0:
            s = jnp.where(from_prev, NEG, sc + bias)
        else:
            s = jnp.where(from_prev, sc[0:w], sc[w:2 * w]) + bias
        s_ref[c] = s
        return jnp.max(s, axis=0, keepdims=True)

    def softmax(tile_max, c, n, kvh, part):
        sink = sinkrow_ref[kvh, :, cols(part)]
        m = jnp.maximum(tile_max, sink)
        p = jnp.exp2(s_ref[c] - m)
        if n == 0:
            p_ref[c, w:2 * w, :] = p.astype(BF16)
        else:
            p_ref[c, 0:w, :] = jnp.where(from_prev, p, 0.0).astype(BF16)
            p_ref[c, w:2 * w, :] = jnp.where(from_prev, 0.0, p).astype(BF16)
        return jnp.exp2(sink - m)

    def fold_values(sink_term, c, n, kvh, part):
        vrows = slice(kvh * vr, (kvh + 1) * vr)
        if n == 0:
            vwin_t = v_ref[0, vrows, :]
        else:
            vwin_t = jnp.concatenate([v_ref[n - 1, vrows, :], v_ref[n, vrows, :]], axis=1)
        lo = 2 * w - key_rows(n)[1]
        pv = jnp.dot(vwin_t, p_ref[c, lo:, :], preferred_element_type=F32)
        o = pv[0:dh] / (pv[dh:dh + 1] + sink_term)
        for i, h in enumerate(heads(kvh, part)):
            o_ref[n, h * dh:(h + 1) * dh, :] = o[:, i * w:(i + 1) * w]

    units = [(n, kvh, part) for n in range(nb) for kvh in range(SWA_KV_HEADS) for part in range(g // hu)]
    steps = [units[i:i + chains] for i in range(0, len(units), chains)]
    tile_max = [stash(issue(*u), c, *u) for c, u in enumerate(steps[0])]
    sink_term = [None] * chains
    for idx, step in enumerate(steps):
        nxt = steps[idx + 1] if idx + 1 < len(steps) else None
        if nxt is not None:
            sc = [issue(*u) for u in nxt]
        if idx > 0:
            for c, u in enumerate(steps[idx - 1]):
                fold_values(sink_term[c], c, *u)
        for c, u in enumerate(step):
            sink_term[c] = softmax(tile_max[c], c, *u)
        if nxt is not None:
            for c, u in enumerate(nxt):
                tile_max[c] = stash(sc[c], c, *u)
    for c, u in enumerate(steps[-1]):
        fold_values(sink_term[c], c, *u)


def _attn_features(om_ref, os_ref, gom_ref, gos_ref):
    om = jnp.concatenate([om_ref[t].T for t in range(om_ref.shape[0])], axis=0)
    osw = jnp.concatenate([os_ref[t].T for t in range(os_ref.shape[0])], axis=0)
    return _rms(om, gom_ref[...]).astype(BF16), _rms(osw, gos_ref[...]).astype(BF16)


def _out_ffn2_kernel(h1_ref, om_next_ref, os_next_ref, om_first_ref, os_first_ref, gom_ref, gos_ref, wom_ref,
                     wos_ref, g2_ref, wg_ref, wu_ref, wd_ref, gfin_ref, out_ref, am_ref, as_ref, h3_ref):
    s = pl.program_id(0)
    n_tiles = pl.num_programs(0) - 1

    @pl.when(s == 0)
    def _():
        am_ref[...], as_ref[...] = _attn_features(om_first_ref, os_first_ref, gom_ref, gos_ref)
        h3_ref[...] = jnp.zeros_like(h3_ref)

    @pl.when(s < n_tiles)
    def _():
        half = h1_ref.shape[0] // 2
        h2_parts, hn_parts = [], []
        for r in (slice(0, half), slice(half, 2 * half)):
            part = (h1_ref[r, :] + jnp.dot(am_ref[r, :], wom_ref[...], preferred_element_type=F32)
                    + jnp.dot(as_ref[r, :], wos_ref[...], preferred_element_type=F32))
            h2_parts.append(part)
            hn_parts.append(_rms(part, g2_ref[...]).astype(BF16))
        h2 = jnp.concatenate(h2_parts, axis=0)
        hn = jnp.concatenate(hn_parts, axis=0)
        out_ref[...] = _rms(h3_ref[...], gfin_ref[...])
        am_ref[...], as_ref[...] = _attn_features(om_next_ref, os_next_ref, gom_ref, gos_ref)
        h3_ref[...] = h2 + 0.5 * _swiglu(hn, wg_ref, wu_ref, wd_ref)

    @pl.when(s == n_tiles)
    def _():
        out_ref[...] = _rms(h3_ref[...], gfin_ref[...])


def _pack_w_in(w_in):
    o_kpe = Q_LORA + KV_LORA
    o_qs = o_kpe + QK_ROPE
    kpe = jnp.pad(w_in[:, o_kpe:o_qs], ((0, 0), (ROPE_LO, LANES - ROPE_LO - QK_ROPE)))
    return jnp.concatenate([w_in[:, :o_kpe], kpe, w_in[:, o_qs:]], axis=1)


def _pack_w_q_b_t(w_q_b):
    r = w_q_b.shape[0]
    w = w_q_b.reshape(r, MLA_HEADS, QK_NOPE + QK_ROPE)
    w = jnp.pad(w, ((0, 0), (0, 0), (0, LANES - QK_NOPE - QK_ROPE)))
    return w.reshape(r, MLA_HEADS * LANES).T


def _pack_w_kv_b(w_kv_b):
    r = w_kv_b.shape[0]
    w = w_kv_b.reshape(r, MLA_HEADS, QK_NOPE + V_HEAD)
    wk = jnp.pad(w[:, :, :QK_NOPE], ((0, 0), (0, 0), (0, LANES - QK_NOPE))).reshape(r, MLA_HEADS * LANES)
    wv_t = w[:, :, QK_NOPE:].reshape(r, MLA_HEADS * V_HEAD).T
    return wk, wv_t


def _rope_angles(seq):
    pos = jnp.arange(seq, dtype=F32)
    inv_freq = ROPE_THETA ** (-jnp.arange(0, QK_ROPE, 2, dtype=F32) / QK_ROPE)
    ang = pos[:, None] * inv_freq[None, :]
    return jnp.cos(ang), jnp.sin(ang)


def _rope_table_lanes(seq):
    cos, sin = _rope_angles(seq)
    z = lambda n: jnp.zeros((seq, n), F32)
    hi = LANES - ROPE_LO - QK_ROPE
    c = [z(ROPE_LO), cos, cos, z(hi)]
    sa = [z(ROPE_LO), -sin, z(ROPE_HALF + hi)]
    sb = [z(ROPE_LO + ROPE_HALF), sin, z(hi)]
    return jnp.concatenate(c + sa + sb, axis=1)


def _t5_bucket(dist):
    n = jnp.maximum(dist, 0)
    max_exact = REL_BUCKETS // 2
    nf = jnp.maximum(n, 1).astype(F32)
    large = max_exact + (jnp.log(nf / max_exact) / math.log(REL_MAX_DIST / max_exact)
                         * (REL_BUCKETS - max_exact)).astype(jnp.int32)
    large = jnp.minimum(large, REL_BUCKETS - 1)
    return jnp.where(n < max_exact, n, large)


def _const_spec(shape):
    nd = len(shape)
    return pl.BlockSpec(shape, lambda *_: (0,) * nd, pipeline_mode=pl.Buffered(1))


def kernel(x, g_ffn1, w_ffn1_gate, w_ffn1_up, w_ffn1_down, g_mix, w_in, g_q_a, w_q_b, g_kv_a, w_kv_b,
           attn_sinks, rel_bias, g_out_mla, g_out_swa, w_o, g_ffn2, w_ffn2_gate, w_ffn2_up, w_ffn2_down,
           g_final):
    bsz, seq, d = x.shape
    n_tok = bsz * seq
    tm = TOKEN_TILE
    assert d == D_MODEL and seq % tm == 0 and tm % MLA_TILE == 0 and tm % WINDOW == 0
    assert g_ffn1.shape[0] == 1, "single layer"

    row = lambda a: a.reshape(1, -1).astype(F32)
    xf = x.reshape(n_tok, d)

    win = _pack_w_in(w_in[0].astype(BF16))
    wqbt = _pack_w_q_b_t(w_q_b[0].astype(BF16))
    wkb, wvbt = _pack_w_kv_b(w_kv_b[0].astype(BF16))
    q_scale = (QK_NOPE + QK_ROPE) ** -0.5 * LOG2E
    cos, sin = _rope_angles(seq)
    rope_q = jnp.concatenate([cos.T, sin.T], axis=0) * q_scale
    rope_k = _rope_table_lanes(seq)

    tiles = n_tok // tm
    tps = seq // tm
    nt_mla, nb_swa = seq // MLA_TILE, seq // WINDOW

    weights1 = [row(g_ffn1[0]), w_ffn1_gate[0].astype(BF16), w_ffn1_up[0].astype(BF16),
                w_ffn1_down[0].astype(BF16), row(g_mix[0]), win, row(g_q_a[0]), wqbt, row(g_kv_a[0]), wkb, wvbt]
    last = tiles - 1
    cur = lambda s: jnp.minimum(s, last)
    prv = lambda s: jnp.maximum(s - 1, 0)
    tok_spec = lambda width, tile_of: pl.BlockSpec((tm, width), lambda s: (tile_of(s), 0))
    fm_spec = lambda rows, tok: pl.BlockSpec(
        (None, tm // tok, rows, tok), lambda s: (prv(s) // tps, prv(s) % tps, 0, 0))
    h1, q_mla, k_mla, v_mla, q_swa, k_swa, v_swa = pl.pallas_call(
        functools.partial(_ffn1_proj_kernel, q_scale=q_scale),
        grid=(tiles + 1,),
        in_specs=[tok_spec(d, cur), tok_spec(d, lambda s: jnp.minimum(s + 1, last))]
                 + [_const_spec(a.shape) for a in weights1]
                 + [pl.BlockSpec((QK_ROPE, tm), lambda s: (0, prv(s) % tps)),
                    pl.BlockSpec((tm, 3 * LANES), lambda s: (prv(s) % tps, 0))],
        out_specs=[tok_spec(d, cur), fm_spec(MLA_HEADS * LANES, MLA_TILE), tok_spec(MLA_HEADS * LANES, prv),
                   fm_spec(MLA_HEADS * MLA_V_ROWS, MLA_TILE), fm_spec(SWA_OUT, WINDOW),
                   tok_spec(SWA_KV_HEADS * LANES, prv),
                   fm_spec(SWA_KV_HEADS * SWA_V_ROWS, WINDOW)],
        scratch_shapes=[pltpu.VMEM((tm, d), BF16), pltpu.VMEM((tm, d), F32)],
        out_shape=[jax.ShapeDtypeStruct((n_tok, d), F32),
                   jax.ShapeDtypeStruct((bsz, nt_mla, MLA_HEADS * LANES, MLA_TILE), BF16),
                   jax.ShapeDtypeStruct((n_tok, MLA_HEADS * LANES), BF16),
                   jax.ShapeDtypeStruct((bsz, nt_mla, MLA_HEADS * MLA_V_ROWS, MLA_TILE), BF16),
                   jax.ShapeDtypeStruct((bsz, nb_swa, SWA_OUT, WINDOW), BF16),
                   jax.ShapeDtypeStruct((n_tok, SWA_KV_HEADS * LANES), BF16),
                   jax.ShapeDtypeStruct((bsz, nb_swa, SWA_KV_HEADS * SWA_V_ROWS, WINDOW), BF16)],
        compiler_params=pltpu.CompilerParams(dimension_semantics=("arbitrary",),
                                             vmem_limit_bytes=VMEM_LIMIT),
        name="ffn1_proj",
    )(xf, xf, *weights1, rope_q, rope_k)

    hp = MLA_HEADS_PER_STEP
    o_mla = pl.pallas_call(
        _mla_kernel,
        grid=(bsz, MLA_HEADS // hp),
        in_specs=[pl.BlockSpec((None, nt_mla, hp * LANES, MLA_TILE), lambda b, p: (b, 0, p, 0)),
                  pl.BlockSpec((seq, hp * LANES), lambda b, p: (b, p)),
                  pl.BlockSpec((None, nt_mla, hp * MLA_V_ROWS, MLA_TILE), lambda b, p: (b, 0, p, 0))],
        out_specs=pl.BlockSpec((None, nt_mla, hp * V_HEAD, MLA_TILE), lambda b, p: (b, 0, p, 0)),
        out_shape=jax.ShapeDtypeStruct((bsz, nt_mla, MLA_OUT, MLA_TILE), F32),
        scratch_shapes=[pltpu.VMEM((hp, MLA_TILE, MLA_TILE), F32), pltpu.VMEM((hp, MLA_TILE, MLA_TILE), BF16)],
        compiler_params=pltpu.CompilerParams(dimension_semantics=("arbitrary", "arbitrary"),
                                             vmem_limit_bytes=VMEM_LIMIT),
        name="mla",
    )(q_mla, k_mla, v_mla)

    kj = jnp.arange(WINDOW)[:, None]
    qi = jnp.arange(WINDOW)[None, :]
    bucket_t = _t5_bucket(jnp.where(kj > qi, qi + WINDOW - kj, qi - kj)).astype(jnp.int32)
    smem = pl.BlockSpec(memory_space=pltpu.SMEM)
    o_swa = pl.pallas_call(
        _swa_kernel,
        grid=(bsz,),
        in_specs=[_const_spec(bucket_t.shape), smem, smem,
                  pl.BlockSpec((None, nb_swa, SWA_OUT, WINDOW), lambda b: (b, 0, 0, 0)),
                  pl.BlockSpec((seq, SWA_KV_HEADS * LANES), lambda b: (b, 0)),
                  pl.BlockSpec((None, nb_swa, SWA_KV_HEADS * SWA_V_ROWS, WINDOW), lambda b: (b, 0, 0, 0))],
        out_specs=pl.BlockSpec((None, nb_swa, SWA_OUT, WINDOW), lambda b: (b, 0, 0, 0)),
        out_shape=jax.ShapeDtypeStruct((bsz, nb_swa, SWA_OUT, WINDOW), F32),
        scratch_shapes=[pltpu.VMEM((SWA_KV_HEADS, WINDOW, SWA_GROUP * WINDOW), F32),
                        pltpu.VMEM((SWA_KV_HEADS, 1, SWA_GROUP * WINDOW), F32),
                        pltpu.VMEM((SWA_UNITS_PER_STEP, WINDOW, SWA_HEADS_PER_UNIT * WINDOW), F32),
                        pltpu.VMEM((SWA_UNITS_PER_STEP, 2 * WINDOW, SWA_HEADS_PER_UNIT * WINDOW), BF16)],
        compiler_params=pltpu.CompilerParams(dimension_semantics=("arbitrary",),
                                             vmem_limit_bytes=VMEM_LIMIT),
        name="swa",
    )(bucket_t, rel_bias.astype(F32), attn_sinks[0].astype(F32), q_swa, k_swa, v_swa)

    wo = w_o[0].astype(BF16)
    weights4 = [row(g_out_mla[0]), row(g_out_swa[0]), wo[:MLA_OUT], wo[MLA_OUT:], row(g_ffn2[0]),
                w_ffn2_gate[0].astype(BF16), w_ffn2_up[0].astype(BF16), w_ffn2_down[0].astype(BF16),
                row(g_final)]
    nxt = lambda s: jnp.minimum(s + 1, last)
    fm_next = lambda rows, tok: pl.BlockSpec(
        (None, tm // tok, rows, tok), lambda s: (nxt(s) // tps, nxt(s) % tps, 0, 0))
    fm_first = lambda rows, tok: pl.BlockSpec((None, tm // tok, rows, tok), lambda s: (0, 0, 0, 0),
                                              pipeline_mode=pl.Buffered(1))
    out = pl.pallas_call(
        _out_ffn2_kernel,
        grid=(tiles + 1,),
        in_specs=[pl.BlockSpec((tm, d), lambda s: (jnp.minimum(s, last), 0)),
                  fm_next(MLA_OUT, MLA_TILE), fm_next(SWA_OUT, WINDOW),
                  fm_first(MLA_OUT, MLA_TILE), fm_first(SWA_OUT, WINDOW)]
                 + [_const_spec(a.shape) for a in weights4],
        out_specs=pl.BlockSpec((tm, d), lambda s: (jnp.maximum(s - 1, 0), 0)),
        out_shape=jax.ShapeDtypeStruct((n_tok, d), F32),
        scratch_shapes=[pltpu.VMEM((tm, MLA_OUT), BF16), pltpu.VMEM((tm, SWA_OUT), BF16),
                        pltpu.VMEM((tm, d), F32)],
        compiler_params=pltpu.CompilerParams(dimension_semantics=("arbitrary",),
                                             vmem_limit_bytes=VMEM_LIMIT),
        name="out_ffn2",
    )(h1, o_mla, o_swa, o_mla, o_swa, *weights4)
    return out.reshape(bsz, seq, d)
```

```python
import functools
import math

import jax
import jax.numpy as jnp
from jax import lax
from jax.experimental import pallas as pl
from jax.experimental.pallas import tpu as pltpu

F32 = jnp.float32
BF16 = jnp.bfloat16

EPS = 1e-6
NEG = -1e30
LOG2E = math.log2(math.e)

D_MODEL = 1024
D_FF = 2816
MLA_HEADS = 8
Q_LORA = 256
KV_LORA = 128
QK_NOPE = 64
QK_ROPE = 32
V_HEAD = 64
ROPE_THETA = 10000.0
SWA_HEADS = 8
SWA_KV_HEADS = 2
SWA_HEAD_DIM = 64
SWA_GROUP = SWA_HEADS // SWA_KV_HEADS
WINDOW = 128
REL_BUCKETS = 32
REL_MAX_DIST = 128
MLA_OUT = MLA_HEADS * V_HEAD
SWA_OUT = SWA_HEADS * SWA_HEAD_DIM
SWA_KV = SWA_KV_HEADS * SWA_HEAD_DIM

LANES = 128
ROPE_LO = QK_NOPE
ROPE_HALF = QK_ROPE // 2

P_CQ = 0
P_CKV = P_CQ + Q_LORA
P_KPE = P_CKV + KV_LORA
P_QS = P_KPE + LANES
P_KS = P_QS + SWA_OUT
P_VS = P_KS + SWA_KV
P_END = P_VS + SWA_KV

TOKEN_TILE = 512
MLA_TILE = 256
MLA_HEADS_PER_STEP = 2
SWA_HEADS_PER_UNIT = 4
SWA_UNITS_PER_STEP = 2
FF_TILE = 256
ONES_ROWS = 16
MLA_V_ROWS = V_HEAD + ONES_ROWS
SWA_V_ROWS = SWA_HEAD_DIM + ONES_ROWS
VMEM_LIMIT = 56 * 1024 * 1024

_NT = (((1,), (1,)), ((), ()))


def _rms(x, g):
    return x * lax.rsqrt(jnp.mean(x * x, axis=-1, keepdims=True) + EPS) * g


def _swiglu(xn, wg_ref, wu_ref, wd_ref):
    ff = wg_ref.shape[1]
    wgu = jnp.concatenate([w[:, c:c + FF_TILE] for c in range(0, ff, FF_TILE) for w in (wg_ref, wu_ref)], axis=1)
    gu = jnp.dot(xn, wgu, preferred_element_type=F32)
    tiles = []
    for j in range(ff // FF_TILE):
        gate = gu[:, 2 * j * FF_TILE:(2 * j + 1) * FF_TILE]
        up = gu[:, (2 * j + 1) * FF_TILE:(2 * j + 2) * FF_TILE]
        tiles.append((gate * jax.nn.sigmoid(gate) * up).astype(BF16))
    return jnp.dot(jnp.concatenate(tiles, axis=1), wd_ref[...], preferred_element_type=F32)


def _rope_lanes(x, tab_ref):
    c = tab_ref[:, 0:LANES]
    sa = tab_ref[:, LANES:2 * LANES]
    sb = tab_ref[:, 2 * LANES:3 * LANES]
    return (x * c + pltpu.roll(x, LANES - ROPE_HALF, 1) * sa + pltpu.roll(x, ROPE_HALF, 1) * sb)


def _ffn1_proj_kernel(x_ref, xnext_ref, g1_ref, wg_ref, wu_ref, wd_ref, gmix_ref, win_ref, gqa_ref, wqbt_ref,
                      gkva_ref, wkb_ref, wvbt_ref, ropeq_ref, ropek_ref,
                      h1_ref, qm_ref, km_ref, vm_ref, qs_ref, ks_ref, vs_ref, xn_ref, h1s_ref, *, q_scale):
    s = pl.program_id(0)
    n_tiles = pl.num_programs(0) - 1
    up_project = functools.partial(_up_projection_stage, gqa_ref, wqbt_ref, gkva_ref, wkb_ref, wvbt_ref,
                                   ropeq_ref, ropek_ref, qm_ref, km_ref, vm_ref, qs_ref, ks_ref, vs_ref,
                                   q_scale=q_scale)

    def project(h1):
        u = _rms(h1, gmix_ref[...]).astype(BF16)
        return jnp.dot(u, win_ref[...], preferred_element_type=F32)

    @pl.when(s == 0)
    def _():
        xn_ref[...] = _rms(x_ref[...], g1_ref[...]).astype(BF16)
        h1s_ref[...] = jnp.zeros_like(h1s_ref)

    @pl.when(s < n_tiles)
    def _():
        gate = jnp.dot(xn_ref[...], wg_ref[...], preferred_element_type=F32)
        proj = project(h1s_ref[...])
        up = jnp.dot(xn_ref[...], wu_ref[...], preferred_element_type=F32)
        up_project(proj)
        act = (gate * jax.nn.sigmoid(gate) * up).astype(BF16)
        h1 = x_ref[...] + 0.5 * jnp.dot(act, wd_ref[...], preferred_element_type=F32)
        h1_ref[...] = h1
        h1s_ref[...] = h1
        xn_ref[...] = _rms(xnext_ref[...], g1_ref[...]).astype(BF16)

    @pl.when(s == n_tiles)
    def _():
        up_project(project(h1s_ref[...]))


def _up_projection_stage(gqa_ref, wqbt_ref, gkva_ref, wkb_ref, wvbt_ref, ropeq_ref, ropek_ref,
                         qm_ref, km_ref, vm_ref, qs_ref, ks_ref, vs_ref, proj, *, q_scale):
    tm = proj.shape[0]

    cq = _rms(proj[:, P_CQ:P_CQ + Q_LORA], gqa_ref[...]).astype(BF16)
    qt = lax.dot_general(wqbt_ref[...], cq, _NT, preferred_element_type=F32)
    cs = ropeq_ref[0:ROPE_HALF, :]
    sn = ropeq_ref[ROPE_HALF:QK_ROPE, :]
    for h in range(MLA_HEADS):
        r = h * LANES
        x1 = qt[r + ROPE_LO:r + ROPE_LO + ROPE_HALF]
        x2 = qt[r + ROPE_LO + ROPE_HALF:r + ROPE_LO + QK_ROPE]
        qh = jnp.concatenate([qt[r:r + QK_NOPE] * q_scale, x1 * cs - x2 * sn, x2 * cs + x1 * sn,
                              qt[r + ROPE_LO + QK_ROPE:r + LANES]], axis=0).astype(BF16)
        for t in range(tm // MLA_TILE):
            qm_ref[t, r:r + LANES, :] = qh[:, t * MLA_TILE:(t + 1) * MLA_TILE]

    ckv = _rms(proj[:, P_CKV:P_CKV + KV_LORA], gkva_ref[...]).astype(BF16)
    kn = jnp.dot(ckv, wkb_ref[...], preferred_element_type=F32)
    kpe = _rope_lanes(proj[:, P_KPE:P_KPE + LANES], ropek_ref)
    for h in range(MLA_HEADS):
        km_ref[:, h * LANES:(h + 1) * LANES] = (kn[:, h * LANES:(h + 1) * LANES] + kpe).astype(BF16)
    vt = lax.dot_general(wvbt_ref[...], ckv, _NT, preferred_element_type=F32).astype(BF16)
    ones = jnp.ones((ONES_ROWS, tm), BF16)
    vt = jnp.concatenate([blk for h in range(MLA_HEADS) for blk in (vt[h * V_HEAD:(h + 1) * V_HEAD], ones)],
                         axis=0)
    for t in range(tm // MLA_TILE):
        vm_ref[t] = vt[:, t * MLA_TILE:(t + 1) * MLA_TILE]

    qst = (proj[:, P_QS:P_QS + SWA_OUT] * (SWA_HEAD_DIM ** -0.5 * LOG2E)).T.astype(BF16)
    vst = proj[:, P_VS:P_VS + SWA_KV].T.astype(BF16)
    vst = jnp.concatenate([blk for h in range(SWA_KV_HEADS)
                           for blk in (vst[h * SWA_HEAD_DIM:(h + 1) * SWA_HEAD_DIM], ones)], axis=0)
    for t in range(tm // WINDOW):
        qs_ref[t] = qst[:, t * WINDOW:(t + 1) * WINDOW]
        vs_ref[t] = vst[:, t * WINDOW:(t + 1) * WINDOW]
    for h in range(SWA_KV_HEADS):
        ks_ref[:, h * LANES:h * LANES + SWA_HEAD_DIM] = (
            proj[:, P_KS + h * SWA_HEAD_DIM:P_KS + (h + 1) * SWA_HEAD_DIM].astype(BF16))
        ks_ref[:, h * LANES + SWA_HEAD_DIM:(h + 1) * LANES] = jnp.zeros((tm, LANES - SWA_HEAD_DIM), BF16)


def _mla_kernel(q_ref, k_ref, v_ref, o_ref, s_ref, p_ref):
    nt, rows, t = q_ref.shape
    hp = rows // LANES
    vr = v_ref.shape[1] // hp
    key = lax.broadcasted_iota(jnp.int32, (t, t), 0)
    qry = lax.broadcasted_iota(jnp.int32, (t, t), 1)
    causal = key <= qry

    def issue(i, j, hh):
        return jnp.dot(k_ref[j * t:(j + 1) * t, hh * LANES:(hh + 1) * LANES],
                       q_ref[i, hh * LANES:(hh + 1) * LANES, :], preferred_element_type=F32)

    def stash(s, masked, hh):
        if masked:
            s = jnp.where(causal, s, NEG)
        s_ref[hh] = s
        return jnp.max(s, axis=0, keepdims=True)

    def fold_values(jp, st, hh):
        m, acc, alpha = st
        pv = jnp.dot(v_ref[jp, hh * vr:(hh + 1) * vr, :], p_ref[hh], preferred_element_type=F32)
        return m, alpha * acc + pv, alpha

    def softmax(tile_max, st, hh):
        m, acc, _ = st
        m_new = jnp.maximum(m, tile_max)
        p_ref[hh] = jnp.exp2(s_ref[hh] - m_new).astype(BF16)
        return m_new, acc, jnp.exp2(m - m_new)

    def finish(i, st, hh):
        acc = st[1]
        o_ref[i, hh * V_HEAD:(hh + 1) * V_HEAD, :] = acc[0:V_HEAD] / acc[V_HEAD:V_HEAD + 1]

    fresh = (jnp.full((1, t), NEG, F32), jnp.zeros((vr, t), F32), jnp.zeros((1, t), F32))

    tiles = [(i, j) for i in range(nt) for j in range(i + 1)]
    tile_max = [stash(issue(0, 0, hh), True, hh) for hh in range(hp)]
    state = [fresh] * hp
    for n, (i, j) in enumerate(tiles):
        prev = tiles[n - 1] if n > 0 else None
        nxt = tiles[n + 1] if n + 1 < len(tiles) else None
        if nxt is not None:
            sc = [issue(*nxt, hh) for hh in range(hp)]
        for hh in range(hp):
            st = state[hh]
            if prev is not None:
                st = fold_values(prev[1], st, hh)
                if prev[0] != i:
                    finish(prev[0], st, hh)
                    st = fresh
            state[hh] = softmax(tile_max[hh], st, hh)
            if nxt is not None:
                tile_max[hh] = stash(sc[hh], nxt[0] == nxt[1], hh)
    for hh in range(hp):
        finish(nt - 1, fold_values(nt - 1, state[hh], hh), hh)


def _swa_kernel(bucket_ref, relb_ref, sink_ref, q_ref, k_ref, v_ref, o_ref, bias_ref, sinkrow_ref,
                s_ref, p_ref):
    w, dh, g = WINDOW, SWA_HEAD_DIM, SWA_GROUP
    nb = q_ref.shape[0]
    vr = v_ref.shape[1] // SWA_KV_HEADS

    @pl.when(pl.program_id(0) == 0)
    def _():
        bucket = bucket_ref[...]
        for kvh in range(SWA_KV_HEADS):
            for gg in range(g):
                h = kvh * g + gg
                b = jnp.zeros((w, w), F32)
                for r in range(REL_BUCKETS):
                    b = jnp.where(bucket == r, relb_ref[r, h] * LOG2E, b)
                bias_ref[kvh, :, gg * w:(gg + 1) * w] = b
                sinkrow_ref[kvh, :, gg * w:(gg + 1) * w] = jnp.full((1, w), sink_ref[h] * LOG2E, F32)

    chains = s_ref.shape[0]
    hu = s_ref.shape[2] // w
    row = lax.broadcasted_iota(jnp.int32, (w, w), 0)
    qry = lax.broadcasted_iota(jnp.int32, (w, w), 1)
    from_prev = jnp.concatenate([row > qry] * hu, axis=1)

    def key_rows(n):
        return (0, w) if n == 0 else ((n - 1) * w, 2 * w)

    def heads(kvh, part):
        return [kvh * g + part * hu + i for i in range(hu)]

    def cols(part):
        return slice(part * hu * w, (part + 1) * hu * w)

    def issue(n, kvh, part):
        qt = jnp.concatenate([q_ref[n, h * dh:(h + 1) * dh, :] for h in heads(kvh, part)], axis=1)
        start, size = key_rows(n)
        k = k_ref[start:start + size, kvh * LANES:kvh * LANES + dh]
        return jnp.dot(k, qt, preferred_element_type=F32)

    def stash(sc, c, n, kvh, part):
        bias = bias_ref[kvh, :, cols(part)]
        if n == 0:
            s = jnp.where(from_prev, NEG, sc + bias)
        else:
            s = jnp.where(from_prev, sc[0:w], sc[w:2 * w]) + bias
        s_ref[c] = s
        return jnp.max(s, axis=0, keepdims=True)

    def softmax(tile_max, c, n, kvh, part):
        sink = sinkrow_ref[kvh, :, cols(part)]
        m = jnp.maximum(tile_max, sink)
        p = jnp.exp2(s_ref[c] - m)
        if n == 0:
            p_ref[c, w:2 * w, :] = p.astype(BF16)
        else:
            p_ref[c, 0:w, :] = jnp.where(from_prev, p, 0.0).astype(BF16)
            p_ref[c, w:2 * w, :] = jnp.where(from_prev, 0.0, p).astype(BF16)
        return jnp.exp2(sink - m)

    def fold_values(sink_term, c, n, kvh, part):
        vrows = slice(kvh * vr, (kvh + 1) * vr)
        if n == 0:
            vwin_t = v_ref[0, vrows, :]
        else:
            vwin_t = jnp.concatenate([v_ref[n - 1, vrows, :], v_ref[n, vrows, :]], axis=1)
        lo = 2 * w - key_rows(n)[1]
        pv = jnp.dot(vwin_t, p_ref[c, lo:, :], preferred_element_type=F32)
        o = pv[0:dh] / (pv[dh:dh + 1] + sink_term)
        for i, h in enumerate(heads(kvh, part)):
            o_ref[n, h * dh:(h + 1) * dh, :] = o[:, i * w:(i + 1) * w]

    units = [(n, kvh, part) for n in range(nb) for kvh in range(SWA_KV_HEADS) for part in range(g // hu)]
    steps = [units[i:i + chains] for i in range(0, len(units), chains)]
    tile_max = [stash(issue(*u), c, *u) for c, u in enumerate(steps[0])]
    sink_term = [None] * chains
    for idx, step in enumerate(steps):
        nxt = steps[idx + 1] if idx + 1 < len(steps) else None
        if nxt is not None:
            sc = [issue(*u) for u in nxt]
        if idx > 0:
            for c, u in enumerate(steps[idx - 1]):
                fold_values(sink_term[c], c, *u)
        for c, u in enumerate(step):
            sink_term[c] = softmax(tile_max[c], c, *u)
        if nxt is not None:
            for c, u in enumerate(nxt):
                tile_max[c] = stash(sc[c], c, *u)
    for c, u in enumerate(steps[-1]):
        fold_values(sink_term[c], c, *u)


def _attn_features(om_ref, os_ref, gom_ref, gos_ref):
    om = jnp.concatenate([om_ref[t].T for t in range(om_ref.shape[0])], axis=0)
    osw = jnp.concatenate([os_ref[t].T for t in range(os_ref.shape[0])], axis=0)
    return _rms(om, gom_ref[...]).astype(BF16), _rms(osw, gos_ref[...]).astype(BF16)


def _out_ffn2_kernel(h1_ref, om_next_ref, os_next_ref, om_first_ref, os_first_ref, gom_ref, gos_ref, wom_ref,
                     wos_ref, g2_ref, wg_ref, wu_ref, wd_ref, gfin_ref, out_ref, am_ref, as_ref, h3_ref):
    s = pl.program_id(0)
    n_tiles = pl.num_programs(0) - 1

    @pl.when(s == 0)
    def _():
        am_ref[...], as_ref[...] = _attn_features(om_first_ref, os_first_ref, gom_ref, gos_ref)
        h3_ref[...] = jnp.zeros_like(h3_ref)

    @pl.when(s < n_tiles)
    def _():
        half = h1_ref.shape[0] // 2
        h2_parts, hn_parts = [], []
        for r in (slice(0, half), slice(half, 2 * half)):
            part = (h1_ref[r, :] + jnp.dot(am_ref[r, :], wom_ref[...], preferred_element_type=F32)
                    + jnp.dot(as_ref[r, :], wos_ref[...], preferred_element_type=F32))
            h2_parts.append(part)
            hn_parts.append(_rms(part, g2_ref[...]).astype(BF16))
        h2 = jnp.concatenate(h2_parts, axis=0)
        hn = jnp.concatenate(hn_parts, axis=0)
        out_ref[...] = _rms(h3_ref[...], gfin_ref[...])
        am_ref[...], as_ref[...] = _attn_features(om_next_ref, os_next_ref, gom_ref, gos_ref)
        h3_ref[...] = h2 + 0.5 * _swiglu(hn, wg_ref, wu_ref, wd_ref)

    @pl.when(s == n_tiles)
    def _():
        out_ref[...] = _rms(h3_ref[...], gfin_ref[...])


def _pack_w_in(w_in):
    o_kpe = Q_LORA + KV_LORA
    o_qs = o_kpe + QK_ROPE
    kpe = jnp.pad(w_in[:, o_kpe:o_qs], ((0, 0), (ROPE_LO, LANES - ROPE_LO - QK_ROPE)))
    return jnp.concatenate([w_in[:, :o_kpe], kpe, w_in[:, o_qs:]], axis=1)


def _pack_w_q_b_t(w_q_b):
    r = w_q_b.shape[0]
    w = w_q_b.reshape(r, MLA_HEADS, QK_NOPE + QK_ROPE)
    w = jnp.pad(w, ((0, 0), (0, 0), (0, LANES - QK_NOPE - QK_ROPE)))
    return w.reshape(r, MLA_HEADS * LANES).T


def _pack_w_kv_b(w_kv_b):
    r = w_kv_b.shape[0]
    w = w_kv_b.reshape(r, MLA_HEADS, QK_NOPE + V_HEAD)
    wk = jnp.pad(w[:, :, :QK_NOPE], ((0, 0), (0, 0), (0, LANES - QK_NOPE))).reshape(r, MLA_HEADS * LANES)
    wv_t = w[:, :, QK_NOPE:].reshape(r, MLA_HEADS * V_HEAD).T
    return wk, wv_t


def _rope_angles(seq):
    pos = jnp.arange(seq, dtype=F32)
    inv_freq = ROPE_THETA ** (-jnp.arange(0, QK_ROPE, 2, dtype=F32) / QK_ROPE)
    ang = pos[:, None] * inv_freq[None, :]
    return jnp.cos(ang), jnp.sin(ang)


def _rope_table_lanes(seq):
    cos, sin = _rope_angles(seq)
    z = lambda n: jnp.zeros((seq, n), F32)
    hi = LANES - ROPE_LO - QK_ROPE
    c = [z(ROPE_LO), cos, cos, z(hi)]
    sa = [z(ROPE_LO), -sin, z(ROPE_HALF + hi)]
    sb = [z(ROPE_LO + ROPE_HALF), sin, z(hi)]
    return jnp.concatenate(c + sa + sb, axis=1)


def _t5_bucket(dist):
    n = jnp.maximum(dist, 0)
    max_exact = REL_BUCKETS // 2
    nf = jnp.maximum(n, 1).astype(F32)
    large = max_exact + (jnp.log(nf / max_exact) / math.log(REL_MAX_DIST / max_exact)
                         * (REL_BUCKETS - max_exact)).astype(jnp.int32)
    large = jnp.minimum(large, REL_BUCKETS - 1)
    return jnp.where(n < max_exact, n, large)


def _const_spec(shape):
    nd = len(shape)
    return pl.BlockSpec(shape, lambda *_: (0,) * nd, pipeline_mode=pl.Buffered(1))


def kernel(x, g_ffn1, w_ffn1_gate, w_ffn1_up, w_ffn1_down, g_mix, w_in, g_q_a, w_q_b, g_kv_a, w_kv_b,
           attn_sinks, rel_bias, g_out_mla, g_out_swa, w_o, g_ffn2, w_ffn2_gate, w_ffn2_up, w_ffn2_down,
           g_final):
    bsz, seq, d = x.shape
    n_tok = bsz * seq
    tm = TOKEN_TILE
    assert d == D_MODEL and seq % tm == 0 and tm % MLA_TILE == 0 and tm % WINDOW == 0
    assert g_ffn1.shape[0] == 1, "single layer"

    row = lambda a: a.reshape(1, -1).astype(F32)
    xf = x.reshape(n_tok, d)

    win = _pack_w_in(w_in[0].astype(BF16))
    wqbt = _pack_w_q_b_t(w_q_b[0].astype(BF16))
    wkb, wvbt = _pack_w_kv_b(w_kv_b[0].astype(BF16))
    q_scale = (QK_NOPE + QK_ROPE) ** -0.5 * LOG2E
    cos, sin = _rope_angles(seq)
    rope_q = jnp.concatenate([cos.T, sin.T], axis=0) * q_scale
    rope_k = _rope_table_lanes(seq)

    tiles = n_tok // tm
    tps = seq // tm
    nt_mla, nb_swa = seq // MLA_TILE, seq // WINDOW

    weights1 = [row(g_ffn1[0]), w_ffn1_gate[0].astype(BF16), w_ffn1_up[0].astype(BF16),
                w_ffn1_down[0].astype(BF16), row(g_mix[0]), win, row(g_q_a[0]), wqbt, row(g_kv_a[0]), wkb, wvbt]
    last = tiles - 1
    cur = lambda s: jnp.minimum(s, last)
    prv = lambda s: jnp.maximum(s - 1, 0)
    tok_spec = lambda width, tile_of: pl.BlockSpec((tm, width), lambda s: (tile_of(s), 0))
    fm_spec = lambda rows, tok: pl.BlockSpec(
        (None, tm // tok, rows, tok), lambda s: (prv(s) // tps, prv(s) % tps, 0, 0))
    h1, q_mla, k_mla, v_mla, q_swa, k_swa, v_swa = pl.pallas_call(
        functools.partial(_ffn1_proj_kernel, q_scale=q_scale),
        grid=(tiles + 1,),
        in_specs=[tok_spec(d, cur), tok_spec(d, lambda s: jnp.minimum(s + 1, last))]
                 + [_const_spec(a.shape) for a in weights1]
                 + [pl.BlockSpec((QK_ROPE, tm), lambda s: (0, prv(s) % tps)),
                    pl.BlockSpec((tm, 3 * LANES), lambda s: (prv(s) % tps, 0))],
        out_specs=[tok_spec(d, cur), fm_spec(MLA_HEADS * LANES, MLA_TILE), tok_spec(MLA_HEADS * LANES, prv),
                   fm_spec(MLA_HEADS * MLA_V_ROWS, MLA_TILE), fm_spec(SWA_OUT, WINDOW),
                   tok_spec(SWA_KV_HEADS * LANES, prv),
                   fm_spec(SWA_KV_HEADS * SWA_V_ROWS, WINDOW)],
        scratch_shapes=[pltpu.VMEM((tm, d), BF16), pltpu.VMEM((tm, d), F32)],
        out_shape=[jax.ShapeDtypeStruct((n_tok, d), F32),
                   jax.ShapeDtypeStruct((bsz, nt_mla, MLA_HEADS * LANES, MLA_TILE), BF16),
                   jax.ShapeDtypeStruct((n_tok, MLA_HEADS * LANES), BF16),
                   jax.ShapeDtypeStruct((bsz, nt_mla, MLA_HEADS * MLA_V_ROWS, MLA_TILE), BF16),
                   jax.ShapeDtypeStruct((bsz, nb_swa, SWA_OUT, WINDOW), BF16),
                   jax.ShapeDtypeStruct((n_tok, SWA_KV_HEADS * LANES), BF16),
                   jax.ShapeDtypeStruct((bsz, nb_swa, SWA_KV_HEADS * SWA_V_ROWS, WINDOW), BF16)],
        compiler_params=pltpu.CompilerParams(dimension_semantics=("arbitrary",),
                                             vmem_limit_bytes=VMEM_LIMIT),
        name="ffn1_proj",
    )(xf, xf, *weights1, rope_q, rope_k)

    hp = MLA_HEADS_PER_STEP
    o_mla = pl.pallas_call(
        _mla_kernel,
        grid=(bsz, MLA_HEADS // hp),
        in_specs=[pl.BlockSpec((None, nt_mla, hp * LANES, MLA_TILE), lambda b, p: (b, 0, p, 0)),
                  pl.BlockSpec((seq, hp * LANES), lambda b, p: (b, p)),
                  pl.BlockSpec((None, nt_mla, hp * MLA_V_ROWS, MLA_TILE), lambda b, p: (b, 0, p, 0))],
        out_specs=pl.BlockSpec((None, nt_mla, hp * V_HEAD, MLA_TILE), lambda b, p: (b, 0, p, 0)),
        out_shape=jax.ShapeDtypeStruct((bsz, nt_mla, MLA_OUT, MLA_TILE), F32),
        scratch_shapes=[pltpu.VMEM((hp, MLA_TILE, MLA_TILE), F32), pltpu.VMEM((hp, MLA_TILE, MLA_TILE), BF16)],
        compiler_params=pltpu.CompilerParams(dimension_semantics=("arbitrary", "arbitrary"),
                                             vmem_limit_bytes=VMEM_LIMIT),
        name="mla",
    )(q_mla, k_mla, v_mla)

    kj = jnp.arange(WINDOW)[:, None]
    qi = jnp.arange(WINDOW)[None, :]
    bucket_t = _t5_bucket(jnp.where(kj > qi, qi + WINDOW - kj, qi - kj)).astype(jnp.int32)
    smem = pl.BlockSpec(memory_space=pltpu.SMEM)
    o_swa = pl.pallas_call(
        _swa_kernel,
        grid=(bsz,),
        in_specs=[_const_spec(bucket_t.shape), smem, smem,
                  pl.BlockSpec((None, nb_swa, SWA_OUT, WINDOW), lambda b: (b, 0, 0, 0)),
                  pl.BlockSpec((seq, SWA_KV_HEADS * LANES), lambda b: (b, 0)),
                  pl.BlockSpec((None, nb_swa, SWA_KV_HEADS * SWA_V_ROWS, WINDOW), lambda b: (b, 0, 0, 0))],
        out_specs=pl.BlockSpec((None, nb_swa, SWA_OUT, WINDOW), lambda b: (b, 0, 0, 0)),
        out_shape=jax.ShapeDtypeStruct((bsz, nb_swa, SWA_OUT, WINDOW), F32),
        scratch_shapes=[pltpu.VMEM((SWA_KV_HEADS, WINDOW, SWA_GROUP * WINDOW), F32),
                        pltpu.VMEM((SWA_KV_HEADS, 1, SWA_GROUP * WINDOW), F32),
                        pltpu.VMEM((SWA_UNITS_PER_STEP, WINDOW, SWA_HEADS_PER_UNIT * WINDOW), F32),
                        pltpu.VMEM((SWA_UNITS_PER_STEP, 2 * WINDOW, SWA_HEADS_PER_UNIT * WINDOW), BF16)],
        compiler_params=pltpu.CompilerParams(dimension_semantics=("arbitrary",),
                                             vmem_limit_bytes=VMEM_LIMIT),
        name="swa",
    )(bucket_t, rel_bias.astype(F32), attn_sinks[0].astype(F32), q_swa, k_swa, v_swa)

    wo = w_o[0].astype(BF16)
    weights4 = [row(g_out_mla[0]), row(g_out_swa[0]), wo[:MLA_OUT], wo[MLA_OUT:], row(g_ffn2[0]),
                w_ffn2_gate[0].astype(BF16), w_ffn2_up[0].astype(BF16), w_ffn2_down[0].astype(BF16),
                row(g_final)]
    nxt = lambda s: jnp.minimum(s + 1, last)
    fm_next = lambda rows, tok: pl.BlockSpec(
        (None, tm // tok, rows, tok), lambda s: (nxt(s) // tps, nxt(s) % tps, 0, 0))
    fm_first = lambda rows, tok: pl.BlockSpec((None, tm // tok, rows, tok), lambda s: (0, 0, 0, 0),
                                              pipeline_mode=pl.Buffered(1))
    out = pl.pallas_call(
        _out_ffn2_kernel,
        grid=(tiles + 1,),
        in_specs=[pl.BlockSpec((tm, d), lambda s: (jnp.minimum(s, last), 0)),
                  fm_next(MLA_OUT, MLA_TILE), fm_next(SWA_OUT, WINDOW),
                  fm_first(MLA_OUT, MLA_TILE), fm_first(SWA_OUT, WINDOW)]
                 + [_const_spec(a.shape) for a in weights4],
        out_specs=pl.BlockSpec((tm, d), lambda s: (jnp.maximum(s - 1, 0), 0)),
        out_shape=jax.ShapeDtypeStruct((n_tok, d), F32),
        scratch_shapes=[pltpu.VMEM((tm, MLA_OUT), BF16), pltpu.VMEM((tm, SWA_OUT), BF16),
                        pltpu.VMEM((tm, d), F32)],
        compiler_params=pltpu.CompilerParams(dimension_semantics=("arbitrary",),
                                             vmem_limit_bytes=VMEM_LIMIT),
        name="out_ffn2",
    )(h1, o_mla, o_swa, o_mla, o_swa, *weights4)
    return out.reshape(bsz, seq, d)
```
